```python
import math
import jax, jax.numpy as jnp
from jax import lax
import numpy as np

D_MODEL = 1024
BATCH = 8
SEQ = 4096
DEPTH = 4
DEC_BATCH = 4
DEC_SEQ = 8192
PAST_LEN = 128

N_HEADS = 4
HEAD_DIM = 64
V_DIM = 2 * HEAD_DIM
D_QK = N_HEADS * 2 * HEAD_DIM
D_ATTN = N_HEADS * V_DIM
ROPE_THETA = 10000.0
Q_BLOCK = 128
D_SSM = 512
GROUP_CH = 16
N_GROUPS = D_SSM // GROUP_CH
N_STATE = 64
DT_MIN = 0.001
DT_MAX = 0.1
D_FF = -(-8 * D_MODEL // (3 * 256)) * 256
EPS = 1e-6

OFF_Q = 0
OFF_K = OFF_Q + D_QK
OFF_V = OFF_K + D_QK
OFF_U = OFF_V + D_ATTN
OFF_GA = OFF_U + D_SSM
OFF_GS = OFF_GA + D_MODEL
IN_COLS = OFF_GS + D_MODEL

kernel_name = 'hybrid_diffattn_s5_encoder'


def rms_norm(x, g):
    x32 = x.astype(jnp.float32)
    y = x32 * lax.rsqrt(jnp.mean(x32 * x32, axis=-1, keepdims=True) + EPS)
    return (y * g.astype(jnp.float32)).astype(x.dtype)


def rope_tables(seq_len):
    inv = 1.0 / (ROPE_THETA ** (jnp.arange(0, HEAD_DIM, 2, dtype=jnp.float32) / HEAD_DIM))
    ang = jnp.arange(seq_len, dtype=jnp.float32)[:, None] * inv[None, :]
    ang = jnp.concatenate([ang, ang], axis=-1)
    return jnp.cos(ang), jnp.sin(ang)


def apply_rope(x, cos, sin):
    x1, x2 = jnp.split(x, 2, axis=-1)
    rot = jnp.concatenate([-x2, x1], axis=-1)
    c = cos[None, :, None, None, :]
    s = sin[None, :, None, None, :]
    return (x.astype(jnp.float32) * c + rot.astype(jnp.float32) * s).astype(x.dtype)


def lambda_init_fn(layer):
    return 0.8 - 0.6 * math.exp(-0.3 * layer)


def diff_attention(q, k, v, lam):
    b, l, h, _, d = q.shape
    nb = l // Q_BLOCK
    scale = HEAD_DIM ** -0.5
    qb = q.reshape(b, nb, Q_BLOCK, h, 2, d).transpose(1, 0, 2, 3, 4, 5)

    def one_block(q_blk):
        s = jnp.einsum('bqhmd,bkhmd->bhmqk', q_blk, k).astype(jnp.float32) * scale
        p = jax.nn.softmax(s, axis=-1)
        a = p[:, :, 0] - lam * p[:, :, 1]
        return jnp.einsum('bhqk,bkhe->bqhe', a.astype(v.dtype), v)

    out = lax.map(one_block, qb)
    return out.transpose(1, 0, 2, 3, 4).reshape(b, l, h, V_DIM)


def zoh(a_re, a_im, log_dt, b_re, b_im):
    dt = jnp.exp(log_dt.astype(jnp.float32))[:, None]
    ar = a_re.astype(jnp.float32)
    ai = a_im.astype(jnp.float32)
    mag = jnp.exp(dt * ar)
    abr = mag * jnp.cos(dt * ai)
    abi = mag * jnp.sin(dt * ai)
    nr = abr - 1.0
    ni = abi
    den = ar * ar + ai * ai
    fr = (nr * ar + ni * ai) / den
    fi = (ni * ar - nr * ai) / den
    br = b_re.astype(jnp.float32)
    bi = b_im.astype(jnp.float32)
    bbr = fr[..., None] * br - fi[..., None] * bi
    bbi = fr[..., None] * bi + fi[..., None] * br
    return abr, abi, bbr, bbi


def complex_affine_combine(e1, e2):
    a1r, a1i, b1r, b1i = e1
    a2r, a2i, b2r, b2i = e2
    return (a1r * a2r - a1i * a2i,
            a1r * a2i + a1i * a2r,
            a2r * b1r - a2i * b1i + b2r,
            a2r * b1i + a2i * b1r + b2i)


def ssm_direction(u32, a_re, a_im, log_dt, b_re, b_im, c_re, c_im, reverse):
    abr, abi, bbr, bbi = zoh(a_re, a_im, log_dt, b_re, b_im)
    bur = jnp.einsum('blgc,gpc->blgp', u32, bbr)
    bui = jnp.einsum('blgc,gpc->blgp', u32, bbi)
    shape = (1, u32.shape[1]) + abr.shape
    ar_t = jnp.broadcast_to(abr, shape)
    ai_t = jnp.broadcast_to(abi, shape)
    _, _, hr, hi = lax.associative_scan(complex_affine_combine, (ar_t, ai_t, bur, bui),
                                        reverse=reverse, axis=1)
    return (jnp.einsum('blgp,gcp->blgc', hr, c_re.astype(jnp.float32))
            - jnp.einsum('blgp,gcp->blgc', hi, c_im.astype(jnp.float32)))


def ssm_branch(u, a_re, a_im, log_dt, b_re, b_im, c_re, c_im, d_skip, w_glu, b_glu):
    b, l, _ = u.shape
    u32 = u.astype(jnp.float32)
    ug = u32.reshape(b, l, N_GROUPS, GROUP_CH)
    y = (ssm_direction(ug, a_re[0], a_im[0], log_dt[0], b_re[0], b_im[0], c_re[0], c_im[0], False)
         + ssm_direction(ug, a_re[1], a_im[1], log_dt[1], b_re[1], b_im[1], c_re[1], c_im[1], True))
    y = y.reshape(b, l, D_SSM) + d_skip.astype(jnp.float32) * u32
    z = jax.nn.gelu(y).astype(u.dtype)
    lin, gate = jnp.split(z @ w_glu + b_glu, 2, axis=-1)
    return lin * jax.nn.sigmoid(gate)


def layer(x, c, cos, sin, lam_init, w_mod, b_mod, norm1_g, w_in, lam_q1, lam_k1, lam_q2, lam_k2,
          subln_g, w_attn_br, ssm_a_re, ssm_a_im, ssm_log_dt, ssm_b_re, ssm_b_im, ssm_c_re, ssm_c_im,
          ssm_d, w_glu, b_glu, w_o, norm2_g, w_ffn_in, w_ffn_out):
    b, l, _ = x.shape
    mod = jax.nn.silu(c) @ w_mod + b_mod
    sh1, sc1, gt1, sh2, sc2, gt2 = jnp.split(mod[:, None, :], 6, axis=-1)

    h = rms_norm(x, norm1_g) * (1 + sc1) + sh1
    proj = h @ w_in
    q = proj[..., OFF_Q:OFF_K].reshape(b, l, N_HEADS, 2, HEAD_DIM)
    k = proj[..., OFF_K:OFF_V].reshape(b, l, N_HEADS, 2, HEAD_DIM)
    v = proj[..., OFF_V:OFF_U].reshape(b, l, N_HEADS, V_DIM)
    u = proj[..., OFF_U:OFF_GA]
    g_a = proj[..., OFF_GA:OFF_GS]
    g_s = proj[..., OFF_GS:IN_COLS]

    q = apply_rope(q, cos, sin)
    k = apply_rope(k, cos, sin)
    lam = (jnp.exp(jnp.sum(lam_q1.astype(jnp.float32) * lam_k1.astype(jnp.float32)))
           - jnp.exp(jnp.sum(lam_q2.astype(jnp.float32) * lam_k2.astype(jnp.float32)))
           + lam_init)
    o = diff_attention(q, k, v, lam)
    o = rms_norm(o, subln_g) * (1.0 - lam_init)
    y_attn = o.reshape(b, l, D_ATTN) @ w_attn_br

    y_ssm = ssm_branch(u, ssm_a_re, ssm_a_im, ssm_log_dt, ssm_b_re, ssm_b_im, ssm_c_re, ssm_c_im,
                       ssm_d, w_glu, b_glu)

    merged = jax.nn.sigmoid(g_a) * y_attn + jax.nn.sigmoid(g_s) * y_ssm
    x = x + gt1 * (merged @ w_o)

    h2 = rms_norm(x, norm2_g) * (1 + sc2) + sh2
    f_gate, f_up = jnp.split(h2 @ w_ffn_in, 2, axis=-1)
    x = x + gt2 * ((jax.nn.silu(f_gate) * f_up) @ w_ffn_out)
    return x


def trunk(x, c, w_mod, b_mod, norm1_g, w_in, lam_q1, lam_k1, lam_q2, lam_k2, subln_g, w_attn_br,
          ssm_a_re, ssm_a_im, ssm_log_dt, ssm_b_re, ssm_b_im, ssm_c_re, ssm_c_im, ssm_d, w_glu, b_glu,
          w_o, norm2_g, w_ffn_in, w_ffn_out, final_g):
    cos, sin = rope_tables(x.shape[1])
    for i in range(DEPTH):
        x = layer(x, c, cos, sin, lambda_init_fn(i), w_mod[i], b_mod[i], norm1_g[i], w_in[i],
                  lam_q1[i], lam_k1[i], lam_q2[i], lam_k2[i], subln_g[i], w_attn_br[i],
                  ssm_a_re[i], ssm_a_im[i], ssm_log_dt[i], ssm_b_re[i], ssm_b_im[i],
                  ssm_c_re[i], ssm_c_im[i], ssm_d[i], w_glu[i], b_glu[i], w_o[i], norm2_g[i],
                  w_ffn_in[i], w_ffn_out[i])
    return rms_norm(x, final_g)


def setup_inputs(seed: int = 0) -> dict:
    key = jax.random.key(seed)
    ks = jax.random.split(key, 32)
    f32 = jnp.float32

    def nrm(k, shape, scale):
        return jax.random.normal(k, shape, f32) * scale

    n = jnp.arange(N_STATE, dtype=f32)
    ssm_a_re = -0.5 + nrm(ks[12], (DEPTH, 2, N_GROUPS, N_STATE), 0.01)
    ssm_a_im = math.pi * n + nrm(ks[13], (DEPTH, 2, N_GROUPS, N_STATE), 0.01)
    ssm_log_dt = jax.random.uniform(ks[14], (DEPTH, 2, N_GROUPS), f32,
                                    minval=math.log(DT_MIN), maxval=math.log(DT_MAX))
    return {
        'x_prompt': nrm(ks[0], (BATCH, SEQ, D_MODEL), 1.0),
        'x_sample': nrm(ks[1], (DEC_BATCH, DEC_SEQ, D_MODEL), 1.0),
        'c_prompt': nrm(ks[2], (BATCH, D_MODEL), 1.0),
        'c_sample': nrm(ks[3], (DEC_BATCH, D_MODEL), 1.0),
        'w_mod': nrm(ks[4], (DEPTH, D_MODEL, 6 * D_MODEL), 0.5 * D_MODEL ** -0.5),
        'b_mod': nrm(ks[5], (DEPTH, 6 * D_MODEL), 0.01),
        'norm1_g': 1.0 + nrm(ks[6], (DEPTH, D_MODEL), 0.02),
        'w_in': nrm(ks[7], (DEPTH, D_MODEL, IN_COLS), D_MODEL ** -0.5),
        'lam_q1': nrm(ks[8], (DEPTH, HEAD_DIM), 0.1),
        'lam_k1': nrm(ks[9], (DEPTH, HEAD_DIM), 0.1),
        'lam_q2': nrm(ks[10], (DEPTH, HEAD_DIM), 0.1),
        'lam_k2': nrm(ks[11], (DEPTH, HEAD_DIM), 0.1),
        'subln_g': 1.0 + nrm(ks[15], (DEPTH, V_DIM), 0.02),
        'w_attn_br': nrm(ks[16], (DEPTH, D_ATTN, D_MODEL), D_ATTN ** -0.5),
        'ssm_a_re': ssm_a_re,
        'ssm_a_im': ssm_a_im,
        'ssm_log_dt': ssm_log_dt,
        'ssm_b_re': nrm(ks[17], (DEPTH, 2, N_GROUPS, N_STATE, GROUP_CH), (2 * GROUP_CH) ** -0.5),
        'ssm_b_im': nrm(ks[18], (DEPTH, 2, N_GROUPS, N_STATE, GROUP_CH), (2 * GROUP_CH) ** -0.5),
        'ssm_c_re': nrm(ks[19], (DEPTH, 2, N_GROUPS, GROUP_CH, N_STATE), (2 * N_STATE) ** -0.5),
        'ssm_c_im': nrm(ks[20], (DEPTH, 2, N_GROUPS, GROUP_CH, N_STATE), (2 * N_STATE) ** -0.5),
        'ssm_d': nrm(ks[21], (DEPTH, D_SSM), 1.0),
        'w_glu': nrm(ks[22], (DEPTH, D_SSM, 2 * D_MODEL), D_SSM ** -0.5),
        'b_glu': nrm(ks[23], (DEPTH, 2 * D_MODEL), 0.01),
        'w_o': nrm(ks[24], (DEPTH, D_MODEL, D_MODEL), D_MODEL ** -0.5),
        'norm2_g': 1.0 + nrm(ks[25], (DEPTH, D_MODEL), 0.02),
        'w_ffn_in': nrm(ks[26], (DEPTH, D_MODEL, 2 * D_FF), D_MODEL ** -0.5),
        'w_ffn_out': nrm(ks[27], (DEPTH, D_FF, D_MODEL), D_FF ** -0.5),
        'final_g': 1.0 + nrm(ks[28], (D_MODEL,), 0.02),
    }


def reference(x_prompt, x_sample, c_prompt, c_sample, w_mod, b_mod, norm1_g, w_in,
              lam_q1, lam_k1, lam_q2, lam_k2, subln_g, w_attn_br, ssm_a_re, ssm_a_im,
              ssm_log_dt, ssm_b_re, ssm_b_im, ssm_c_re, ssm_c_im, ssm_d, w_glu, b_glu,
              w_o, norm2_g, w_ffn_in, w_ffn_out, final_g):
    params = (w_mod, b_mod, norm1_g, w_in, lam_q1, lam_k1, lam_q2, lam_k2, subln_g, w_attn_br,
              ssm_a_re, ssm_a_im, ssm_log_dt, ssm_b_re, ssm_b_im, ssm_c_re, ssm_c_im, ssm_d,
              w_glu, b_glu, w_o, norm2_g, w_ffn_in, w_ffn_out, final_g)
    y_prompt = trunk(x_prompt, c_prompt, *params)
    y_sample = trunk(x_sample, c_sample, *params)
    return (y_prompt, y_sample)
```

```python
import functools
import math

import jax
import jax.numpy as jnp
from jax import lax
from jax.experimental import pallas as pl
from jax.experimental.pallas import tpu as pltpu

F32 = jnp.float32
BF16 = jnp.bfloat16

N_HEADS = 4
HEAD_DIM = 64
V_DIM = 2 * HEAD_DIM
GROUP_CH = 16
N_STATE = 64
ROPE_THETA = 10000.0
EPS = 1e-6

VMEM_LIMIT_BYTES = 56 * 1024 * 1024
LANES = 128

TOKEN_TILE = 512
ATTN_TQ = 512
ATTN_TK = 512
SSM_CHUNK = 64
FFN_TILE = 1408
SCAN_LANES = 512


def _cparams(*sem):
    return pltpu.CompilerParams(dimension_semantics=sem, vmem_limit_bytes=VMEM_LIMIT_BYTES)


def _dot(a, b):
    return jnp.dot(a, b, preferred_element_type=F32)


def _split3(x):
    hi = x.astype(BF16)
    r1 = x - hi.astype(F32)
    mid = r1.astype(BF16)
    lo = (r1 - mid.astype(F32)).astype(BF16)
    return hi, mid, lo


def _dot_f32(a, b):
    a_hi = a.astype(BF16)
    a_lo = (a - a_hi.astype(F32)).astype(BF16)
    b_hi = b.astype(BF16)
    b_lo = (b - b_hi.astype(F32)).astype(BF16)
    return _dot(a_hi, b_hi) + (_dot(a_hi, b_lo) + _dot(a_lo, b_hi))


def _select_cols(x, sel):
    hi, mid, lo = _split3(x)
    return _dot(hi, sel) + (_dot(mid, sel) + _dot(lo, sel))


def _select_rows(sel, x):
    hi, mid, lo = _split3(x)
    return _dot(sel, hi) + (_dot(sel, mid) + _dot(sel, lo))


def _onehot(cond):
    return jnp.where(cond, 1.0, 0.0).astype(BF16)


def _rms(x):
    return x * lax.rsqrt(jnp.mean(x * x, axis=-1, keepdims=True) + EPS)


def _mod_kernel(c_ref, w_ref, b_ref, o_ref):
    c = c_ref[...]
    s = c * jax.nn.sigmoid(c)
    o_ref[0] = _dot_f32(s, w_ref[0]) + b_ref[0]


def _modulation(c_all, w_mod, b_mod):
    depth, d, n6 = w_mod.shape
    bp = c_all.shape[0]
    tn = 1536
    return pl.pallas_call(
        _mod_kernel,
        grid=(depth, n6 // tn),
        in_specs=[
            pl.BlockSpec((bp, d), lambda l, j: (0, 0)),
            pl.BlockSpec((1, d, tn), lambda l, j: (l, 0, j)),
            pl.BlockSpec((1, 1, tn), lambda l, j: (l, 0, j)),
        ],
        out_specs=pl.BlockSpec((1, bp, tn), lambda l, j: (l, 0, j)),
        out_shape=jax.ShapeDtypeStruct((depth, bp, n6), F32),
        compiler_params=_cparams("parallel", "parallel"),
        name="modulation",
    )(c_all, w_mod, b_mod.reshape(depth, 1, n6))


def _inproj_kernel(x_ref, mod_ref, g_ref, cos_ref, sin_ref, w_ref,
                   q_ref, k_ref, v_ref, u_ref, ga_ref, gs_ref):
    d = x_ref.shape[1]
    d_qk = q_ref.shape[1]
    mod = mod_ref[0]
    h = (_rms(x_ref[...]) * g_ref[...] * (1.0 + mod[1:2]) + mod[0:1]).astype(BF16)

    qk = _dot(h, w_ref[:, 0:2 * d_qk])
    reps = 2 * d_qk // LANES
    cos = jnp.tile(cos_ref[...], (1, reps))
    sin = jnp.tile(sin_ref[...], (1, reps))
    lane = lax.broadcasted_iota(jnp.int32, qk.shape, 1)
    first_half = jnp.bitwise_and(lane, HEAD_DIM // 2) == 0
    half = HEAD_DIM // 2
    rot = jnp.where(first_half,
                    pltpu.roll(qk, 2 * d_qk - half, 1),
                    pltpu.roll(qk, half, 1))
    qk = qk * cos + rot * sin
    q_ref[...] = (qk[:, :d_qk] * (HEAD_DIM ** -0.5)).astype(BF16)
    k_ref[...] = qk[:, d_qk:].astype(BF16)

    off = 2 * d_qk
    d_v = v_ref.shape[1]
    d_u = u_ref.shape[1]
    vu = _dot(h, w_ref[:, off:off + d_v + d_u])
    v_ref[...] = vu[:, :d_v].astype(BF16)
    u_ref[...] = vu[:, d_v:].astype(BF16)
    off += d_v + d_u
    ga_ref[...] = jax.nn.sigmoid(_dot(h, w_ref[:, off:off + d])).astype(BF16)
    gs_ref[...] = jax.nn.sigmoid(_dot(h, w_ref[:, off + d:off + 2 * d])).astype(BF16)


def _in_projection(x, mod, g, cos, sin, w, seq_len, d_qk, d_v, d_u):
    n, d = x.shape
    tm = min(TOKEN_TILE, seq_len)
    tps = seq_len // tm
    tok = lambda i: (i, 0)
    shapes = [(n, d_qk), (n, d_qk), (n, d_v), (n, d_u), (n, d), (n, d)]
    return pl.pallas_call(
        _inproj_kernel,
        grid=(n // tm,),
        in_specs=[
            pl.BlockSpec((tm, d), tok),
            pl.BlockSpec((1, 6, d), lambda i: (i // tps, 0, 0)),
            pl.BlockSpec((1, d), lambda i: (0, 0)),
            pl.BlockSpec((tm, LANES), lambda i: (i % tps, 0)),
            pl.BlockSpec((tm, LANES), lambda i: (i % tps, 0)),
            pl.BlockSpec(w.shape, lambda i: (0, 0)),
        ],
        out_specs=[pl.BlockSpec((tm, s[1]), tok) for s in shapes],
        out_shape=[jax.ShapeDtypeStruct(s, BF16) for s in shapes],
        compiler_params=_cparams("parallel"),
        name="in_projection",
    )(x, mod, g, cos, sin, w)


def _attn_kernel(q_ref, k_ref, v_ref, lam_ref, g_ref, o_ref, qm_ref, m_ref, l_ref, acc_ref,
                 *, lam_init):
    ki = pl.program_id(3)

    @pl.when(ki == 0)
    def _():
        q = q_ref[0].astype(F32)
        lane = lax.broadcasted_iota(jnp.int32, q.shape, 1)
        qm_ref[0] = jnp.where(lane < HEAD_DIM, q, 0.0).astype(BF16)
        qm_ref[1] = jnp.where(lane >= HEAD_DIM, q, 0.0).astype(BF16)
        m_ref[...] = jnp.full(m_ref.shape, -jnp.inf, F32)
        l_ref[...] = jnp.zeros(l_ref.shape, F32)
        acc_ref[...] = jnp.zeros(acc_ref.shape, F32)

    k = k_ref[0]
    v = v_ref[0]
    for m in range(2):
        s = lax.dot_general(qm_ref[m], k, (((1,), (1,)), ((), ())), preferred_element_type=F32)
        m_prev = m_ref[m]
        m_new = jnp.maximum(m_prev, jnp.max(s, axis=-1, keepdims=True))
        alpha = jnp.exp(m_prev - m_new)
        p = jnp.exp(s - m_new)
        l_ref[m] = alpha * l_ref[m] + jnp.sum(p, axis=-1, keepdims=True)
        acc_ref[m] = alpha * acc_ref[m] + _dot(p.astype(BF16), v)
        m_ref[m] = m_new

    @pl.when(ki == pl.num_programs(3) - 1)
    def _():
        lp = lam_ref[...]
        lam = (jnp.exp(jnp.sum(lp[0:1] * lp[1:2], axis=-1, keepdims=True))
               - jnp.exp(jnp.sum(lp[2:3] * lp[3:4], axis=-1, keepdims=True)) + lam_init)
        o = acc_ref[0] / l_ref[0] - lam * (acc_ref[1] / l_ref[1])
        o = _rms(o) * g_ref[...] * (1.0 - lam_init)
        o_ref[0] = o.astype(o_ref.dtype)


def _diff_attention(q, k, v, lam_params, subln_g, lam_init):
    b, l, _ = q.shape
    tq = min(ATTN_TQ, l)
    tk = min(ATTN_TK, l)
    return pl.pallas_call(
        functools.partial(_attn_kernel, lam_init=lam_init),
        grid=(b, N_HEADS, l // tq, l // tk),
        in_specs=[
            pl.BlockSpec((1, tq, V_DIM), lambda bi, h, qi, ki: (bi, qi, h)),
            pl.BlockSpec((1, tk, V_DIM), lambda bi, h, qi, ki: (bi, ki, h)),
            pl.BlockSpec((1, tk, V_DIM), lambda bi, h, qi, ki: (bi, ki, h)),
            pl.BlockSpec(lam_params.shape, lambda bi, h, qi, ki: (0, 0)),
            pl.BlockSpec((1, V_DIM), lambda bi, h, qi, ki: (0, 0)),
        ],
        out_specs=pl.BlockSpec((1, tq, V_DIM), lambda bi, h, qi, ki: (bi, qi, h)),
        out_shape=jax.ShapeDtypeStruct((b, l, N_HEADS * V_DIM), BF16),
        scratch_shapes=[
            pltpu.VMEM((2, tq, V_DIM), BF16),
            pltpu.VMEM((2, tq, 1), F32),
            pltpu.VMEM((2, tq, 1), F32),
            pltpu.VMEM((2, tq, V_DIM), F32),
        ],
        compiler_params=_cparams("parallel", "parallel", "parallel", "arbitrary"),
        name="diff_attention",
    )(q, k, v, lam_params, subln_g)


def _ssm_prep_kernel(arow_ref, acol_ref, bt_ref, ct_ref, m_ref, win_ref, wout_ref, dec_ref, *, chunk):
    t_len = chunk
    w = t_len * GROUP_CH
    w2 = 2 * w
    arow = arow_ref[0, 0]
    acol = acol_ref[0, 0]
    lo = lax.broadcasted_iota(jnp.int32, (1, LANES), 1) < N_STATE

    s_of_row = lax.shift_right_logical(lax.broadcasted_iota(jnp.int32, (w, t_len), 0), 4)
    j_of_col = lax.broadcasted_iota(jnp.int32, (w, t_len), 1)
    rep_rows = (_onehot(j_of_col == (t_len - 1 - s_of_row)), _onehot(j_of_col == s_of_row))
    n_rows = lax.broadcasted_iota(jnp.int32, (t_len, 1), 0).astype(F32)

    def lane_maps(width):
        lane = lax.broadcasted_iota(jnp.int32, (LANES, width), 1)
        return (lax.shift_right_logical(lane, 4), jnp.bitwise_and(lane, GROUP_CH - 1),
                lax.broadcasted_iota(jnp.int32, (LANES, width), 0))

    lag_idx, ch_idx, jrow = lane_maps(w2)
    lag_idx_w, _, jrow_w = lane_maps(w)
    tile_ch = _onehot(jrow == ch_idx)
    n_lanes = jnp.minimum(lax.broadcasted_iota(jnp.int32, (1, LANES), 1), t_len).astype(F32)

    strip = jnp.zeros((GROUP_CH, w2), F32)
    for d in range(2):
        dt = jnp.exp(arow[4 + d:5 + d])
        ar = arow[d:d + 1]
        ai = arow[2 + d:3 + d]
        zr = dt * ar
        zi = dt * ai
        mag = jnp.exp(zr)
        nr = mag * jnp.cos(zi) - 1.0
        ni = mag * jnp.sin(zi)
        den = ar * ar + ai * ai
        fr = (nr * ar + ni * ai) / den
        fi = (ni * ar - nr * ai) / den
        br = bt_ref[0, 0, d]
        bi = bt_ref[0, 0, 2 + d]
        bbr = fr * br - fi * bi
        bbi = fr * bi + fi * br

        pm = jnp.exp(n_rows * zr)
        pc = pm * jnp.cos(n_rows * zi)
        ps = pm * jnp.sin(n_rows * zi)
        e = _select_rows(rep_rows[d], jnp.where(lo, pc, ps))
        e_sw = _select_rows(rep_rows[d], jnp.where(lo, -ps, pc))
        win = e * jnp.tile(bbr, (t_len, 1)) + e_sw * jnp.tile(bbi, (t_len, 1))
        win_ref[0, 0, :, d * LANES:(d + 1) * LANES] = win.astype(win_ref.dtype)

        dm = jnp.exp(t_len * zr)
        dec_ref[0, 0, d:d + 1, :] = jnp.where(lo, dm * jnp.cos(t_len * zi), dm * jnp.sin(t_len * zi))

        dtc = jnp.exp(acol[:, 4 + d:5 + d])
        zrc = dtc * acol[:, d:d + 1]
        zic = dtc * acol[:, 2 + d:3 + d]
        ptm = jnp.exp(zrc * n_lanes)
        pt_re = ptm * jnp.cos(zic * n_lanes)
        pt_im = ptm * jnp.sin(zic * n_lanes)
        c_re = _select_cols(ct_ref[0, 0, d], tile_ch)
        c_im = _select_cols(ct_ref[0, 0, 2 + d], tile_ch)

        def table(rep):
            width = rep.shape[1]
            p_re = _select_cols(pt_re, rep)
            p_im = _select_cols(pt_im, rep)
            cr = c_re[:, :width]
            ci = c_im[:, :width]
            return p_re * cr - p_im * ci, p_re * ci + p_im * cr

        if d == 0:
            power = lag_idx - (t_len - 1)
        else:
            power = (t_len - 1) - lag_idx
        power = jnp.where(power >= 0, power, -1)
        f_re, f_im = table(_onehot(jrow == power))
        bpk = jnp.where(lo, bbr, -bbi)
        strip = strip + _dot_f32(bpk, jnp.concatenate([f_re, f_im], axis=0))

        if d == 0:
            o_re, o_im = f_re[:, w:], f_im[:, w:]
        else:
            o_re, o_im = table(_onehot(jrow_w == (t_len - lag_idx_w)))
        wout_ref[0, 0, (2 * d) * N_STATE:(2 * d + 1) * N_STATE, :] = o_re.astype(wout_ref.dtype)
        wout_ref[0, 0, (2 * d + 1) * N_STATE:(2 * d + 2) * N_STATE, :] = (-o_im).astype(wout_ref.dtype)

    per_tile = LANES // GROUP_CH
    for r in range(per_tile):
        rolled = strip if r == 0 else pltpu.roll(strip, w2 - r * GROUP_CH, 1)
        for s in range(t_len):
            if (t_len - 1 - s) % per_tile == r:
                q = (t_len - 1 - s) // per_tile
                m_ref[0, 0, s * GROUP_CH:(s + 1) * GROUP_CH, :] = (
                    rolled[:, q * LANES:q * LANES + w].astype(m_ref.dtype))


def _ssm_prep(a_re, a_im, log_dt, b_re, b_im, c_re, c_im, chunk):
    depth, _, n_groups, n_state = a_re.shape
    w = chunk * GROUP_CH
    gd = lambda x: jnp.swapaxes(x, 1, 2)
    ldt = jnp.broadcast_to(gd(log_dt)[..., None], (depth, n_groups, 2, n_state))
    rows = jnp.concatenate([gd(a_re), gd(a_im), ldt, jnp.zeros_like(ldt)], axis=2)
    arow = jnp.concatenate([rows, rows], axis=-1)
    acol = jnp.swapaxes(rows, 2, 3)
    bt = jnp.swapaxes(jnp.concatenate([gd(b_re), gd(b_im)], axis=2), 3, 4)
    bt = jnp.concatenate([bt, bt], axis=-1)
    ct = jnp.swapaxes(jnp.concatenate([gd(c_re), gd(c_im)], axis=2), 3, 4)
    ct = jnp.pad(ct, ((0, 0),) * 4 + ((0, LANES - GROUP_CH),))
    blk = lambda *tail: pl.BlockSpec((1, 1) + tail, lambda l, g: (l, g) + (0,) * len(tail))
    return pl.pallas_call(
        functools.partial(_ssm_prep_kernel, chunk=chunk),
        grid=(depth, n_groups),
        in_specs=[blk(8, LANES), blk(n_state, 8), blk(4, GROUP_CH, LANES), blk(4, n_state, LANES)],
        out_specs=[blk(w, w), blk(w, 4 * n_state), blk(4 * n_state, w), blk(2, LANES)],
        out_shape=[
            jax.ShapeDtypeStruct((depth, n_groups, w, w), BF16),
            jax.ShapeDtypeStruct((depth, n_groups, w, 4 * n_state), BF16),
            jax.ShapeDtypeStruct((depth, n_groups, 4 * n_state, w), BF16),
            jax.ShapeDtypeStruct((depth, n_groups, 2, LANES), F32),
        ],
        compiler_params=_cparams("parallel", "parallel"),
        name="ssm_prep",
    )(arow, acol, bt, ct)


def _ssm_state_kernel(u_ref, win_ref, s_ref):
    s_ref[0] = _dot(u_ref[0], win_ref[0, 0])


def _ssm_out_kernel(u_ref, h_ref, m_ref, wout_ref, y_ref):
    y_ref[0] = _dot(u_ref[0], m_ref[0, 0]) + _dot(h_ref[0], wout_ref[0, 0])


def _ssm_scan_kernel(s_ref, dec_ref, h_ref):
    n_chunks = s_ref.shape[1]
    zeros = jnp.zeros(s_ref.shape[2:], F32)
    for d in range(2):
        dr = dec_ref[2 * d]
        di = dec_ref[2 * d + 1]

        def step(i, carry, d=d, dr=dr, di=di):
            hr, hi = carry
            k = i if d == 0 else n_chunks - 1 - i
            h_ref[2 * d, k] = hr
            h_ref[2 * d + 1, k] = hi
            return (dr * hr - di * hi + s_ref[2 * d, k], dr * hi + di * hr + s_ref[2 * d + 1, k])

        lax.fori_loop(0, n_chunks, step, (zeros, zeros))


def _ssm_branch(u, layer, m_all, win_all, wout_all, dec_all, batch, seq_len, chunk):
    n, d_u = u.shape
    n_groups = d_u // GROUP_CH
    w = chunk * GROUP_CH
    n_chunks = seq_len // chunk
    rows = batch * n_chunks
    st = 4 * N_STATE

    ug = u.reshape(rows, chunk, n_groups, GROUP_CH).transpose(2, 0, 1, 3).reshape(n_groups, rows, w)
    s = pl.pallas_call(
        _ssm_state_kernel,
        grid=(n_groups,),
        in_specs=[pl.BlockSpec((1, rows, w), lambda g: (g, 0, 0)),
                  pl.BlockSpec((1, 1, w, st), lambda g: (layer, g, 0, 0))],
        out_specs=pl.BlockSpec((1, rows, st), lambda g: (g, 0, 0)),
        out_shape=jax.ShapeDtypeStruct((n_groups, rows, st), F32),
        compiler_params=_cparams("parallel"),
        name="ssm_chunk_state",
    )(ug, win_all)

    gs = n_groups * N_STATE
    s4 = s.reshape(n_groups, batch, n_chunks, 4, N_STATE).transpose(3, 2, 1, 0, 4).reshape(4, n_chunks, batch, gs)
    dec4 = dec_all[layer].reshape(n_groups, 4, N_STATE).transpose(1, 0, 2).reshape(4, 1, gs)
    sl = min(SCAN_LANES, gs)
    h4 = pl.pallas_call(
        _ssm_scan_kernel,
        grid=(gs // sl,),
        in_specs=[pl.BlockSpec((4, n_chunks, batch, sl), lambda j: (0, 0, 0, j)),
                  pl.BlockSpec((4, 1, sl), lambda j: (0, 0, j))],
        out_specs=pl.BlockSpec((4, n_chunks, batch, sl), lambda j: (0, 0, 0, j)),
        out_shape=jax.ShapeDtypeStruct((4, n_chunks, batch, gs), F32),
        compiler_params=_cparams("parallel"),
        name="ssm_chunk_scan",
    )(s4, dec4)
    hg = (h4.reshape(4, n_chunks, batch, n_groups, N_STATE).transpose(3, 2, 1, 0, 4)
          .reshape(n_groups, rows, st).astype(BF16))

    y = pl.pallas_call(
        _ssm_out_kernel,
        grid=(n_groups,),
        in_specs=[pl.BlockSpec((1, rows, w), lambda g: (g, 0, 0)),
                  pl.BlockSpec((1, rows, st), lambda g: (g, 0, 0)),
                  pl.BlockSpec((1, 1, w, w), lambda g: (layer, g, 0, 0)),
                  pl.BlockSpec((1, 1, st, w), lambda g: (layer, g, 0, 0))],
        out_specs=pl.BlockSpec((1, rows, w), lambda g: (g, 0, 0)),
        out_shape=jax.ShapeDtypeStruct((n_groups, rows, w), F32),
        compiler_params=_cparams("parallel"),
        name="ssm_chunk_output",
    )(ug, hg, m_all, wout_all)
    return y.reshape(n_groups, rows, chunk, GROUP_CH).transpose(1, 2, 0, 3).reshape(n, d_u)


def _merge_kernel(x_ref, mod_ref, o_ref, y_ref, u_ref, ga_ref, gs_ref, d_ref,
                  wglu_ref, bglu_ref, wattn_ref, wo_ref, out_ref):
    d = x_ref.shape[1]
    y = y_ref[...] + d_ref[...] * u_ref[...].astype(F32)
    z = jax.nn.gelu(y, approximate=True).astype(BF16)
    glu = _dot(z, wglu_ref[...]) + bglu_ref[...]
    y_ssm = glu[:, :d] * jax.nn.sigmoid(glu[:, d:])
    y_attn = _dot(o_ref[...], wattn_ref[...])
    merged = (ga_ref[...].astype(F32) * y_attn + gs_ref[...].astype(F32) * y_ssm).astype(BF16)
    out_ref[...] = x_ref[...] + mod_ref[0][2:3] * _dot(merged, wo_ref[...])


def _merge(x, mod, o, y, u, ga, gs, ssm_d, w_glu, b_glu, w_attn, w_o, seq_len):
    n, d = x.shape
    tm = min(TOKEN_TILE, seq_len)
    tps = seq_len // tm
    tok = lambda i: (i, 0)
    full = lambda a: pl.BlockSpec(a.shape, lambda i: (0, 0))
    return pl.pallas_call(
        _merge_kernel,
        grid=(n // tm,),
        in_specs=[
            pl.BlockSpec((tm, d), tok),
            pl.BlockSpec((1, 6, d), lambda i: (i // tps, 0, 0)),
            pl.BlockSpec((tm, o.shape[1]), tok),
            pl.BlockSpec((tm, y.shape[1]), tok),
            pl.BlockSpec((tm, u.shape[1]), tok),
            pl.BlockSpec((tm, d), tok),
            pl.BlockSpec((tm, d), tok),
            full(ssm_d), full(w_glu), full(b_glu), full(w_attn), full(w_o),
        ],
        out_specs=pl.BlockSpec((tm, d), tok),
        out_shape=jax.ShapeDtypeStruct((n, d), F32),
        input_output_aliases={0: 0},
        compiler_params=_cparams("parallel"),
        name="merge_out_projection",
    )(x, mod, o, y, u, ga, gs, ssm_d, w_glu, b_glu, w_attn, w_o)


def _ffn_kernel(x_ref, mod_ref, g_ref, wg_ref, wu_ref, wd_ref, fg_ref, out_ref, h_ref, acc_ref,
                *, final_norm):
    j = pl.program_id(1)

    @pl.when(j == 0)
    def _():
        mod = mod_ref[0]
        h_ref[...] = (_rms(x_ref[...]) * g_ref[...] * (1.0 + mod[4:5]) + mod[3:4]).astype(BF16)
        acc_ref[...] = jnp.zeros(acc_ref.shape, F32)

    h = h_ref[...]
    gate = _dot(h, wg_ref[...])
    up = _dot(h, wu_ref[...])
    t = (gate * jax.nn.sigmoid(gate) * up).astype(BF16)
    acc_ref[...] += _dot(t, wd_ref[...])

    @pl.when(j == pl.num_programs(1) - 1)
    def _():
        xn = x_ref[...] + mod_ref[0][5:6] * acc_ref[...]
        if final_norm:
            xn = _rms(xn) * fg_ref[...]
        out_ref[...] = xn


def _ffn(x, mod, g, w_in, w_out, final_g, seq_len, final_norm):
    n, d = x.shape
    d_ff = w_out.shape[0]
    tm = min(TOKEN_TILE, seq_len)
    tps = seq_len // tm
    tf = FFN_TILE
    nf = d_ff // tf
    return pl.pallas_call(
        functools.partial(_ffn_kernel, final_norm=final_norm),
        grid=(n // tm, nf),
        in_specs=[
            pl.BlockSpec((tm, d), lambda i, j: (i, 0)),
            pl.BlockSpec((1, 6, d), lambda i, j: (i // tps, 0, 0)),
            pl.BlockSpec((1, d), lambda i, j: (0, 0)),
            pl.BlockSpec((d, tf), lambda i, j: (0, j)),
            pl.BlockSpec((d, tf), lambda i, j: (0, nf + j)),
            pl.BlockSpec((tf, d), lambda i, j: (j, 0)),
            pl.BlockSpec((1, d), lambda i, j: (0, 0)),
        ],
        out_specs=pl.BlockSpec((tm, d), lambda i, j: (i, 0)),
        out_shape=jax.ShapeDtypeStruct((n, d), F32),
        scratch_shapes=[pltpu.VMEM((tm, d), BF16), pltpu.VMEM((tm, d), F32)],
        input_output_aliases={0: 0},
        compiler_params=_cparams("parallel", "arbitrary"),
        name="ffn",
    )(x, mod, g, w_in, w_in, w_out, final_g)


def _rope_tables(seq_len):
    inv = 1.0 / (ROPE_THETA ** (jnp.arange(0, HEAD_DIM, 2, dtype=F32) / HEAD_DIM))
    ang = jnp.arange(seq_len, dtype=F32)[:, None] * inv[None, :]
    cos = jnp.cos(ang)
    sin = jnp.sin(ang)
    reps = LANES // HEAD_DIM
    cos_t = jnp.tile(jnp.concatenate([cos, cos], axis=-1), (1, reps))
    sin_t = jnp.tile(jnp.concatenate([-sin, sin], axis=-1), (1, reps))
    return cos_t, sin_t


def _lambda_init(layer):
    return 0.8 - 0.6 * math.exp(-0.3 * layer)


def _trunk(x, mod_all, p, ssm_ops):
    batch, seq_len, d = x.shape
    depth = p["w_in"].shape[0]
    d_qk = N_HEADS * 2 * HEAD_DIM
    d_v = N_HEADS * V_DIM
    d_u = p["ssm_d"].shape[1]
    cos, sin = _rope_tables(seq_len)
    x = x.reshape(batch * seq_len, d)
    row = lambda a: a.reshape(1, -1)
    for i in range(depth):
        mod = mod_all[i].reshape(batch, 6, d)
        q, k, v, u, ga, gs = _in_projection(x, mod, row(p["norm1_g"][i]), cos, sin, p["w_in"][i],
                                            seq_len, d_qk, d_v, d_u)
        lam_params = jnp.stack([p["lam_q1"][i], p["lam_k1"][i], p["lam_q2"][i], p["lam_k2"][i]])
        shp = (batch, seq_len, -1)
        o = _diff_attention(q.reshape(shp), k.reshape(shp), v.reshape(shp), lam_params,
                            row(p["subln_g"][i]), _lambda_init(i))
        y = _ssm_branch(u, i, *ssm_ops, batch, seq_len, SSM_CHUNK)
        x = _merge(x, mod, o.reshape(batch * seq_len, d_v), y, u, ga, gs, row(p["ssm_d"][i]),
                   p["w_glu"][i], row(p["b_glu"][i]), p["w_attn_br"][i], p["w_o"][i], seq_len)
        x = _ffn(x, mod, row(p["norm2_g"][i]), p["w_ffn_in"][i], p["w_ffn_out"][i],
                 row(p["final_g"]), seq_len, final_norm=(i == depth - 1))
    return x.reshape(batch, seq_len, d)


def kernel(x_prompt, x_sample, c_prompt, c_sample, w_mod, b_mod, norm1_g, w_in, lam_q1, lam_k1, lam_q2, lam_k2, subln_g, w_attn_br, ssm_a_re, ssm_a_im, ssm_log_dt, ssm_b_re, ssm_b_im, ssm_c_re, ssm_c_im, ssm_d, w_glu, b_glu, w_o, norm2_g, w_ffn_in, w_ffn_out, final_g):
    bp, bs = c_prompt.shape[0], c_sample.shape[0]
    pad = -(bp + bs) % 8
    c_all = jnp.concatenate([c_prompt, c_sample, jnp.zeros((pad, c_prompt.shape[1]), F32)], axis=0)
    mod_all = _modulation(c_all, w_mod, b_mod)
    ssm_ops = _ssm_prep(ssm_a_re, ssm_a_im, ssm_log_dt, ssm_b_re, ssm_b_im, ssm_c_re, ssm_c_im, SSM_CHUNK)
    p = dict(
        norm1_g=norm1_g, w_in=w_in.astype(BF16), lam_q1=lam_q1, lam_k1=lam_k1, lam_q2=lam_q2, lam_k2=lam_k2,
        subln_g=subln_g, w_attn_br=w_attn_br.astype(BF16), ssm_d=ssm_d, w_glu=w_glu.astype(BF16),
        b_glu=b_glu, w_o=w_o.astype(BF16), norm2_g=norm2_g, w_ffn_in=w_ffn_in.astype(BF16),
        w_ffn_out=w_ffn_out.astype(BF16), final_g=final_g)
    y_prompt = _trunk(x_prompt, mod_all[:, :bp], p, ssm_ops)
    y_sample = _trunk(x_sample, mod_all[:, bp:bp + bs], p, ssm_ops)
    return (y_prompt, y_sample)
```

```python
import functools
import math

import jax
import jax.numpy as jnp
from jax import lax
from jax.experimental import pallas as pl
from jax.experimental.pallas import tpu as pltpu

F32 = jnp.float32
BF16 = jnp.bfloat16

N_HEADS = 4
HEAD_DIM = 64
V_DIM = 2 * HEAD_DIM
GROUP_CH = 16
N_STATE = 64
ROPE_THETA = 10000.0
EPS = 1e-6
Q_SCALE = HEAD_DIM ** -0.5 * math.log2(math.e)
V_PAD = 16

VMEM_LIMIT_BYTES = 56 * 1024 * 1024
LANES = 128

TOKEN_TILE = 512
ATTN_TQ = 256
ATTN_TK = 256
ATTN_UNROLL = 10
SSM_CHUNK = 64
FFN_TILE = 1408
SCAN_LANES = 512


def _cparams(*sem):
    return pltpu.CompilerParams(dimension_semantics=sem, vmem_limit_bytes=VMEM_LIMIT_BYTES)


def _dot(a, b):
    return jnp.dot(a, b, preferred_element_type=F32)


def _split3(x):
    hi = x.astype(BF16)
    r1 = x - hi.astype(F32)
    mid = r1.astype(BF16)
    lo = (r1 - mid.astype(F32)).astype(BF16)
    return hi, mid, lo


def _dot_f32(a, b):
    a_hi = a.astype(BF16)
    a_lo = (a - a_hi.astype(F32)).astype(BF16)
    b_hi = b.astype(BF16)
    b_lo = (b - b_hi.astype(F32)).astype(BF16)
    return _dot(a_hi, b_hi) + (_dot(a_hi, b_lo) + _dot(a_lo, b_hi))


def _select_cols(x, sel):
    hi, mid, lo = _split3(x)
    return _dot(hi, sel) + (_dot(mid, sel) + _dot(lo, sel))


def _select_rows(sel, x):
    hi, mid, lo = _split3(x)
    return _dot(sel, hi) + (_dot(sel, mid) + _dot(sel, lo))


def _onehot(cond):
    return jnp.where(cond, 1.0, 0.0).astype(BF16)


def _rms(x):
    return x * lax.rsqrt(jnp.mean(x * x, axis=-1, keepdims=True) + EPS)


def _mod_kernel(c_ref, w_ref, b_ref, o_ref):
    c = c_ref[...]
    s = c * jax.nn.sigmoid(c)
    o_ref[0] = _dot_f32(s, w_ref[0]) + b_ref[0]


def _modulation(c_all, w_mod, b_mod):
    depth, d, n6 = w_mod.shape
    bp = c_all.shape[0]
    tn = 1536
    return pl.pallas_call(
        _mod_kernel,
        grid=(depth, n6 // tn),
        in_specs=[
            pl.BlockSpec((bp, d), lambda l, j: (0, 0)),
            pl.BlockSpec((1, d, tn), lambda l, j: (l, 0, j)),
            pl.BlockSpec((1, 1, tn), lambda l, j: (l, 0, j)),
        ],
        out_specs=pl.BlockSpec((1, bp, tn), lambda l, j: (l, 0, j)),
        out_shape=jax.ShapeDtypeStruct((depth, bp, n6), F32),
        compiler_params=_cparams("parallel", "parallel"),
        name="modulation",
    )(c_all, w_mod, b_mod.reshape(depth, 1, n6))


def _inproj_kernel(x_ref, mod_ref, g_ref, cos_ref, sin_ref, w_ref,
                   qt_ref, k_ref, vt_ref, u_ref, ga_ref, gs_ref):
    d = x_ref.shape[1]
    n_heads = k_ref.shape[1]
    d_qk = n_heads * V_DIM
    mod = mod_ref[0]
    h = (_rms(x_ref[...]) * g_ref[...] * (1.0 + mod[1:2]) + mod[0:1]).astype(BF16)

    qk = _dot(h, w_ref[:, 0:2 * d_qk])
    reps = 2 * d_qk // LANES
    cos = jnp.tile(cos_ref[...], (1, reps))
    sin = jnp.tile(sin_ref[...], (1, reps))
    lane = lax.broadcasted_iota(jnp.int32, qk.shape, 1)
    first_half = jnp.bitwise_and(lane, HEAD_DIM // 2) == 0
    half = HEAD_DIM // 2
    rot = jnp.where(first_half,
                    pltpu.roll(qk, 2 * d_qk - half, 1),
                    pltpu.roll(qk, half, 1))
    qk = qk * cos + rot * sin
    for hd in range(n_heads):
        qh = qk[:, hd * V_DIM:(hd + 1) * V_DIM] * Q_SCALE
        qt_ref[0, hd] = qh.T.astype(BF16)
        k_ref[0, hd] = qk[:, d_qk + hd * V_DIM:d_qk + (hd + 1) * V_DIM].astype(BF16)

    off = 2 * d_qk
    d_v = n_heads * V_DIM
    d_u = u_ref.shape[1]
    vu = _dot(h, w_ref[:, off:off + d_v + d_u])
    tk = vt_ref.shape[4]
    tail_row = lax.broadcasted_iota(jnp.int32, (V_PAD, tk), 0)
    ones_row = jnp.where(tail_row == 0, 1.0, 0.0).astype(BF16)
    for hd in range(n_heads):
        for c in range(vt_ref.shape[2]):
            vt_ref[0, hd, c, :V_DIM, :] = vu[c * tk:(c + 1) * tk, hd * V_DIM:(hd + 1) * V_DIM].T.astype(BF16)
            vt_ref[0, hd, c, V_DIM:, :] = ones_row
    u_ref[...] = vu[:, d_v:].astype(BF16)
    off += d_v + d_u
    ga_ref[...] = jax.nn.sigmoid(_dot(h, w_ref[:, off:off + d])).astype(BF16)
    gs_ref[...] = jax.nn.sigmoid(_dot(h, w_ref[:, off + d:off + 2 * d])).astype(BF16)


def _in_projection(x, mod, g, cos, sin, w, batch, seq_len, d_u):
    n, d = x.shape
    tm = min(TOKEN_TILE, seq_len)
    tps = seq_len // tm
    tk = min(ATTN_TK, tm)
    tok = lambda i: (i, 0)
    bf = lambda *shape: jax.ShapeDtypeStruct(shape, BF16)
    return pl.pallas_call(
        _inproj_kernel,
        grid=(n // tm,),
        in_specs=[
            pl.BlockSpec((tm, d), tok),
            pl.BlockSpec((1, 6, d), lambda i: (i // tps, 0, 0)),
            pl.BlockSpec((1, d), lambda i: (0, 0)),
            pl.BlockSpec((tm, LANES), lambda i: (i % tps, 0)),
            pl.BlockSpec((tm, LANES), lambda i: (i % tps, 0)),
            pl.BlockSpec(w.shape, lambda i: (0, 0)),
        ],
        out_specs=[
            pl.BlockSpec((1, N_HEADS, V_DIM, tm), lambda i: (i // tps, 0, 0, i % tps)),
            pl.BlockSpec((1, N_HEADS, tm, V_DIM), lambda i: (i // tps, 0, i % tps, 0)),
            pl.BlockSpec((1, N_HEADS, tm // tk, V_DIM + V_PAD, tk), lambda i: (i // tps, 0, i % tps, 0, 0)),
            pl.BlockSpec((tm, d_u), tok),
            pl.BlockSpec((tm, d), tok),
            pl.BlockSpec((tm, d), tok),
        ],
        out_shape=[
            bf(batch, N_HEADS, V_DIM, seq_len),
            bf(batch, N_HEADS, seq_len, V_DIM),
            bf(batch, N_HEADS, seq_len // tk, V_DIM + V_PAD, tk),
            bf(n, d_u), bf(n, d), bf(n, d),
        ],
        compiler_params=_cparams("parallel"),
        name="in_projection",
    )(x, mod, g, cos, sin, w)


def _attn_kernel(qt_ref, k_ref, vt_ref, lam_ref, g_ref, o_ref, s0_ref, s1_ref, p0_ref, p1_ref, acc_ref,
                 *, lam_init):
    n_chunks, _, tk = vt_ref.shape[2:]
    tq = qt_ref.shape[3]
    s_refs = (s0_ref, s1_ref)
    p_refs = (p0_ref, p1_ref)
    qt = qt_ref[0, 0].astype(F32)
    row = lax.broadcasted_iota(jnp.int32, qt.shape, 0)
    qtb = jnp.concatenate([jnp.where(row < HEAD_DIM, qt, 0.0), jnp.where(row >= HEAD_DIM, qt, 0.0)],
                          axis=1).astype(BF16)

    def scores(c):
        kc = k_ref[0, 0, pl.ds(pl.multiple_of(c * tk, tk), tk), :]
        return _dot(kc, qtb)

    def softmax(slot, m_old):
        s = s_refs[slot][...]
        m_new = jnp.maximum(m_old, jnp.max(s, axis=0, keepdims=True))
        p_refs[slot][...] = jnp.exp2(s - m_new).astype(BF16)
        return m_new, jnp.exp2(m_old - m_new)

    def attend(c, slot, alpha):
        acc_ref[...] = alpha * acc_ref[...] + _dot(vt_ref[0, 0, c], p_refs[slot][...])

    def step(c, slot, carry):
        m, alpha = carry
        s_refs[1 - slot][...] = scores(c + 1)
        m, alpha_new = softmax(slot, m)
        attend(c - 1, 1 - slot, alpha)
        return m, alpha_new

    acc_ref[...] = jnp.zeros(acc_ref.shape, F32)
    s_refs[0][...] = scores(0)
    s_refs[1][...] = scores(1)
    carry = softmax(0, jnp.full((1, 2 * tq), -jnp.inf, F32))

    n_steady = n_chunks - 2
    unroll = min(ATTN_UNROLL, n_steady)
    n_loops = n_steady // unroll if unroll else 0

    def body(j, carry):
        c = unroll * j + 1
        for i in range(unroll):
            carry = step(c + i, (1 + i) % 2, carry)
        return carry

    if n_loops:
        carry = lax.fori_loop(0, n_loops, body, carry)
    for c in range(n_loops * unroll + 1, n_chunks - 1):
        carry = step(c, c % 2, carry)
    m, alpha = carry
    last = (n_chunks - 1) % 2
    _, alpha_last = softmax(last, m)
    attend(n_chunks - 2, 1 - last, alpha)
    attend(n_chunks - 1, last, alpha_last)

    lp = lam_ref[...]
    lam = (jnp.exp(jnp.sum(lp[0:1] * lp[1:2], axis=-1, keepdims=True))
           - jnp.exp(jnp.sum(lp[2:3] * lp[3:4], axis=-1, keepdims=True)) + lam_init)
    acc = acc_ref[...]
    acc = acc[:V_DIM] / acc[V_DIM:V_DIM + 1]
    o = (acc[:, :tq] - lam * acc[:, tq:]).T
    o = _rms(o) * g_ref[...] * (1.0 - lam_init)
    o_ref[0] = o.astype(o_ref.dtype)


def _diff_attention(qt, k, vt, lam_params, subln_g, lam_init):
    b, n_heads, l, _ = k.shape
    tq = min(ATTN_TQ, l)
    n_chunks, v_rows, tk = vt.shape[2:]
    assert n_chunks % 2 == 0 and ATTN_UNROLL % 2 == 0, "double-buffer slots need an even chunk count"
    return pl.pallas_call(
        functools.partial(_attn_kernel, lam_init=lam_init),
        grid=(b, n_heads, l // tq),
        in_specs=[
            pl.BlockSpec((1, 1, V_DIM, tq), lambda bi, h, qi: (bi, h, 0, qi)),
            pl.BlockSpec((1, 1, l, V_DIM), lambda bi, h, qi: (bi, h, 0, 0)),
            pl.BlockSpec((1, 1) + vt.shape[2:], lambda bi, h, qi: (bi, h, 0, 0, 0)),
            pl.BlockSpec(lam_params.shape, lambda bi, h, qi: (0, 0)),
            pl.BlockSpec((1, V_DIM), lambda bi, h, qi: (0, 0)),
        ],
        out_specs=pl.BlockSpec((1, tq, V_DIM), lambda bi, h, qi: (bi, qi, h)),
        out_shape=jax.ShapeDtypeStruct((b, l, n_heads * V_DIM), BF16),
        scratch_shapes=[pltpu.VMEM((tk, 2 * tq), F32)] * 2 + [pltpu.VMEM((tk, 2 * tq), BF16)] * 2
                       + [pltpu.VMEM((v_rows, 2 * tq), F32)],
        compiler_params=_cparams("parallel", "parallel", "parallel"),
        name="diff_attention",
    )(qt, k, vt, lam_params, subln_g)


def _ssm_prep_kernel(arow_ref, acol_ref, bt_ref, ct_ref, m_ref, win_ref, wout_ref, dec_ref, *, chunk):
    t_len = chunk
    w = t_len * GROUP_CH
    w2 = 2 * w
    arow = arow_ref[0, 0]
    acol = acol_ref[0, 0]
    lo = lax.broadcasted_iota(jnp.int32, (1, LANES), 1) < N_STATE

    s_of_row = lax.shift_right_logical(lax.broadcasted_iota(jnp.int32, (w, t_len), 0), 4)
    j_of_col = lax.broadcasted_iota(jnp.int32, (w, t_len), 1)
    rep_rows = (_onehot(j_of_col == (t_len - 1 - s_of_row)), _onehot(j_of_col == s_of_row))
    n_rows = lax.broadcasted_iota(jnp.int32, (t_len, 1), 0).astype(F32)

    def lane_maps(width):
        lane = lax.broadcasted_iota(jnp.int32, (LANES, width), 1)
        return (lax.shift_right_logical(lane, 4), jnp.bitwise_and(lane, GROUP_CH - 1),
                lax.broadcasted_iota(jnp.int32, (LANES, width), 0))

    lag_idx, ch_idx, jrow = lane_maps(w2)
    lag_idx_w, _, jrow_w = lane_maps(w)
    tile_ch = _onehot(jrow == ch_idx)
    n_lanes = jnp.minimum(lax.broadcasted_iota(jnp.int32, (1, LANES), 1), t_len).astype(F32)

    strip = jnp.zeros((GROUP_CH, w2), F32)
    for d in range(2):
        dt = jnp.exp(arow[4 + d:5 + d])
        ar = arow[d:d + 1]
        ai = arow[2 + d:3 + d]
        zr = dt * ar
        zi = dt * ai
        mag = jnp.exp(zr)
        nr = mag * jnp.cos(zi) - 1.0
        ni = mag * jnp.sin(zi)
        den = ar * ar + ai * ai
        fr = (nr * ar + ni * ai) / den
        fi = (ni * ar - nr * ai) / den
        br = bt_ref[0, 0, d]
        bi = bt_ref[0, 0, 2 + d]
        bbr = fr * br - fi * bi
        bbi = fr * bi + fi * br

        pm = jnp.exp(n_rows * zr)
        pc = pm * jnp.cos(n_rows * zi)
        ps = pm * jnp.sin(n_rows * zi)
        e = _select_rows(rep_rows[d], jnp.where(lo, pc, ps))
        e_sw = _select_rows(rep_rows[d], jnp.where(lo, -ps, pc))
        win = e * jnp.tile(bbr, (t_len, 1)) + e_sw * jnp.tile(bbi, (t_len, 1))
        win_ref[0, 0, :, d * LANES:(d + 1) * LANES] = win.astype(win_ref.dtype)

        dm = jnp.exp(t_len * zr)
        dec_ref[0, 0, d:d + 1, :] = jnp.where(lo, dm * jnp.cos(t_len * zi), dm * jnp.sin(t_len * zi))

        dtc = jnp.exp(acol[:, 4 + d:5 + d])
        zrc = dtc * acol[:, d:d + 1]
        zic = dtc * acol[:, 2 + d:3 + d]
        ptm = jnp.exp(zrc * n_lanes)
        pt_re = ptm * jnp.cos(zic * n_lanes)
        pt_im = ptm * jnp.sin(zic * n_lanes)
        c_re = _select_cols(ct_ref[0, 0, d], tile_ch)
        c_im = _select_cols(ct_ref[0, 0, 2 + d], tile_ch)

        def table(rep):
            width = rep.shape[1]
            p_re = _select_cols(pt_re, rep)
            p_im = _select_cols(pt_im, rep)
            cr = c_re[:, :width]
            ci = c_im[:, :width]
            return p_re * cr - p_im * ci, p_re * ci + p_im * cr

        if d == 0:
            power = lag_idx - (t_len - 1)
        else:
            power = (t_len - 1) - lag_idx
        power = jnp.where(power >= 0, power, -1)
        f_re, f_im = table(_onehot(jrow == power))
        bpk = jnp.where(lo, bbr, -bbi)
        strip = strip + _dot_f32(bpk, jnp.concatenate([f_re, f_im], axis=0))

        if d == 0:
            o_re, o_im = f_re[:, w:], f_im[:, w:]
        else:
            o_re, o_im = table(_onehot(jrow_w == (t_len - lag_idx_w)))
        wout_ref[0, 0, (2 * d) * N_STATE:(2 * d + 1) * N_STATE, :] = o_re.astype(wout_ref.dtype)
        wout_ref[0, 0, (2 * d + 1) * N_STATE:(2 * d + 2) * N_STATE, :] = (-o_im).astype(wout_ref.dtype)

    per_tile = LANES // GROUP_CH
    for r in range(per_tile):
        rolled = strip if r == 0 else pltpu.roll(strip, w2 - r * GROUP_CH, 1)
        for s in range(t_len):
            if (t_len - 1 - s) % per_tile == r:
                q = (t_len - 1 - s) // per_tile
                m_ref[0, 0, s * GROUP_CH:(s + 1) * GROUP_CH, :] = (
                    rolled[:, q * LANES:q * LANES + w].astype(m_ref.dtype))


def _ssm_prep(a_re, a_im, log_dt, b_re, b_im, c_re, c_im, chunk):
    depth, _, n_groups, n_state = a_re.shape
    w = chunk * GROUP_CH
    gd = lambda x: jnp.swapaxes(x, 1, 2)
    ldt = jnp.broadcast_to(gd(log_dt)[..., None], (depth, n_groups, 2, n_state))
    rows = jnp.concatenate([gd(a_re), gd(a_im), ldt, jnp.zeros_like(ldt)], axis=2)
    arow = jnp.concatenate([rows, rows], axis=-1)
    acol = jnp.swapaxes(rows, 2, 3)
    bt = jnp.swapaxes(jnp.concatenate([gd(b_re), gd(b_im)], axis=2), 3, 4)
    bt = jnp.concatenate([bt, bt], axis=-1)
    ct = jnp.swapaxes(jnp.concatenate([gd(c_re), gd(c_im)], axis=2), 3, 4)
    ct = jnp.pad(ct, ((0, 0),) * 4 + ((0, LANES - GROUP_CH),))
    blk = lambda *tail: pl.BlockSpec((1, 1) + tail, lambda l, g: (l, g) + (0,) * len(tail))
    return pl.pallas_call(
        functools.partial(_ssm_prep_kernel, chunk=chunk),
        grid=(depth, n_groups),
        in_specs=[blk(8, LANES), blk(n_state, 8), blk(4, GROUP_CH, LANES), blk(4, n_state, LANES)],
        out_specs=[blk(w, w), blk(w, 4 * n_state), blk(4 * n_state, w), blk(2, LANES)],
        out_shape=[
            jax.ShapeDtypeStruct((depth, n_groups, w, w), BF16),
            jax.ShapeDtypeStruct((depth, n_groups, w, 4 * n_state), BF16),
            jax.ShapeDtypeStruct((depth, n_groups, 4 * n_state, w), BF16),
            jax.ShapeDtypeStruct((depth, n_groups, 2, LANES), F32),
        ],
        compiler_params=_cparams("parallel", "parallel"),
        name="ssm_prep",
    )(arow, acol, bt, ct)


def _ssm_state_kernel(u_ref, win_ref, s_ref):
    s_ref[0] = _dot(u_ref[0], win_ref[0, 0])


def _ssm_out_kernel(u_ref, h_ref, m_ref, wout_ref, y_ref):
    y_ref[0] = _dot(u_ref[0], m_ref[0, 0]) + _dot(h_ref[0], wout_ref[0, 0])


def _ssm_scan_kernel(s_ref, dec_ref, h_ref):
    n_chunks = s_ref.shape[1]
    zeros = jnp.zeros(s_ref.shape[2:], F32)
    for d in range(2):
        dr = dec_ref[2 * d]
        di = dec_ref[2 * d + 1]

        def step(i, carry, d=d, dr=dr, di=di):
            hr, hi = carry
            k = i if d == 0 else n_chunks - 1 - i
            h_ref[2 * d, k] = hr
            h_ref[2 * d + 1, k] = hi
            return (dr * hr - di * hi + s_ref[2 * d, k], dr * hi + di * hr + s_ref[2 * d + 1, k])

        lax.fori_loop(0, n_chunks, step, (zeros, zeros))


def _ssm_branch(u, layer, m_all, win_all, wout_all, dec_all, batch, seq_len, chunk):
    n, d_u = u.shape
    n_groups = d_u // GROUP_CH
    w = chunk * GROUP_CH
    n_chunks = seq_len // chunk
    rows = batch * n_chunks
    st = 4 * N_STATE

    ug = u.reshape(rows, chunk, n_groups, GROUP_CH).transpose(2, 0, 1, 3).reshape(n_groups, rows, w)
    s = pl.pallas_call(
        _ssm_state_kernel,
        grid=(n_groups,),
        in_specs=[pl.BlockSpec((1, rows, w), lambda g: (g, 0, 0)),
                  pl.BlockSpec((1, 1, w, st), lambda g: (layer, g, 0, 0))],
        out_specs=pl.BlockSpec((1, rows, st), lambda g: (g, 0, 0)),
        out_shape=jax.ShapeDtypeStruct((n_groups, rows, st), F32),
        compiler_params=_cparams("parallel"),
        name="ssm_chunk_state",
    )(ug, win_all)

    gs = n_groups * N_STATE
    s4 = s.reshape(n_groups, batch, n_chunks, 4, N_STATE).transpose(3, 2, 1, 0, 4).reshape(4, n_chunks, batch, gs)
    dec4 = dec_all[layer].reshape(n_groups, 4, N_STATE).transpose(1, 0, 2).reshape(4, 1, gs)
    sl = min(SCAN_LANES, gs)
    h4 = pl.pallas_call(
        _ssm_scan_kernel,
        grid=(gs // sl,),
        in_specs=[pl.BlockSpec((4, n_chunks, batch, sl), lambda j: (0, 0, 0, j)),
                  pl.BlockSpec((4, 1, sl), lambda j: (0, 0, j))],
        out_specs=pl.BlockSpec((4, n_chunks, batch, sl), lambda j: (0, 0, 0, j)),
        out_shape=jax.ShapeDtypeStruct((4, n_chunks, batch, gs), F32),
        compiler_params=_cparams("parallel"),
        name="ssm_chunk_scan",
    )(s4, dec4)
    hg = (h4.reshape(4, n_chunks, batch, n_groups, N_STATE).transpose(3, 2, 1, 0, 4)
          .reshape(n_groups, rows, st).astype(BF16))

    y = pl.pallas_call(
        _ssm_out_kernel,
        grid=(n_groups,),
        in_specs=[pl.BlockSpec((1, rows, w), lambda g: (g, 0, 0)),
                  pl.BlockSpec((1, rows, st), lambda g: (g, 0, 0)),
                  pl.BlockSpec((1, 1, w, w), lambda g: (layer, g, 0, 0)),
                  pl.BlockSpec((1, 1, st, w), lambda g: (layer, g, 0, 0))],
        out_specs=pl.BlockSpec((1, rows, w), lambda g: (g, 0, 0)),
        out_shape=jax.ShapeDtypeStruct((n_groups, rows, w), F32),
        compiler_params=_cparams("parallel"),
        name="ssm_chunk_output",
    )(ug, hg, m_all, wout_all)
    return y.reshape(n_groups, rows, chunk, GROUP_CH).transpose(1, 2, 0, 3).reshape(n, d_u)


def _merge_kernel(x_ref, mod_ref, o_ref, y_ref, u_ref, ga_ref, gs_ref, d_ref,
                  wglu_ref, bglu_ref, wattn_ref, wo_ref, out_ref):
    d = x_ref.shape[1]
    y = y_ref[...] + d_ref[...] * u_ref[...].astype(F32)
    z = jax.nn.gelu(y, approximate=True).astype(BF16)
    glu = _dot(z, wglu_ref[...]) + bglu_ref[...]
    y_ssm = glu[:, :d] * jax.nn.sigmoid(glu[:, d:])
    y_attn = _dot(o_ref[...], wattn_ref[...])
    merged = (ga_ref[...].astype(F32) * y_attn + gs_ref[...].astype(F32) * y_ssm).astype(BF16)
    out_ref[...] = x_ref[...] + mod_ref[0][2:3] * _dot(merged, wo_ref[...])


def _merge(x, mod, o, y, u, ga, gs, ssm_d, w_glu, b_glu, w_attn, w_o, seq_len):
    n, d = x.shape
    tm = min(TOKEN_TILE, seq_len)
    tps = seq_len // tm
    tok = lambda i: (i, 0)
    full = lambda a: pl.BlockSpec(a.shape, lambda i: (0, 0))
    return pl.pallas_call(
        _merge_kernel,
        grid=(n // tm,),
        in_specs=[
            pl.BlockSpec((tm, d), tok),
            pl.BlockSpec((1, 6, d), lambda i: (i // tps, 0, 0)),
            pl.BlockSpec((tm, o.shape[1]), tok),
            pl.BlockSpec((tm, y.shape[1]), tok),
            pl.BlockSpec((tm, u.shape[1]), tok),
            pl.BlockSpec((tm, d), tok),
            pl.BlockSpec((tm, d), tok),
            full(ssm_d), full(w_glu), full(b_glu), full(w_attn), full(w_o),
        ],
        out_specs=pl.BlockSpec((tm, d), tok),
        out_shape=jax.ShapeDtypeStruct((n, d), F32),
        input_output_aliases={0: 0},
        compiler_params=_cparams("parallel"),
        name="merge_out_projection",
    )(x, mod, o, y, u, ga, gs, ssm_d, w_glu, b_glu, w_attn, w_o)


def _ffn_kernel(x_ref, mod_ref, g_ref, wg_ref, wu_ref, wd_ref, fg_ref, out_ref, h_ref, acc_ref,
                *, final_norm):
    j = pl.program_id(1)

    @pl.when(j == 0)
    def _():
        mod = mod_ref[0]
        h_ref[...] = (_rms(x_ref[...]) * g_ref[...] * (1.0 + mod[4:5]) + mod[3:4]).astype(BF16)
        acc_ref[...] = jnp.zeros(acc_ref.shape, F32)

    h = h_ref[...]
    gate = _dot(h, wg_ref[...])
    up = _dot(h, wu_ref[...])
    t = (gate * jax.nn.sigmoid(gate) * up).astype(BF16)
    acc_ref[...] += _dot(t, wd_ref[...])

    @pl.when(j == pl.num_programs(1) - 1)
    def _():
        xn = x_ref[...] + mod_ref[0][5:6] * acc_ref[...]
        if final_norm:
            xn = _rms(xn) * fg_ref[...]
        out_ref[...] = xn


def _ffn(x, mod, g, w_in, w_out, final_g, seq_len, final_norm):
    n, d = x.shape
    d_ff = w_out.shape[0]
    tm = min(TOKEN_TILE, seq_len)
    tps = seq_len // tm
    tf = FFN_TILE
    nf = d_ff // tf
    return pl.pallas_call(
        functools.partial(_ffn_kernel, final_norm=final_norm),
        grid=(n // tm, nf),
        in_specs=[
            pl.BlockSpec((tm, d), lambda i, j: (i, 0)),
            pl.BlockSpec((1, 6, d), lambda i, j: (i // tps, 0, 0)),
            pl.BlockSpec((1, d), lambda i, j: (0, 0)),
            pl.BlockSpec((d, tf), lambda i, j: (0, j)),
            pl.BlockSpec((d, tf), lambda i, j: (0, nf + j)),
            pl.BlockSpec((tf, d), lambda i, j: (j, 0)),
            pl.BlockSpec((1, d), lambda i, j: (0, 0)),
        ],
        out_specs=pl.BlockSpec((tm, d), lambda i, j: (i, 0)),
        out_shape=jax.ShapeDtypeStruct((n, d), F32),
        scratch_shapes=[pltpu.VMEM((tm, d), BF16), pltpu.VMEM((tm, d), F32)],
        input_output_aliases={0: 0},
        compiler_params=_cparams("parallel", "arbitrary"),
        name="ffn",
    )(x, mod, g, w_in, w_in, w_out, final_g)


def _rope_tables(seq_len):
    inv = 1.0 / (ROPE_THETA ** (jnp.arange(0, HEAD_DIM, 2, dtype=F32) / HEAD_DIM))
    ang = jnp.arange(seq_len, dtype=F32)[:, None] * inv[None, :]
    cos = jnp.cos(ang)
    sin = jnp.sin(ang)
    reps = LANES // HEAD_DIM
    cos_t = jnp.tile(jnp.concatenate([cos, cos], axis=-1), (1, reps))
    sin_t = jnp.tile(jnp.concatenate([-sin, sin], axis=-1), (1, reps))
    return cos_t, sin_t


def _lambda_init(layer):
    return 0.8 - 0.6 * math.exp(-0.3 * layer)


def _trunk(x, mod_all, p, ssm_ops):
    batch, seq_len, d = x.shape
    depth = p["w_in"].shape[0]
    d_v = N_HEADS * V_DIM
    d_u = p["ssm_d"].shape[1]
    cos, sin = _rope_tables(seq_len)
    x = x.reshape(batch * seq_len, d)
    row = lambda a: a.reshape(1, -1)
    for i in range(depth):
        mod = mod_all[i].reshape(batch, 6, d)
        qt, k, vt, u, ga, gs = _in_projection(x, mod, row(p["norm1_g"][i]), cos, sin, p["w_in"][i],
                                              batch, seq_len, d_u)
        lam_params = jnp.stack([p["lam_q1"][i], p["lam_k1"][i], p["lam_q2"][i], p["lam_k2"][i]])
        o = _diff_attention(qt, k, vt, lam_params, row(p["subln_g"][i]), _lambda_init(i))
        y = _ssm_branch(u, i, *ssm_ops, batch, seq_len, SSM_CHUNK)
        x = _merge(x, mod, o.reshape(batch * seq_len, d_v), y, u, ga, gs, row(p["ssm_d"][i]),
                   p["w_glu"][i], row(p["b_glu"][i]), p["w_attn_br"][i], p["w_o"][i], seq_len)
        x = _ffn(x, mod, row(p["norm2_g"][i]), p["w_ffn_in"][i], p["w_ffn_out"][i],
                 row(p["final_g"]), seq_len, final_norm=(i == depth - 1))
    return x.reshape(batch, seq_len, d)


def kernel(x_prompt, x_sample, c_prompt, c_sample, w_mod, b_mod, norm1_g, w_in, lam_q1, lam_k1, lam_q2, lam_k2, subln_g, w_attn_br, ssm_a_re, ssm_a_im, ssm_log_dt, ssm_b_re, ssm_b_im, ssm_c_re, ssm_c_im, ssm_d, w_glu, b_glu, w_o, norm2_g, w_ffn_in, w_ffn_out, final_g):
    bp, bs = c_prompt.shape[0], c_sample.shape[0]
    pad = -(bp + bs) % 8
    c_all = jnp.concatenate([c_prompt, c_sample, jnp.zeros((pad, c_prompt.shape[1]), F32)], axis=0)
    mod_all = _modulation(c_all, w_mod, b_mod)
    ssm_ops = _ssm_prep(ssm_a_re, ssm_a_im, ssm_log_dt, ssm_b_re, ssm_b_im, ssm_c_re, ssm_c_im, SSM_CHUNK)
    p = dict(
        norm1_g=norm1_g, w_in=w_in.astype(BF16), lam_q1=lam_q1, lam_k1=lam_k1, lam_q2=lam_q2, lam_k2=lam_k2,
        subln_g=subln_g, w_attn_br=w_attn_br.astype(BF16), ssm_d=ssm_d, w_glu=w_glu.astype(BF16),
        b_glu=b_glu, w_o=w_o.astype(BF16), norm2_g=norm2_g, w_ffn_in=w_ffn_in.astype(BF16),
        w_ffn_out=w_ffn_out.astype(BF16), final_g=final_g)
    y_prompt = _trunk(x_prompt, mod_all[:, :bp], p, ssm_ops)
    y_sample = _trunk(x_sample, mod_all[:, bp:bp + bs], p, ssm_ops)
    return (y_prompt, y_sample)
```

```python
import functools
import math

import jax
import jax.numpy as jnp
from jax import lax
from jax.experimental import pallas as pl
from jax.experimental.pallas import tpu as pltpu

F32 = jnp.float32
BF16 = jnp.bfloat16

N_HEADS = 4
HEAD_DIM = 64
V_DIM = 2 * HEAD_DIM
GROUP_CH = 16
N_STATE = 64
ROPE_THETA = 10000.0
EPS = 1e-6
Q_SCALE = HEAD_DIM ** -0.5 * math.log2(math.e)
V_PAD = 16

VMEM_LIMIT_BYTES = 56 * 1024 * 1024
LANES = 128
SUBLANES = 8

TOKEN_TILE = 512
ATTN_TQ = 256
ATTN_TK = 256
ATTN_UNROLL = 10
SSM_CHUNK = 64
FFN_TILE = 1408
SCAN_LANES = 1024


def _cparams(*sem):
    return pltpu.CompilerParams(dimension_semantics=sem, vmem_limit_bytes=VMEM_LIMIT_BYTES)


def _dot(a, b):
    return jnp.dot(a, b, preferred_element_type=F32)


def _split3(x):
    hi = x.astype(BF16)
    r1 = x - hi.astype(F32)
    mid = r1.astype(BF16)
    lo = (r1 - mid.astype(F32)).astype(BF16)
    return hi, mid, lo


def _dot_f32(a, b):
    a_hi = a.astype(BF16)
    a_lo = (a - a_hi.astype(F32)).astype(BF16)
    b_hi = b.astype(BF16)
    b_lo = (b - b_hi.astype(F32)).astype(BF16)
    return _dot(a_hi, b_hi) + (_dot(a_hi, b_lo) + _dot(a_lo, b_hi))


def _select_cols(x, sel):
    hi, mid, lo = _split3(x)
    return _dot(hi, sel) + (_dot(mid, sel) + _dot(lo, sel))


def _select_rows(sel, x):
    hi, mid, lo = _split3(x)
    return _dot(sel, hi) + (_dot(sel, mid) + _dot(sel, lo))


def _onehot(cond):
    return jnp.where(cond, 1.0, 0.0).astype(BF16)


def _rms(x):
    return x * lax.rsqrt(jnp.mean(x * x, axis=-1, keepdims=True) + EPS)


def _mod_kernel(c_ref, w_ref, b_ref, o_ref):
    c = c_ref[...]
    s = c * jax.nn.sigmoid(c)
    o_ref[0] = _dot_f32(s, w_ref[0]) + b_ref[0]


def _modulation(c_all, w_mod, b_mod):
    depth, d, n6 = w_mod.shape
    bp = c_all.shape[0]
    tn = 1536
    return pl.pallas_call(
        _mod_kernel,
        grid=(depth, n6 // tn),
        in_specs=[
            pl.BlockSpec((bp, d), lambda l, j: (0, 0)),
            pl.BlockSpec((1, d, tn), lambda l, j: (l, 0, j)),
            pl.BlockSpec((1, 1, tn), lambda l, j: (l, 0, j)),
        ],
        out_specs=pl.BlockSpec((1, bp, tn), lambda l, j: (l, 0, j)),
        out_shape=jax.ShapeDtypeStruct((depth, bp, n6), F32),
        compiler_params=_cparams("parallel", "parallel"),
        name="modulation",
    )(c_all, w_mod, b_mod.reshape(depth, 1, n6))


def _inproj_kernel(x_ref, mod_ref, g_ref, cos_ref, sin_ref, w_ref,
                   qt_ref, k_ref, vt_ref, u_ref, ga_ref, gs_ref):
    d = x_ref.shape[1]
    n_heads = k_ref.shape[1]
    d_qk = n_heads * V_DIM
    mod = mod_ref[0]
    h = (_rms(x_ref[...]) * g_ref[...] * (1.0 + mod[1:2]) + mod[0:1]).astype(BF16)

    qk = _dot(h, w_ref[:, 0:2 * d_qk])
    reps = 2 * d_qk // LANES
    cos = jnp.tile(cos_ref[...], (1, reps))
    sin = jnp.tile(sin_ref[...], (1, reps))
    lane = lax.broadcasted_iota(jnp.int32, qk.shape, 1)
    first_half = jnp.bitwise_and(lane, HEAD_DIM // 2) == 0
    half = HEAD_DIM // 2
    rot = jnp.where(first_half,
                    pltpu.roll(qk, 2 * d_qk - half, 1),
                    pltpu.roll(qk, half, 1))
    qk = qk * cos + rot * sin
    for hd in range(n_heads):
        qh = qk[:, hd * V_DIM:(hd + 1) * V_DIM] * Q_SCALE
        qt_ref[0, hd] = qh.T.astype(BF16)
        k_ref[0, hd] = qk[:, d_qk + hd * V_DIM:d_qk + (hd + 1) * V_DIM].astype(BF16)

    off = 2 * d_qk
    d_v = n_heads * V_DIM
    d_u = u_ref.shape[1]
    vu = _dot(h, w_ref[:, off:off + d_v + d_u])
    tk = vt_ref.shape[4]
    tail_row = lax.broadcasted_iota(jnp.int32, (V_PAD, tk), 0)
    ones_row = jnp.where(tail_row == 0, 1.0, 0.0).astype(BF16)
    for hd in range(n_heads):
        for c in range(vt_ref.shape[2]):
            vt_ref[0, hd, c, :V_DIM, :] = vu[c * tk:(c + 1) * tk, hd * V_DIM:(hd + 1) * V_DIM].T.astype(BF16)
            vt_ref[0, hd, c, V_DIM:, :] = ones_row
    u_ref[...] = vu[:, d_v:].astype(BF16)
    off += d_v + d_u
    ga_ref[...] = jax.nn.sigmoid(_dot(h, w_ref[:, off:off + d])).astype(BF16)
    gs_ref[...] = jax.nn.sigmoid(_dot(h, w_ref[:, off + d:off + 2 * d])).astype(BF16)


def _in_projection(x, mod, g, cos, sin, w_all, layer, batch, seq_len, d_u):
    n, d = x.shape
    tm = min(TOKEN_TILE, seq_len)
    tps = seq_len // tm
    tk = min(ATTN_TK, tm)
    tok = lambda i: (i, 0)
    bf = lambda *shape: jax.ShapeDtypeStruct(shape, BF16)
    return pl.pallas_call(
        _inproj_kernel,
        grid=(n // tm,),
        in_specs=[
            pl.BlockSpec((tm, d), tok),
            pl.BlockSpec((1, 6, d), lambda i: (i // tps, 0, 0)),
            pl.BlockSpec((1, d), lambda i: (0, 0)),
            pl.BlockSpec((tm, LANES), lambda i: (i % tps, 0)),
            pl.BlockSpec((tm, LANES), lambda i: (i % tps, 0)),
            pl.BlockSpec((None,) + w_all.shape[1:], lambda i: (layer, 0, 0)),
        ],
        out_specs=[
            pl.BlockSpec((1, N_HEADS, V_DIM, tm), lambda i: (i // tps, 0, 0, i % tps)),
            pl.BlockSpec((1, N_HEADS, tm, V_DIM), lambda i: (i // tps, 0, i % tps, 0)),
            pl.BlockSpec((1, N_HEADS, tm // tk, V_DIM + V_PAD, tk), lambda i: (i // tps, 0, i % tps, 0, 0)),
            pl.BlockSpec((tm, d_u), tok),
            pl.BlockSpec((tm, d), tok),
            pl.BlockSpec((tm, d), tok),
        ],
        out_shape=[
            bf(batch, N_HEADS, V_DIM, seq_len),
            bf(batch, N_HEADS, seq_len, V_DIM),
            bf(batch, N_HEADS, seq_len // tk, V_DIM + V_PAD, tk),
            bf(n, d_u), bf(n, d), bf(n, d),
        ],
        compiler_params=_cparams("parallel"),
        name="in_projection",
    )(x, mod, g, cos, sin, w_all)


def _attn_kernel(qt_ref, k_ref, vt_ref, lam_ref, g_ref, o_ref, s0_ref, s1_ref, p0_ref, p1_ref, acc_ref,
                 *, lam_init):
    n_chunks, _, tk = vt_ref.shape[2:]
    tq = qt_ref.shape[3]
    s_refs = (s0_ref, s1_ref)
    p_refs = (p0_ref, p1_ref)
    qt = qt_ref[0, 0].astype(F32)
    row = lax.broadcasted_iota(jnp.int32, qt.shape, 0)
    qtb = jnp.concatenate([jnp.where(row < HEAD_DIM, qt, 0.0), jnp.where(row >= HEAD_DIM, qt, 0.0)],
                          axis=1).astype(BF16)

    def scores(c):
        kc = k_ref[0, 0, pl.ds(pl.multiple_of(c * tk, tk), tk), :]
        return _dot(kc, qtb)

    def softmax(slot, m_old):
        s = s_refs[slot][...]
        m_new = jnp.maximum(m_old, jnp.max(s, axis=0, keepdims=True))
        p_refs[slot][...] = jnp.exp2(s - m_new).astype(BF16)
        return m_new, jnp.exp2(m_old - m_new)

    def attend(c, slot, alpha):
        acc_ref[...] = alpha * acc_ref[...] + _dot(vt_ref[0, 0, c], p_refs[slot][...])

    def step(c, slot, carry):
        m, alpha = carry
        s_refs[1 - slot][...] = scores(c + 1)
        m, alpha_new = softmax(slot, m)
        attend(c - 1, 1 - slot, alpha)
        return m, alpha_new

    acc_ref[...] = jnp.zeros(acc_ref.shape, F32)
    s_refs[0][...] = scores(0)
    s_refs[1][...] = scores(1)
    carry = softmax(0, jnp.full((1, 2 * tq), -jnp.inf, F32))

    n_steady = n_chunks - 2
    unroll = min(ATTN_UNROLL, n_steady)
    n_loops = n_steady // unroll if unroll else 0

    def body(j, carry):
        c = unroll * j + 1
        for i in range(unroll):
            carry = step(c + i, (1 + i) % 2, carry)
        return carry

    if n_loops:
        carry = lax.fori_loop(0, n_loops, body, carry)
    for c in range(n_loops * unroll + 1, n_chunks - 1):
        carry = step(c, c % 2, carry)
    m, alpha = carry
    last = (n_chunks - 1) % 2
    _, alpha_last = softmax(last, m)
    attend(n_chunks - 2, 1 - last, alpha)
    attend(n_chunks - 1, last, alpha_last)

    lp = lam_ref[...]
    lam = (jnp.exp(jnp.sum(lp[0:1] * lp[1:2], axis=-1, keepdims=True))
           - jnp.exp(jnp.sum(lp[2:3] * lp[3:4], axis=-1, keepdims=True)) + lam_init)
    acc = acc_ref[...]
    acc = acc[:V_DIM] / acc[V_DIM:V_DIM + 1]
    o = (acc[:, :tq] - lam * acc[:, tq:]).T
    o = _rms(o) * g_ref[...] * (1.0 - lam_init)
    o_ref[0] = o.astype(o_ref.dtype)


def _diff_attention(qt, k, vt, lam_params, subln_g, lam_init):
    b, n_heads, l, _ = k.shape
    tq = min(ATTN_TQ, l)
    n_chunks, v_rows, tk = vt.shape[2:]
    assert n_chunks % 2 == 0 and ATTN_UNROLL % 2 == 0, "double-buffer slots need an even chunk count"
    return pl.pallas_call(
        functools.partial(_attn_kernel, lam_init=lam_init),
        grid=(b, n_heads, l // tq),
        in_specs=[
            pl.BlockSpec((1, 1, V_DIM, tq), lambda bi, h, qi: (bi, h, 0, qi)),
            pl.BlockSpec((1, 1, l, V_DIM), lambda bi, h, qi: (bi, h, 0, 0)),
            pl.BlockSpec((1, 1) + vt.shape[2:], lambda bi, h, qi: (bi, h, 0, 0, 0)),
            pl.BlockSpec(lam_params.shape, lambda bi, h, qi: (0, 0)),
            pl.BlockSpec((1, V_DIM), lambda bi, h, qi: (0, 0)),
        ],
        out_specs=pl.BlockSpec((1, tq, V_DIM), lambda bi, h, qi: (bi, qi, h)),
        out_shape=jax.ShapeDtypeStruct((b, l, n_heads * V_DIM), BF16),
        scratch_shapes=[pltpu.VMEM((tk, 2 * tq), F32)] * 2 + [pltpu.VMEM((tk, 2 * tq), BF16)] * 2
                       + [pltpu.VMEM((v_rows, 2 * tq), F32)],
        compiler_params=_cparams("parallel", "parallel", "parallel"),
        name="diff_attention",
    )(qt, k, vt, lam_params, subln_g)


def _ssm_prep_kernel(arow_ref, acol_ref, bt_ref, ct_ref, m_ref, win_ref, wout_ref, dec_ref, *, chunk):
    t_len = chunk
    w = t_len * GROUP_CH
    w2 = 2 * w
    arow = arow_ref[0, 0]
    acol = acol_ref[0, 0]
    fwd_lanes = lax.broadcasted_iota(jnp.int32, (1, LANES), 1) < N_STATE

    ar = arow[0:1]
    ai = arow[1:2]
    dt = jnp.exp(arow[2:3])
    zr = dt * ar
    zi = dt * ai
    mag = jnp.exp(zr)
    nr = mag * jnp.cos(zi) - 1.0
    ni = mag * jnp.sin(zi)
    den = ar * ar + ai * ai
    fr = (nr * ar + ni * ai) / den
    fi = (ni * ar - nr * ai) / den
    br = bt_ref[0, 0, 0]
    bi = bt_ref[0, 0, 1]
    bbr = fr * br - fi * bi
    bbi = fr * bi + fi * br

    s_of_row = lax.shift_right_logical(lax.broadcasted_iota(jnp.int32, (w, t_len), 0), 4)
    j_of_col = lax.broadcasted_iota(jnp.int32, (w, t_len), 1)
    rep_fwd = _onehot(j_of_col == (t_len - 1 - s_of_row))
    rep_bwd = _onehot(j_of_col == s_of_row)
    n_rows = lax.broadcasted_iota(jnp.int32, (t_len, 1), 0).astype(F32)
    pm = jnp.exp(n_rows * zr)

    def expand(p):
        return jnp.where(fwd_lanes, _select_rows(rep_fwd, p), _select_rows(rep_bwd, p))

    e_re = expand(pm * jnp.cos(n_rows * zi))
    e_im = expand(pm * jnp.sin(n_rows * zi))
    b_re = jnp.tile(bbr, (t_len, 1))
    b_im = jnp.tile(bbi, (t_len, 1))
    win_ref[0, 0, :, :LANES] = (e_re * b_re - e_im * b_im).astype(win_ref.dtype)
    win_ref[0, 0, :, LANES:] = (e_re * b_im + e_im * b_re).astype(win_ref.dtype)
    dm = jnp.exp(t_len * zr)
    dec_ref[0, 0, 0:1, :] = dm * jnp.cos(t_len * zi)
    dec_ref[0, 0, 1:2, :] = dm * jnp.sin(t_len * zi)

    def lane_maps(width):
        lane = lax.broadcasted_iota(jnp.int32, (LANES, width), 1)
        return (lax.shift_right_logical(lane, 4), jnp.bitwise_and(lane, GROUP_CH - 1),
                lax.broadcasted_iota(jnp.int32, (LANES, width), 0))

    lag_idx, ch_idx, jrow = lane_maps(w2)
    lag_idx_w, _, jrow_w = lane_maps(w)
    tile_ch = _onehot(jrow == ch_idx)
    n_lanes = jnp.minimum(lax.broadcasted_iota(jnp.int32, (1, LANES), 1), t_len).astype(F32)

    lag_tables = []
    out_tables = []
    for d in range(2):
        dtc = jnp.exp(acol[:, 4 + d:5 + d])
        zrc = dtc * acol[:, d:d + 1]
        zic = dtc * acol[:, 2 + d:3 + d]
        ptm = jnp.exp(zrc * n_lanes)
        pt_re = ptm * jnp.cos(zic * n_lanes)
        pt_im = ptm * jnp.sin(zic * n_lanes)
        c_re = _select_cols(ct_ref[0, 0, d], tile_ch)
        c_im = _select_cols(ct_ref[0, 0, 2 + d], tile_ch)

        def table(rep):
            width = rep.shape[1]
            p_re = _select_cols(pt_re, rep)
            p_im = _select_cols(pt_im, rep)
            cr = c_re[:, :width]
            ci = c_im[:, :width]
            return p_re * cr - p_im * ci, p_re * ci + p_im * cr

        if d == 0:
            power = lag_idx - (t_len - 1)
        else:
            power = (t_len - 1) - lag_idx
        power = jnp.where(power >= 0, power, -1)
        f_re, f_im = table(_onehot(jrow == power))
        lag_tables.append((f_re, f_im))
        if d == 0:
            out_tables.append((f_re[:, w:], f_im[:, w:]))
        else:
            out_tables.append(table(_onehot(jrow_w == (t_len - lag_idx_w))))

    for i, o in enumerate((out_tables[0][0], out_tables[1][0], -out_tables[0][1], -out_tables[1][1])):
        wout_ref[0, 0, i * N_STATE:(i + 1) * N_STATE, :] = o.astype(wout_ref.dtype)

    lhs = jnp.concatenate([bbr, -bbi], axis=1)
    rhs = jnp.concatenate([lag_tables[0][0], lag_tables[1][0], lag_tables[0][1], lag_tables[1][1]], axis=0)
    strip = _dot_f32(lhs, rhs)

    per_tile = LANES // GROUP_CH
    for r in range(per_tile):
        rolled = strip if r == 0 else pltpu.roll(strip, w2 - r * GROUP_CH, 1)
        for s in range(t_len):
            if (t_len - 1 - s) % per_tile == r:
                q = (t_len - 1 - s) // per_tile
                m_ref[0, 0, s * GROUP_CH:(s + 1) * GROUP_CH, :] = (
                    rolled[:, q * LANES:q * LANES + w].astype(m_ref.dtype))


def _ssm_prep(a_re, a_im, log_dt, b_re, b_im, c_re, c_im, chunk):
    depth, _, n_groups, n_state = a_re.shape
    w = chunk * GROUP_CH
    gd = lambda x: jnp.swapaxes(x, 1, 2)
    ldt = jnp.broadcast_to(gd(log_dt)[..., None], (depth, n_groups, 2, n_state))
    packed = lambda x: x.reshape(depth, n_groups, 1, 2 * n_state)
    arow = jnp.concatenate([packed(gd(a_re)), packed(gd(a_im)), packed(ldt),
                            jnp.zeros((depth, n_groups, 5, 2 * n_state), F32)], axis=2)
    acol = jnp.swapaxes(jnp.concatenate([gd(a_re), gd(a_im), ldt, jnp.zeros_like(ldt)], axis=2), 2, 3)
    bt = jnp.stack([gd(b_re), gd(b_im)], axis=2)
    bt = bt.transpose(0, 1, 2, 5, 3, 4).reshape(depth, n_groups, 2, GROUP_CH, 2 * n_state)
    ct = jnp.swapaxes(jnp.concatenate([gd(c_re), gd(c_im)], axis=2), 3, 4)
    ct = jnp.pad(ct, ((0, 0),) * 4 + ((0, LANES - GROUP_CH),))
    blk = lambda *tail: pl.BlockSpec((1, 1) + tail, lambda l, g: (l, g) + (0,) * len(tail))
    return pl.pallas_call(
        functools.partial(_ssm_prep_kernel, chunk=chunk),
        grid=(depth, n_groups),
        in_specs=[blk(8, LANES), blk(n_state, 8), blk(2, GROUP_CH, LANES), blk(4, n_state, LANES)],
        out_specs=[blk(w, w), blk(w, 4 * n_state), blk(4 * n_state, w), blk(2, LANES)],
        out_shape=[
            jax.ShapeDtypeStruct((depth, n_groups, w, w), BF16),
            jax.ShapeDtypeStruct((depth, n_groups, w, 4 * n_state), BF16),
            jax.ShapeDtypeStruct((depth, n_groups, 4 * n_state, w), BF16),
            jax.ShapeDtypeStruct((depth, n_groups, 2, LANES), F32),
        ],
        compiler_params=_cparams("parallel", "parallel"),
        name="ssm_prep",
    )(arow, acol, bt, ct)


def _ssm_state_kernel(u_ref, win_ref, sre_ref, sim_ref):
    s = _dot(u_ref[0], win_ref[0, 0])
    sre_ref[...] = s[:, :LANES]
    sim_ref[...] = s[:, LANES:]


def _ssm_out_kernel(u_ref, hfr_ref, hfi_ref, hbr_ref, hbi_ref, m_ref, wout_ref, y_ref):
    fwd = lax.broadcasted_iota(jnp.int32, hfr_ref.shape, 1) < N_STATE
    h = jnp.concatenate([jnp.where(fwd, hfr_ref[...], hbr_ref[...]),
                         jnp.where(fwd, hfi_ref[...], hbi_ref[...])], axis=1).astype(BF16)
    y_ref[0] = (_dot(u_ref[0], m_ref[0, 0]) + _dot(h, wout_ref[0, 0])).astype(y_ref.dtype)


def _ssm_scan_kernel(sre_ref, sim_ref, dre_ref, dim_ref, hfr_ref, hfi_ref, hbr_ref, hbi_ref, *, batch):
    rows, lanes = sre_ref.shape
    per_slab = SUBLANES // batch
    n_slabs = rows // SUBLANES
    shape = (SUBLANES, lanes)
    fwd = jnp.bitwise_and(lax.broadcasted_iota(jnp.int32, shape, 1), N_STATE) == 0
    blk = lax.shift_right_logical(lax.broadcasted_iota(jnp.int32, shape, 0), batch.bit_length() - 1)
    dr = dre_ref[...]
    di = dim_ref[...]

    def spread(s, q):
        part = jnp.where(blk == q, s, 0.0)
        out = part
        for r in range(1, per_slab):
            out = out + pltpu.roll(part, r * batch, 0)
        return out

    def step(i, carry):
        xr, xi = carry
        rf = pl.multiple_of(i * SUBLANES, SUBLANES)
        rb = pl.multiple_of((n_slabs - 1 - i) * SUBLANES, SUBLANES)
        sfr, sfi = sre_ref[pl.ds(rf, SUBLANES), :], sim_ref[pl.ds(rf, SUBLANES), :]
        sbr, sbi = sre_ref[pl.ds(rb, SUBLANES), :], sim_ref[pl.ds(rb, SUBLANES), :]
        hfr = hfi = hbr = hbi = jnp.zeros(shape, F32)
        for q in range(per_slab):
            qb = per_slab - 1 - q
            hfr, hfi = jnp.where(blk == q, xr, hfr), jnp.where(blk == q, xi, hfi)
            hbr, hbi = jnp.where(blk == qb, xr, hbr), jnp.where(blk == qb, xi, hbi)
            in_r = jnp.where(fwd, spread(sfr, q), spread(sbr, qb))
            in_i = jnp.where(fwd, spread(sfi, q), spread(sbi, qb))
            xr, xi = dr * xr - di * xi + in_r, dr * xi + di * xr + in_i
        hfr_ref[pl.ds(rf, SUBLANES), :] = hfr
        hfi_ref[pl.ds(rf, SUBLANES), :] = hfi
        hbr_ref[pl.ds(rb, SUBLANES), :] = hbr
        hbi_ref[pl.ds(rb, SUBLANES), :] = hbi
        return xr, xi

    zeros = jnp.zeros(shape, F32)
    lax.fori_loop(0, n_slabs, step, (zeros, zeros))


def _ssm_branch(u, layer, m_all, win_all, wout_all, dec_all, batch, seq_len, chunk):
    n, d_u = u.shape
    n_groups = d_u // GROUP_CH
    w = chunk * GROUP_CH
    n_chunks = seq_len // chunk
    rows = batch * n_chunks
    st = 4 * N_STATE
    assert SUBLANES % batch == 0 and rows % SUBLANES == 0

    ug = (u.reshape(batch, n_chunks, chunk, n_groups, GROUP_CH).transpose(3, 1, 0, 2, 4)
          .reshape(n_groups, rows, w))
    state = jax.ShapeDtypeStruct((rows, n_groups * LANES), F32)
    col = lambda g: (0, g)
    sre, sim = pl.pallas_call(
        _ssm_state_kernel,
        grid=(n_groups,),
        in_specs=[pl.BlockSpec((1, rows, w), lambda g: (g, 0, 0)),
                  pl.BlockSpec((1, 1, w, st), lambda g: (layer, g, 0, 0))],
        out_specs=[pl.BlockSpec((rows, LANES), col)] * 2,
        out_shape=[state] * 2,
        compiler_params=_cparams("parallel"),
        name="ssm_chunk_state",
    )(ug, win_all)

    dec = dec_all[layer]
    dre = dec[:, 0, :].reshape(1, n_groups * LANES)
    dim = dec[:, 1, :].reshape(1, n_groups * LANES)
    sl = min(SCAN_LANES, n_groups * LANES)
    hs = pl.pallas_call(
        functools.partial(_ssm_scan_kernel, batch=batch),
        grid=(n_groups * LANES // sl,),
        in_specs=[pl.BlockSpec((rows, sl), col)] * 2 + [pl.BlockSpec((1, sl), col)] * 2,
        out_specs=[pl.BlockSpec((rows, sl), col)] * 4,
        out_shape=[state] * 4,
        compiler_params=_cparams("parallel"),
        name="ssm_chunk_scan",
    )(sre, sim, dre, dim)

    y = pl.pallas_call(
        _ssm_out_kernel,
        grid=(n_groups,),
        in_specs=[pl.BlockSpec((1, rows, w), lambda g: (g, 0, 0))]
                 + [pl.BlockSpec((rows, LANES), col)] * 4
                 + [pl.BlockSpec((1, 1, w, w), lambda g: (layer, g, 0, 0)),
                    pl.BlockSpec((1, 1, st, w), lambda g: (layer, g, 0, 0))],
        out_specs=pl.BlockSpec((1, rows, w), lambda g: (g, 0, 0)),
        out_shape=jax.ShapeDtypeStruct((n_groups, rows, w), BF16),
        compiler_params=_cparams("parallel"),
        name="ssm_chunk_output",
    )(ug, *hs, m_all, wout_all)
    return (y.reshape(n_groups, n_chunks, batch, chunk, GROUP_CH).transpose(2, 1, 3, 0, 4)
            .reshape(n, d_u))


def _merge_kernel(x_ref, mod_ref, o_ref, y_ref, u_ref, ga_ref, gs_ref, d_ref,
                  wglu_ref, bglu_ref, wattn_ref, wo_ref, out_ref):
    d = x_ref.shape[1]
    y = y_ref[...] + d_ref[...] * u_ref[...].astype(F32)
    z = jax.nn.gelu(y, approximate=True).astype(BF16)
    glu = _dot(z, wglu_ref[...]) + bglu_ref[...]
    y_ssm = glu[:, :d] * jax.nn.sigmoid(glu[:, d:])
    y_attn = _dot(o_ref[...], wattn_ref[...])
    merged = (ga_ref[...].astype(F32) * y_attn + gs_ref[...].astype(F32) * y_ssm).astype(BF16)
    out_ref[...] = x_ref[...] + mod_ref[0][2:3] * _dot(merged, wo_ref[...])


def _merge(x, mod, o, y, u, ga, gs, ssm_d, w_glu, b_glu, w_attn, w_o, layer, seq_len):
    n, d = x.shape
    tm = min(TOKEN_TILE, seq_len)
    tps = seq_len // tm
    tok = lambda i: (i, 0)
    full = lambda a: pl.BlockSpec(a.shape, lambda i: (0, 0))
    of_layer = lambda a: pl.BlockSpec((None,) + a.shape[1:], lambda i: (layer, 0, 0))
    return pl.pallas_call(
        _merge_kernel,
        grid=(n // tm,),
        in_specs=[
            pl.BlockSpec((tm, d), tok),
            pl.BlockSpec((1, 6, d), lambda i: (i // tps, 0, 0)),
            pl.BlockSpec((tm, o.shape[1]), tok),
            pl.BlockSpec((tm, y.shape[1]), tok),
            pl.BlockSpec((tm, u.shape[1]), tok),
            pl.BlockSpec((tm, d), tok),
            pl.BlockSpec((tm, d), tok),
            full(ssm_d), of_layer(w_glu), full(b_glu), of_layer(w_attn), of_layer(w_o),
        ],
        out_specs=pl.BlockSpec((tm, d), tok),
        out_shape=jax.ShapeDtypeStruct((n, d), F32),
        input_output_aliases={0: 0},
        compiler_params=_cparams("parallel"),
        name="merge_out_projection",
    )(x, mod, o, y, u, ga, gs, ssm_d, w_glu, b_glu, w_attn, w_o)


def _ffn_kernel(x_ref, mod_ref, g_ref, wg_ref, wu_ref, wd_ref, fg_ref, out_ref, h_ref, acc_ref,
                *, final_norm):
    j = pl.program_id(1)

    @pl.when(j == 0)
    def _():
        mod = mod_ref[0]
        h_ref[...] = (_rms(x_ref[...]) * g_ref[...] * (1.0 + mod[4:5]) + mod[3:4]).astype(BF16)
        acc_ref[...] = jnp.zeros(acc_ref.shape, F32)

    h = h_ref[...]
    gate = _dot(h, wg_ref[...])
    up = _dot(h, wu_ref[...])
    t = (gate * jax.nn.sigmoid(gate) * up).astype(BF16)
    acc_ref[...] += _dot(t, wd_ref[...])

    @pl.when(j == pl.num_programs(1) - 1)
    def _():
        xn = x_ref[...] + mod_ref[0][5:6] * acc_ref[...]
        if final_norm:
            xn = _rms(xn) * fg_ref[...]
        out_ref[...] = xn


def _ffn(x, mod, g, w_in, w_out, final_g, layer, seq_len, final_norm):
    n, d = x.shape
    d_ff = w_out.shape[1]
    tm = min(TOKEN_TILE, seq_len)
    tps = seq_len // tm
    tf = FFN_TILE
    nf = d_ff // tf
    return pl.pallas_call(
        functools.partial(_ffn_kernel, final_norm=final_norm),
        grid=(n // tm, nf),
        in_specs=[
            pl.BlockSpec((tm, d), lambda i, j: (i, 0)),
            pl.BlockSpec((1, 6, d), lambda i, j: (i // tps, 0, 0)),
            pl.BlockSpec((1, d), lambda i, j: (0, 0)),
            pl.BlockSpec((None, d, tf), lambda i, j: (layer, 0, j)),
            pl.BlockSpec((None, d, tf), lambda i, j: (layer, 0, nf + j)),
            pl.BlockSpec((None, tf, d), lambda i, j: (layer, j, 0)),
            pl.BlockSpec((1, d), lambda i, j: (0, 0)),
        ],
        out_specs=pl.BlockSpec((tm, d), lambda i, j: (i, 0)),
        out_shape=jax.ShapeDtypeStruct((n, d), F32),
        scratch_shapes=[pltpu.VMEM((tm, d), BF16), pltpu.VMEM((tm, d), F32)],
        input_output_aliases={0: 0},
        compiler_params=_cparams("parallel", "arbitrary"),
        name="ffn",
    )(x, mod, g, w_in, w_in, w_out, final_g)


def _rope_tables(seq_len):
    inv = 1.0 / (ROPE_THETA ** (jnp.arange(0, HEAD_DIM, 2, dtype=F32) / HEAD_DIM))
    ang = jnp.arange(seq_len, dtype=F32)[:, None] * inv[None, :]
    cos = jnp.cos(ang)
    sin = jnp.sin(ang)
    reps = LANES // HEAD_DIM
    cos_t = jnp.tile(jnp.concatenate([cos, cos], axis=-1), (1, reps))
    sin_t = jnp.tile(jnp.concatenate([-sin, sin], axis=-1), (1, reps))
    return cos_t, sin_t


def _lambda_init(layer):
    return 0.8 - 0.6 * math.exp(-0.3 * layer)


def _trunk(x, mod_all, p, ssm_ops):
    batch, seq_len, d = x.shape
    depth = p["w_in"].shape[0]
    d_v = N_HEADS * V_DIM
    d_u = p["ssm_d"].shape[1]
    cos, sin = _rope_tables(seq_len)
    x = x.reshape(batch * seq_len, d)
    row = lambda a: a.reshape(1, -1)
    for i in range(depth):
        mod = mod_all[i].reshape(batch, 6, d)
        qt, k, vt, u, ga, gs = _in_projection(x, mod, row(p["norm1_g"][i]), cos, sin, p["w_in"], i,
                                              batch, seq_len, d_u)
        lam_params = jnp.stack([p["lam_q1"][i], p["lam_k1"][i], p["lam_q2"][i], p["lam_k2"][i]])
        o = _diff_attention(qt, k, vt, lam_params, row(p["subln_g"][i]), _lambda_init(i))
        y = _ssm_branch(u, i, *ssm_ops, batch, seq_len, SSM_CHUNK)
        x = _merge(x, mod, o.reshape(batch * seq_len, d_v), y, u, ga, gs, row(p["ssm_d"][i]),
                   p["w_glu"], row(p["b_glu"][i]), p["w_attn_br"], p["w_o"], i, seq_len)
        x = _ffn(x, mod, row(p["norm2_g"][i]), p["w_ffn_in"], p["w_ffn_out"],
                 row(p["final_g"]), i, seq_len, final_norm=(i == depth - 1))
    return x.reshape(batch, seq_len, d)


def kernel(x_prompt, x_sample, c_prompt, c_sample, w_mod, b_mod, norm1_g, w_in, lam_q1, lam_k1, lam_q2, lam_k2, subln_g, w_attn_br, ssm_a_re, ssm_a_im, ssm_log_dt, ssm_b_re, ssm_b_im, ssm_c_re, ssm_c_im, ssm_d, w_glu, b_glu, w_o, norm2_g, w_ffn_in, w_ffn_out, final_g):
    bp, bs = c_prompt.shape[0], c_sample.shape[0]
    pad = -(bp + bs) % 8
    c_all = jnp.concatenate([c_prompt, c_sample, jnp.zeros((pad, c_prompt.shape[1]), F32)], axis=0)
    mod_all = _modulation(c_all, w_mod, b_mod)
    ssm_ops = _ssm_prep(ssm_a_re, ssm_a_im, ssm_log_dt, ssm_b_re, ssm_b_im, ssm_c_re, ssm_c_im, SSM_CHUNK)
    p = dict(
        norm1_g=norm1_g, w_in=w_in.astype(BF16), lam_q1=lam_q1, lam_k1=lam_k1, lam_q2=lam_q2, lam_k2=lam_k2,
        subln_g=subln_g, w_attn_br=w_attn_br.astype(BF16), ssm_d=ssm_d, w_glu=w_glu.astype(BF16),
        b_glu=b_glu, w_o=w_o.astype(BF16), norm2_g=norm2_g, w_ffn_in=w_ffn_in.astype(BF16),
        w_ffn_out=w_ffn_out.astype(BF16), final_g=final_g)
    y_prompt = _trunk(x_prompt, mod_all[:, :bp], p, ssm_ops)
    y_sample = _trunk(x_sample, mod_all[:, bp:bp + bs], p, ssm_ops)
    return (y_prompt, y_sample)
```

```python
import functools
import math

import jax
import jax.numpy as jnp
from jax import lax
from jax.experimental import pallas as pl
from jax.experimental.pallas import tpu as pltpu

F32 = jnp.float32
BF16 = jnp.bfloat16

N_HEADS = 4
HEAD_DIM = 64
V_DIM = 2 * HEAD_DIM
GROUP_CH = 16
N_STATE = 64
ROPE_THETA = 10000.0
EPS = 1e-6
Q_SCALE = HEAD_DIM ** -0.5 * math.log2(math.e)
V_PAD = 16

VMEM_LIMIT_BYTES = 56 * 1024 * 1024
LANES = 128
SUBLANES = 8

TOKEN_TILE = 512
ATTN_TQ = 256
ATTN_TK = 256
ATTN_UNROLL = 10
SSM_CHUNK = 64
FFN_TILE = 1408
SCAN_LANES = 1024
RELAYOUT_TILE = 1024


def _cparams(*sem):
    return pltpu.CompilerParams(dimension_semantics=sem, vmem_limit_bytes=VMEM_LIMIT_BYTES)


def _dot(a, b):
    return jnp.dot(a, b, preferred_element_type=F32)


def _split3(x):
    hi = x.astype(BF16)
    r1 = x - hi.astype(F32)
    mid = r1.astype(BF16)
    lo = (r1 - mid.astype(F32)).astype(BF16)
    return hi, mid, lo


def _dot_f32(a, b):
    a_hi = a.astype(BF16)
    a_lo = (a - a_hi.astype(F32)).astype(BF16)
    b_hi = b.astype(BF16)
    b_lo = (b - b_hi.astype(F32)).astype(BF16)
    return _dot(a_hi, b_hi) + (_dot(a_hi, b_lo) + _dot(a_lo, b_hi))


def _select_cols(x, sel):
    hi, mid, lo = _split3(x)
    return _dot(hi, sel) + (_dot(mid, sel) + _dot(lo, sel))


def _select_rows(sel, x):
    hi, mid, lo = _split3(x)
    return _dot(sel, hi) + (_dot(sel, mid) + _dot(sel, lo))


def _onehot(cond):
    return jnp.where(cond, 1.0, 0.0).astype(BF16)


def _rms(x):
    return x * lax.rsqrt(jnp.mean(x * x, axis=-1, keepdims=True) + EPS)


def _mod_kernel(c_ref, w_ref, b_ref, o_ref):
    c = c_ref[...]
    s = c * jax.nn.sigmoid(c)
    o_ref[0] = _dot_f32(s, w_ref[0]) + b_ref[0]


def _modulation(c_all, w_mod, b_mod):
    depth, d, n6 = w_mod.shape
    bp = c_all.shape[0]
    tn = 1536
    return pl.pallas_call(
        _mod_kernel,
        grid=(depth, n6 // tn),
        in_specs=[
            pl.BlockSpec((bp, d), lambda l, j: (0, 0)),
            pl.BlockSpec((1, d, tn), lambda l, j: (l, 0, j)),
            pl.BlockSpec((1, 1, tn), lambda l, j: (l, 0, j)),
        ],
        out_specs=pl.BlockSpec((1, bp, tn), lambda l, j: (l, 0, j)),
        out_shape=jax.ShapeDtypeStruct((depth, bp, n6), F32),
        compiler_params=_cparams("parallel", "parallel"),
        name="modulation",
    )(c_all, w_mod, b_mod.reshape(depth, 1, n6))


def _inproj_kernel(x_ref, mod_ref, g_ref, cos_ref, sin_ref, w_ref,
                   qt_ref, k_ref, vt_ref, u_ref, ga_ref, gs_ref):
    d = x_ref.shape[1]
    n_heads = k_ref.shape[1]
    d_qk = n_heads * V_DIM
    mod = mod_ref[0]
    h = (_rms(x_ref[...]) * g_ref[...] * (1.0 + mod[1:2]) + mod[0:1]).astype(BF16)

    qk = _dot(h, w_ref[:, 0:2 * d_qk])
    reps = 2 * d_qk // LANES
    cos = jnp.tile(cos_ref[...], (1, reps))
    sin = jnp.tile(sin_ref[...], (1, reps))
    lane = lax.broadcasted_iota(jnp.int32, qk.shape, 1)
    first_half = jnp.bitwise_and(lane, HEAD_DIM // 2) == 0
    half = HEAD_DIM // 2
    rot = jnp.where(first_half,
                    pltpu.roll(qk, 2 * d_qk - half, 1),
                    pltpu.roll(qk, half, 1))
    qk = qk * cos + rot * sin
    for hd in range(n_heads):
        qh = qk[:, hd * V_DIM:(hd + 1) * V_DIM] * Q_SCALE
        qt_ref[0, hd] = qh.T.astype(BF16)
        k_ref[0, hd] = qk[:, d_qk + hd * V_DIM:d_qk + (hd + 1) * V_DIM].astype(BF16)

    off = 2 * d_qk
    d_v = n_heads * V_DIM
    d_u = u_ref.shape[1]
    vu = _dot(h, w_ref[:, off:off + d_v + d_u])
    tk = vt_ref.shape[4]
    tail_row = lax.broadcasted_iota(jnp.int32, (V_PAD, tk), 0)
    ones_row = jnp.where(tail_row == 0, 1.0, 0.0).astype(BF16)
    for hd in range(n_heads):
        for c in range(vt_ref.shape[2]):
            vt_ref[0, hd, c, :V_DIM, :] = vu[c * tk:(c + 1) * tk, hd * V_DIM:(hd + 1) * V_DIM].T.astype(BF16)
            vt_ref[0, hd, c, V_DIM:, :] = ones_row
    u_ref[...] = vu[:, d_v:].astype(BF16)
    off += d_v + d_u
    ga_ref[...] = jax.nn.sigmoid(_dot(h, w_ref[:, off:off + d])).astype(BF16)
    gs_ref[...] = jax.nn.sigmoid(_dot(h, w_ref[:, off + d:off + 2 * d])).astype(BF16)


def _in_projection(x, mod, g, cos, sin, w_all, layer, batch, seq_len, d_u):
    n, d = x.shape
    tm = min(TOKEN_TILE, seq_len)
    tps = seq_len // tm
    tk = min(ATTN_TK, tm)
    tok = lambda i: (i, 0)
    bf = lambda *shape: jax.ShapeDtypeStruct(shape, BF16)
    return pl.pallas_call(
        _inproj_kernel,
        grid=(n // tm,),
        in_specs=[
            pl.BlockSpec((tm, d), tok),
            pl.BlockSpec((1, 6, d), lambda i: (i // tps, 0, 0)),
            pl.BlockSpec((1, d), lambda i: (0, 0)),
            pl.BlockSpec((tm, LANES), lambda i: (i % tps, 0)),
            pl.BlockSpec((tm, LANES), lambda i: (i % tps, 0)),
            pl.BlockSpec((None,) + w_all.shape[1:], lambda i: (layer, 0, 0)),
        ],
        out_specs=[
            pl.BlockSpec((1, N_HEADS, V_DIM, tm), lambda i: (i // tps, 0, 0, i % tps)),
            pl.BlockSpec((1, N_HEADS, tm, V_DIM), lambda i: (i // tps, 0, i % tps, 0)),
            pl.BlockSpec((1, N_HEADS, tm // tk, V_DIM + V_PAD, tk), lambda i: (i // tps, 0, i % tps, 0, 0)),
            pl.BlockSpec((tm, d_u), tok),
            pl.BlockSpec((tm, d), tok),
            pl.BlockSpec((tm, d), tok),
        ],
        out_shape=[
            bf(batch, N_HEADS, V_DIM, seq_len),
            bf(batch, N_HEADS, seq_len, V_DIM),
            bf(batch, N_HEADS, seq_len // tk, V_DIM + V_PAD, tk),
            bf(n, d_u), bf(n, d), bf(n, d),
        ],
        compiler_params=_cparams("parallel"),
        name="in_projection",
    )(x, mod, g, cos, sin, w_all)


def _attn_kernel(qt_ref, k_ref, vt_ref, lam_ref, g_ref, o_ref, s0_ref, s1_ref, p0_ref, p1_ref, acc_ref,
                 *, lam_init):
    n_chunks, _, tk = vt_ref.shape[2:]
    tq = qt_ref.shape[3]
    s_refs = (s0_ref, s1_ref)
    p_refs = (p0_ref, p1_ref)
    qt = qt_ref[0, 0].astype(F32)
    row = lax.broadcasted_iota(jnp.int32, qt.shape, 0)
    qtb = jnp.concatenate([jnp.where(row < HEAD_DIM, qt, 0.0), jnp.where(row >= HEAD_DIM, qt, 0.0)],
                          axis=1).astype(BF16)

    def scores(c):
        kc = k_ref[0, 0, pl.ds(pl.multiple_of(c * tk, tk), tk), :]
        return _dot(kc, qtb)

    def softmax(slot, m_old):
        s = s_refs[slot][...]
        m_new = jnp.maximum(m_old, jnp.max(s, axis=0, keepdims=True))
        p_refs[slot][...] = jnp.exp2(s - m_new).astype(BF16)
        return m_new, jnp.exp2(m_old - m_new)

    def attend(c, slot, alpha):
        acc_ref[...] = alpha * acc_ref[...] + _dot(vt_ref[0, 0, c], p_refs[slot][...])

    def step(c, slot, carry):
        m, alpha = carry
        s_refs[1 - slot][...] = scores(c + 1)
        m, alpha_new = softmax(slot, m)
        attend(c - 1, 1 - slot, alpha)
        return m, alpha_new

    acc_ref[...] = jnp.zeros(acc_ref.shape, F32)
    s_refs[0][...] = scores(0)
    s_refs[1][...] = scores(1)
    carry = softmax(0, jnp.full((1, 2 * tq), -jnp.inf, F32))

    n_steady = n_chunks - 2
    unroll = min(ATTN_UNROLL, n_steady)
    n_loops = n_steady // unroll if unroll else 0

    def body(j, carry):
        c = unroll * j + 1
        for i in range(unroll):
            carry = step(c + i, (1 + i) % 2, carry)
        return carry

    if n_loops:
        carry = lax.fori_loop(0, n_loops, body, carry)
    for c in range(n_loops * unroll + 1, n_chunks - 1):
        carry = step(c, c % 2, carry)
    m, alpha = carry
    last = (n_chunks - 1) % 2
    _, alpha_last = softmax(last, m)
    attend(n_chunks - 2, 1 - last, alpha)
    attend(n_chunks - 1, last, alpha_last)

    lp = lam_ref[...]
    lam = (jnp.exp(jnp.sum(lp[0:1] * lp[1:2], axis=-1, keepdims=True))
           - jnp.exp(jnp.sum(lp[2:3] * lp[3:4], axis=-1, keepdims=True)) + lam_init)
    acc = acc_ref[...]
    acc = acc[:V_DIM] / acc[V_DIM:V_DIM + 1]
    o = (acc[:, :tq] - lam * acc[:, tq:]).T
    o = _rms(o) * g_ref[...] * (1.0 - lam_init)
    o_ref[0] = o.astype(o_ref.dtype)


def _diff_attention(qt, k, vt, lam_params, subln_g, lam_init):
    b, n_heads, l, _ = k.shape
    tq = min(ATTN_TQ, l)
    n_chunks, v_rows, tk = vt.shape[2:]
    assert n_chunks % 2 == 0 and ATTN_UNROLL % 2 == 0, "double-buffer slots need an even chunk count"
    return pl.pallas_call(
        functools.partial(_attn_kernel, lam_init=lam_init),
        grid=(b, n_heads, l // tq),
        in_specs=[
            pl.BlockSpec((1, 1, V_DIM, tq), lambda bi, h, qi: (bi, h, 0, qi)),
            pl.BlockSpec((1, 1, l, V_DIM), lambda bi, h, qi: (bi, h, 0, 0)),
            pl.BlockSpec((1, 1) + vt.shape[2:], lambda bi, h, qi: (bi, h, 0, 0, 0)),
            pl.BlockSpec(lam_params.shape, lambda bi, h, qi: (0, 0)),
            pl.BlockSpec((1, V_DIM), lambda bi, h, qi: (0, 0)),
        ],
        out_specs=pl.BlockSpec((1, tq, V_DIM), lambda bi, h, qi: (bi, qi, h)),
        out_shape=jax.ShapeDtypeStruct((b, l, n_heads * V_DIM), BF16),
        scratch_shapes=[pltpu.VMEM((tk, 2 * tq), F32)] * 2 + [pltpu.VMEM((tk, 2 * tq), BF16)] * 2
                       + [pltpu.VMEM((v_rows, 2 * tq), F32)],
        compiler_params=_cparams("parallel", "parallel", "parallel"),
        name="diff_attention",
    )(qt, k, vt, lam_params, subln_g)


def _ssm_prep_kernel(arow_ref, acol_ref, bt_ref, ct_ref, m_ref, win_ref, wout_ref, dec_ref, *, chunk):
    t_len = chunk
    w = t_len * GROUP_CH
    w2 = 2 * w
    arow = arow_ref[0, 0]
    acol = acol_ref[0, 0]
    fwd_lanes = lax.broadcasted_iota(jnp.int32, (1, LANES), 1) < N_STATE

    ar = arow[0:1]
    ai = arow[1:2]
    dt = jnp.exp(arow[2:3])
    zr = dt * ar
    zi = dt * ai
    mag = jnp.exp(zr)
    nr = mag * jnp.cos(zi) - 1.0
    ni = mag * jnp.sin(zi)
    den = ar * ar + ai * ai
    fr = (nr * ar + ni * ai) / den
    fi = (ni * ar - nr * ai) / den
    br = bt_ref[0, 0, 0]
    bi = bt_ref[0, 0, 1]
    bbr = fr * br - fi * bi
    bbi = fr * bi + fi * br

    s_of_row = lax.shift_right_logical(lax.broadcasted_iota(jnp.int32, (w, t_len), 0), 4)
    j_of_col = lax.broadcasted_iota(jnp.int32, (w, t_len), 1)
    rep_fwd = _onehot(j_of_col == (t_len - 1 - s_of_row))
    rep_bwd = _onehot(j_of_col == s_of_row)
    n_rows = lax.broadcasted_iota(jnp.int32, (t_len, 1), 0).astype(F32)
    pm = jnp.exp(n_rows * zr)

    def expand(p):
        return jnp.where(fwd_lanes, _select_rows(rep_fwd, p), _select_rows(rep_bwd, p))

    e_re = expand(pm * jnp.cos(n_rows * zi))
    e_im = expand(pm * jnp.sin(n_rows * zi))
    b_re = jnp.tile(bbr, (t_len, 1))
    b_im = jnp.tile(bbi, (t_len, 1))
    win_ref[0, 0, :, :LANES] = (e_re * b_re - e_im * b_im).astype(win_ref.dtype)
    win_ref[0, 0, :, LANES:] = (e_re * b_im + e_im * b_re).astype(win_ref.dtype)
    dm = jnp.exp(t_len * zr)
    dec_ref[0, 0, 0:1, :] = dm * jnp.cos(t_len * zi)
    dec_ref[0, 0, 1:2, :] = dm * jnp.sin(t_len * zi)

    def lane_maps(width):
        lane = lax.broadcasted_iota(jnp.int32, (LANES, width), 1)
        return (lax.shift_right_logical(lane, 4), jnp.bitwise_and(lane, GROUP_CH - 1),
                lax.broadcasted_iota(jnp.int32, (LANES, width), 0))

    lag_idx, ch_idx, jrow = lane_maps(w2)
    lag_idx_w, _, jrow_w = lane_maps(w)
    tile_ch = _onehot(jrow == ch_idx)
    n_lanes = jnp.minimum(lax.broadcasted_iota(jnp.int32, (1, LANES), 1), t_len).astype(F32)

    lag_tables = []
    out_tables = []
    for d in range(2):
        dtc = jnp.exp(acol[:, 4 + d:5 + d])
        zrc = dtc * acol[:, d:d + 1]
        zic = dtc * acol[:, 2 + d:3 + d]
        ptm = jnp.exp(zrc * n_lanes)
        pt_re = ptm * jnp.cos(zic * n_lanes)
        pt_im = ptm * jnp.sin(zic * n_lanes)
        c_re = _select_cols(ct_ref[0, 0, d], tile_ch)
        c_im = _select_cols(ct_ref[0, 0, 2 + d], tile_ch)

        def table(rep):
            width = rep.shape[1]
            p_re = _select_cols(pt_re, rep)
            p_im = _select_cols(pt_im, rep)
            cr = c_re[:, :width]
            ci = c_im[:, :width]
            return p_re * cr - p_im * ci, p_re * ci + p_im * cr

        if d == 0:
            power = lag_idx - (t_len - 1)
        else:
            power = (t_len - 1) - lag_idx
        power = jnp.where(power >= 0, power, -1)
        f_re, f_im = table(_onehot(jrow == power))
        lag_tables.append((f_re, f_im))
        if d == 0:
            out_tables.append((f_re[:, w:], f_im[:, w:]))
        else:
            out_tables.append(table(_onehot(jrow_w == (t_len - lag_idx_w))))

    for i, o in enumerate((out_tables[0][0], out_tables[1][0], -out_tables[0][1], -out_tables[1][1])):
        wout_ref[0, 0, i * N_STATE:(i + 1) * N_STATE, :] = o.astype(wout_ref.dtype)

    lhs = jnp.concatenate([bbr, -bbi], axis=1)
    rhs = jnp.concatenate([lag_tables[0][0], lag_tables[1][0], lag_tables[0][1], lag_tables[1][1]], axis=0)
    strip = _dot_f32(lhs, rhs)

    per_tile = LANES // GROUP_CH
    for r in range(per_tile):
        rolled = strip if r == 0 else pltpu.roll(strip, w2 - r * GROUP_CH, 1)
        for s in range(t_len):
            if (t_len - 1 - s) % per_tile == r:
                q = (t_len - 1 - s) // per_tile
                m_ref[0, 0, s * GROUP_CH:(s + 1) * GROUP_CH, :] = (
                    rolled[:, q * LANES:q * LANES + w].astype(m_ref.dtype))


def _ssm_prep(a_re, a_im, log_dt, b_re, b_im, c_re, c_im, chunk):
    depth, _, n_groups, n_state = a_re.shape
    w = chunk * GROUP_CH
    gd = lambda x: jnp.swapaxes(x, 1, 2)
    ldt = jnp.broadcast_to(gd(log_dt)[..., None], (depth, n_groups, 2, n_state))
    packed = lambda x: x.reshape(depth, n_groups, 1, 2 * n_state)
    arow = jnp.concatenate([packed(gd(a_re)), packed(gd(a_im)), packed(ldt),
                            jnp.zeros((depth, n_groups, 5, 2 * n_state), F32)], axis=2)
    acol = jnp.swapaxes(jnp.concatenate([gd(a_re), gd(a_im), ldt, jnp.zeros_like(ldt)], axis=2), 2, 3)
    bt = jnp.stack([gd(b_re), gd(b_im)], axis=2)
    bt = bt.transpose(0, 1, 2, 5, 3, 4).reshape(depth, n_groups, 2, GROUP_CH, 2 * n_state)
    ct = jnp.swapaxes(jnp.concatenate([gd(c_re), gd(c_im)], axis=2), 3, 4)
    ct = jnp.pad(ct, ((0, 0),) * 4 + ((0, LANES - GROUP_CH),))
    blk = lambda *tail: pl.BlockSpec((1, 1) + tail, lambda l, g: (l, g) + (0,) * len(tail))
    return pl.pallas_call(
        functools.partial(_ssm_prep_kernel, chunk=chunk),
        grid=(depth, n_groups),
        in_specs=[blk(8, LANES), blk(n_state, 8), blk(2, GROUP_CH, LANES), blk(4, n_state, LANES)],
        out_specs=[blk(w, w), blk(w, 4 * n_state), blk(4 * n_state, w), blk(2, LANES)],
        out_shape=[
            jax.ShapeDtypeStruct((depth, n_groups, w, w), BF16),
            jax.ShapeDtypeStruct((depth, n_groups, w, 4 * n_state), BF16),
            jax.ShapeDtypeStruct((depth, n_groups, 4 * n_state, w), BF16),
            jax.ShapeDtypeStruct((depth, n_groups, 2, LANES), F32),
        ],
        compiler_params=_cparams("parallel", "parallel"),
        name="ssm_prep",
    )(arow, acol, bt, ct)


def _ssm_state_kernel(u_ref, win_ref, sre_ref, sim_ref):
    s = _dot(u_ref[0], win_ref[0, 0])
    sre_ref[...] = s[:, :LANES]
    sim_ref[...] = s[:, LANES:]


def _ssm_out_kernel(u_ref, hfr_ref, hfi_ref, hbr_ref, hbi_ref, m_ref, wout_ref, y_ref):
    fwd = lax.broadcasted_iota(jnp.int32, hfr_ref.shape, 1) < N_STATE
    h = jnp.concatenate([jnp.where(fwd, hfr_ref[...], hbr_ref[...]),
                         jnp.where(fwd, hfi_ref[...], hbi_ref[...])], axis=1).astype(BF16)
    y_ref[0] = (_dot(u_ref[0], m_ref[0, 0]) + _dot(h, wout_ref[0, 0])).astype(y_ref.dtype)


def _ssm_scan_kernel(sre_ref, sim_ref, dre_ref, dim_ref, hfr_ref, hfi_ref, hbr_ref, hbi_ref, *, batch):
    rows, lanes = sre_ref.shape
    per_slab = SUBLANES // batch
    n_slabs = rows // SUBLANES
    shape = (SUBLANES, lanes)
    fwd = jnp.bitwise_and(lax.broadcasted_iota(jnp.int32, shape, 1), N_STATE) == 0
    blk = lax.shift_right_logical(lax.broadcasted_iota(jnp.int32, shape, 0), batch.bit_length() - 1)
    dr = dre_ref[...]
    di = dim_ref[...]

    def spread(s, q):
        part = jnp.where(blk == q, s, 0.0)
        out = part
        for r in range(1, per_slab):
            out = out + pltpu.roll(part, r * batch, 0)
        return out

    def step(i, carry):
        xr, xi = carry
        rf = pl.multiple_of(i * SUBLANES, SUBLANES)
        rb = pl.multiple_of((n_slabs - 1 - i) * SUBLANES, SUBLANES)
        sfr, sfi = sre_ref[pl.ds(rf, SUBLANES), :], sim_ref[pl.ds(rf, SUBLANES), :]
        sbr, sbi = sre_ref[pl.ds(rb, SUBLANES), :], sim_ref[pl.ds(rb, SUBLANES), :]
        hfr = hfi = hbr = hbi = jnp.zeros(shape, F32)
        for q in range(per_slab):
            qb = per_slab - 1 - q
            hfr, hfi = jnp.where(blk == q, xr, hfr), jnp.where(blk == q, xi, hfi)
            hbr, hbi = jnp.where(blk == qb, xr, hbr), jnp.where(blk == qb, xi, hbi)
            in_r = jnp.where(fwd, spread(sfr, q), spread(sbr, qb))
            in_i = jnp.where(fwd, spread(sfi, q), spread(sbi, qb))
            xr, xi = dr * xr - di * xi + in_r, dr * xi + di * xr + in_i
        hfr_ref[pl.ds(rf, SUBLANES), :] = hfr
        hfi_ref[pl.ds(rf, SUBLANES), :] = hfi
        hbr_ref[pl.ds(rb, SUBLANES), :] = hbr
        hbi_ref[pl.ds(rb, SUBLANES), :] = hbi
        return xr, xi

    zeros = jnp.zeros(shape, F32)
    lax.fori_loop(0, n_slabs, step, (zeros, zeros))


def _to_groups_kernel(u_ref, ug_ref, scr_ref, *, chunk):
    tm, d_u = u_ref.shape
    n_ch = tm // chunk
    per_tile = LANES // GROUP_CH
    for j in range(d_u // LANES):
        scr_ref[j] = u_ref[:, j * LANES:(j + 1) * LANES].astype(F32)
    for s in range(chunk):
        dst = (s % per_tile) * GROUP_CH
        for j in range(d_u // LANES):
            slab = scr_ref[j, pl.ds(s, n_ch, stride=chunk), :]
            for gg in range(per_tile):
                shift = (dst - gg * GROUP_CH) % LANES
                moved = (pltpu.roll(slab, shift, 1) if shift else slab).astype(ug_ref.dtype)
                ug_ref[j * per_tile + gg, :, s * GROUP_CH:(s + 1) * GROUP_CH] = moved[:, dst:dst + GROUP_CH]


def _from_groups_kernel(yg_ref, y_ref, scr_ref, *, chunk):
    tm, d_u = y_ref.shape
    n_ch = tm // chunk
    per_tile = LANES // GROUP_CH
    group_of_lane = lax.shift_right_logical(lax.broadcasted_iota(jnp.int32, (n_ch, LANES), 1), 4)
    for s in range(chunk):
        src = (s % per_tile) * GROUP_CH
        base = (s // per_tile) * LANES
        for j in range(d_u // LANES):
            slab = jnp.zeros((n_ch, LANES), F32)
            for gg in range(per_tile):
                v = yg_ref[j * per_tile + gg, :, base:base + LANES].astype(F32)
                shift = (gg * GROUP_CH - src) % LANES
                v = pltpu.roll(v, shift, 1) if shift else v
                slab = jnp.where(group_of_lane == gg, v, slab)
            scr_ref[j, pl.ds(s, n_ch, stride=chunk), :] = slab
    for j in range(d_u // LANES):
        y_ref[:, j * LANES:(j + 1) * LANES] = scr_ref[j].astype(y_ref.dtype)


def _group_relayout(x, to_groups, batch, seq_len, chunk, d_u):
    n_groups = d_u // GROUP_CH
    w = chunk * GROUP_CH
    n_chunks = seq_len // chunk
    tm = min(RELAYOUT_TILE, seq_len)
    tps = seq_len // tm
    tok_spec = pl.BlockSpec((tm, d_u), lambda i: (i, 0))
    grp_spec = pl.BlockSpec((n_groups, tm // chunk, w), lambda i: (0, i % tps, i // tps))
    grp_shape = (n_groups, n_chunks, batch * w)
    scratch = [pltpu.VMEM((d_u // LANES, tm, LANES), F32)]
    if to_groups:
        out = pl.pallas_call(
            functools.partial(_to_groups_kernel, chunk=chunk),
            grid=(batch * tps,), in_specs=[tok_spec], out_specs=grp_spec,
            out_shape=jax.ShapeDtypeStruct(grp_shape, x.dtype), scratch_shapes=scratch,
            compiler_params=_cparams("parallel"), name="ssm_to_groups",
        )(x)
        return out.reshape(n_groups, n_chunks * batch, w)
    return pl.pallas_call(
        functools.partial(_from_groups_kernel, chunk=chunk),
        grid=(batch * tps,), in_specs=[grp_spec], out_specs=tok_spec,
        out_shape=jax.ShapeDtypeStruct((batch * seq_len, d_u), x.dtype), scratch_shapes=scratch,
        compiler_params=_cparams("parallel"), name="ssm_from_groups",
    )(x.reshape(grp_shape))


def _ssm_branch(u, layer, m_all, win_all, wout_all, dec_all, batch, seq_len, chunk):
    n, d_u = u.shape
    n_groups = d_u // GROUP_CH
    w = chunk * GROUP_CH
    n_chunks = seq_len // chunk
    rows = batch * n_chunks
    st = 4 * N_STATE
    assert SUBLANES % batch == 0 and rows % SUBLANES == 0

    ug = _group_relayout(u, True, batch, seq_len, chunk, d_u)
    state = jax.ShapeDtypeStruct((rows, n_groups * LANES), F32)
    col = lambda g: (0, g)
    sre, sim = pl.pallas_call(
        _ssm_state_kernel,
        grid=(n_groups,),
        in_specs=[pl.BlockSpec((1, rows, w), lambda g: (g, 0, 0)),
                  pl.BlockSpec((1, 1, w, st), lambda g: (layer, g, 0, 0))],
        out_specs=[pl.BlockSpec((rows, LANES), col)] * 2,
        out_shape=[state] * 2,
        compiler_params=_cparams("parallel"),
        name="ssm_chunk_state",
    )(ug, win_all)

    dec = dec_all[layer]
    dre = dec[:, 0, :].reshape(1, n_groups * LANES)
    dim = dec[:, 1, :].reshape(1, n_groups * LANES)
    sl = min(SCAN_LANES, n_groups * LANES)
    hs = pl.pallas_call(
        functools.partial(_ssm_scan_kernel, batch=batch),
        grid=(n_groups * LANES // sl,),
        in_specs=[pl.BlockSpec((rows, sl), col)] * 2 + [pl.BlockSpec((1, sl), col)] * 2,
        out_specs=[pl.BlockSpec((rows, sl), col)] * 4,
        out_shape=[state] * 4,
        compiler_params=_cparams("parallel"),
        name="ssm_chunk_scan",
    )(sre, sim, dre, dim)

    y = pl.pallas_call(
        _ssm_out_kernel,
        grid=(n_groups,),
        in_specs=[pl.BlockSpec((1, rows, w), lambda g: (g, 0, 0))]
                 + [pl.BlockSpec((rows, LANES), col)] * 4
                 + [pl.BlockSpec((1, 1, w, w), lambda g: (layer, g, 0, 0)),
                    pl.BlockSpec((1, 1, st, w), lambda g: (layer, g, 0, 0))],
        out_specs=pl.BlockSpec((1, rows, w), lambda g: (g, 0, 0)),
        out_shape=jax.ShapeDtypeStruct((n_groups, rows, w), BF16),
        compiler_params=_cparams("parallel"),
        name="ssm_chunk_output",
    )(ug, *hs, m_all, wout_all)
    return _group_relayout(y, False, batch, seq_len, chunk, d_u)


def _merge_kernel(x_ref, mod_ref, o_ref, y_ref, u_ref, ga_ref, gs_ref, d_ref,
                  wglu_ref, bglu_ref, wattn_ref, wo_ref, out_ref):
    d = x_ref.shape[1]
    y = y_ref[...] + d_ref[...] * u_ref[...].astype(F32)
    z = jax.nn.gelu(y, approximate=True).astype(BF16)
    glu = _dot(z, wglu_ref[...]) + bglu_ref[...]
    y_ssm = glu[:, :d] * jax.nn.sigmoid(glu[:, d:])
    y_attn = _dot(o_ref[...], wattn_ref[...])
    merged = (ga_ref[...].astype(F32) * y_attn + gs_ref[...].astype(F32) * y_ssm).astype(BF16)
    out_ref[...] = x_ref[...] + mod_ref[0][2:3] * _dot(merged, wo_ref[...])


def _merge(x, mod, o, y, u, ga, gs, ssm_d, w_glu, b_glu, w_attn, w_o, layer, seq_len, in_place):
    n, d = x.shape
    tm = min(TOKEN_TILE, seq_len)
    tps = seq_len // tm
    tok = lambda i: (i, 0)
    full = lambda a: pl.BlockSpec(a.shape, lambda i: (0, 0))
    of_layer = lambda a: pl.BlockSpec((None,) + a.shape[1:], lambda i: (layer, 0, 0))
    return pl.pallas_call(
        _merge_kernel,
        grid=(n // tm,),
        in_specs=[
            pl.BlockSpec((tm, d), tok),
            pl.BlockSpec((1, 6, d), lambda i: (i // tps, 0, 0)),
            pl.BlockSpec((tm, o.shape[1]), tok),
            pl.BlockSpec((tm, y.shape[1]), tok),
            pl.BlockSpec((tm, u.shape[1]), tok),
            pl.BlockSpec((tm, d), tok),
            pl.BlockSpec((tm, d), tok),
            full(ssm_d), of_layer(w_glu), full(b_glu), of_layer(w_attn), of_layer(w_o),
        ],
        out_specs=pl.BlockSpec((tm, d), tok),
        out_shape=jax.ShapeDtypeStruct((n, d), F32),
        input_output_aliases={0: 0} if in_place else {},
        compiler_params=_cparams("parallel"),
        name="merge_out_projection",
    )(x, mod, o, y, u, ga, gs, ssm_d, w_glu, b_glu, w_attn, w_o)


def _ffn_kernel(x_ref, mod_ref, g_ref, wg_ref, wu_ref, wd_ref, fg_ref, out_ref, h_ref, acc_ref,
                *, final_norm):
    j = pl.program_id(1)

    @pl.when(j == 0)
    def _():
        mod = mod_ref[0]
        h_ref[...] = (_rms(x_ref[...]) * g_ref[...] * (1.0 + mod[4:5]) + mod[3:4]).astype(BF16)
        acc_ref[...] = jnp.zeros(acc_ref.shape, F32)

    h = h_ref[...]
    gate = _dot(h, wg_ref[...])
    up = _dot(h, wu_ref[...])
    t = (gate * jax.nn.sigmoid(gate) * up).astype(BF16)
    acc_ref[...] += _dot(t, wd_ref[...])

    @pl.when(j == pl.num_programs(1) - 1)
    def _():
        xn = x_ref[...] + mod_ref[0][5:6] * acc_ref[...]
        if final_norm:
            xn = _rms(xn) * fg_ref[...]
        out_ref[...] = xn


def _ffn(x, mod, g, w_in, w_out, final_g, layer, seq_len, final_norm):
    n, d = x.shape
    d_ff = w_out.shape[1]
    tm = min(TOKEN_TILE, seq_len)
    tps = seq_len // tm
    tf = FFN_TILE
    nf = d_ff // tf
    return pl.pallas_call(
        functools.partial(_ffn_kernel, final_norm=final_norm),
        grid=(n // tm, nf),
        in_specs=[
            pl.BlockSpec((tm, d), lambda i, j: (i, 0)),
            pl.BlockSpec((1, 6, d), lambda i, j: (i // tps, 0, 0)),
            pl.BlockSpec((1, d), lambda i, j: (0, 0)),
            pl.BlockSpec((None, d, tf), lambda i, j: (layer, 0, j)),
            pl.BlockSpec((None, d, tf), lambda i, j: (layer, 0, nf + j)),
            pl.BlockSpec((None, tf, d), lambda i, j: (layer, j, 0)),
            pl.BlockSpec((1, d), lambda i, j: (0, 0)),
        ],
        out_specs=pl.BlockSpec((tm, d), lambda i, j: (i, 0)),
        out_shape=jax.ShapeDtypeStruct((n, d), F32),
        scratch_shapes=[pltpu.VMEM((tm, d), BF16), pltpu.VMEM((tm, d), F32)],
        input_output_aliases={0: 0},
        compiler_params=_cparams("parallel", "arbitrary"),
        name="ffn",
    )(x, mod, g, w_in, w_in, w_out, final_g)


def _rope_tables(seq_len):
    inv = 1.0 / (ROPE_THETA ** (jnp.arange(0, HEAD_DIM, 2, dtype=F32) / HEAD_DIM))
    ang = jnp.arange(seq_len, dtype=F32)[:, None] * inv[None, :]
    cos = jnp.cos(ang)
    sin = jnp.sin(ang)
    reps = LANES // HEAD_DIM
    cos_t = jnp.tile(jnp.concatenate([cos, cos], axis=-1), (1, reps))
    sin_t = jnp.tile(jnp.concatenate([-sin, sin], axis=-1), (1, reps))
    return cos_t, sin_t


def _lambda_init(layer):
    return 0.8 - 0.6 * math.exp(-0.3 * layer)


def _trunk(x, mod_all, p, ssm_ops):
    batch, seq_len, d = x.shape
    depth = p["w_in"].shape[0]
    d_v = N_HEADS * V_DIM
    d_u = p["ssm_d"].shape[1]
    cos, sin = _rope_tables(seq_len)
    x = x.reshape(batch * seq_len, d)
    row = lambda a: a.reshape(1, -1)
    for i in range(depth):
        mod = mod_all[i].reshape(batch, 6, d)
        qt, k, vt, u, ga, gs = _in_projection(x, mod, row(p["norm1_g"][i]), cos, sin, p["w_in"], i,
                                              batch, seq_len, d_u)
        lam_params = jnp.stack([p["lam_q1"][i], p["lam_k1"][i], p["lam_q2"][i], p["lam_k2"][i]])
        o = _diff_attention(qt, k, vt, lam_params, row(p["subln_g"][i]), _lambda_init(i))
        y = _ssm_branch(u, i, *ssm_ops, batch, seq_len, SSM_CHUNK)
        x = _merge(x, mod, o.reshape(batch * seq_len, d_v), y, u, ga, gs, row(p["ssm_d"][i]),
                   p["w_glu"], row(p["b_glu"][i]), p["w_attn_br"], p["w_o"], i, seq_len,
                   in_place=(i > 0))
        x = _ffn(x, mod, row(p["norm2_g"][i]), p["w_ffn_in"], p["w_ffn_out"],
                 row(p["final_g"]), i, seq_len, final_norm=(i == depth - 1))
    return x.reshape(batch, seq_len, d)


def kernel(x_prompt, x_sample, c_prompt, c_sample, w_mod, b_mod, norm1_g, w_in, lam_q1, lam_k1, lam_q2, lam_k2, subln_g, w_attn_br, ssm_a_re, ssm_a_im, ssm_log_dt, ssm_b_re, ssm_b_im, ssm_c_re, ssm_c_im, ssm_d, w_glu, b_glu, w_o, norm2_g, w_ffn_in, w_ffn_out, final_g):
    bp, bs = c_prompt.shape[0], c_sample.shape[0]
    pad = -(bp + bs) % 8
    c_all = jnp.concatenate([c_prompt, c_sample, jnp.zeros((pad, c_prompt.shape[1]), F32)], axis=0)
    mod_all = _modulation(c_all, w_mod, b_mod)
    ssm_ops = _ssm_prep(ssm_a_re, ssm_a_im, ssm_log_dt, ssm_b_re, ssm_b_im, ssm_c_re, ssm_c_im, SSM_CHUNK)
    p = dict(
        norm1_g=norm1_g, w_in=w_in.astype(BF16), lam_q1=lam_q1, lam_k1=lam_k1, lam_q2=lam_q2, lam_k2=lam_k2,
        subln_g=subln_g, w_attn_br=w_attn_br.astype(BF16), ssm_d=ssm_d, w_glu=w_glu.astype(BF16),
        b_glu=b_glu, w_o=w_o.astype(BF16), norm2_g=norm2_g, w_ffn_in=w_ffn_in.astype(BF16),
        w_ffn_out=w_ffn_out.astype(BF16), final_g=final_g)
    y_prompt = _trunk(x_prompt, mod_all[:, :bp], p, ssm_ops)
    y_sample = _trunk(x_sample, mod_all[:, bp:bp + bs], p, ssm_ops)
    return (y_prompt, y_sample)
```

```python
import functools
import math

import jax
import jax.numpy as jnp
from jax import lax
from jax.experimental import pallas as pl
from jax.experimental.pallas import tpu as pltpu

F32 = jnp.float32
BF16 = jnp.bfloat16

N_HEADS = 4
HEAD_DIM = 64
V_DIM = 2 * HEAD_DIM
GROUP_CH = 16
N_STATE = 64
ROPE_THETA = 10000.0
EPS = 1e-6
Q_SCALE = HEAD_DIM ** -0.5 * math.log2(math.e)
V_PAD = 16

VMEM_LIMIT_BYTES = 56 * 1024 * 1024
LANES = 128

TOKEN_TILE = 512
ATTN_TQ = 256
ATTN_TK = 256
SSM_CHUNK = 64
FFN_TILE = 1408
SCAN_GROUPS = 4
RELAYOUT_TILE = 1024


def _cparams(*sem):
    return pltpu.CompilerParams(dimension_semantics=sem, vmem_limit_bytes=VMEM_LIMIT_BYTES)


def _dot(a, b):
    return jnp.dot(a, b, preferred_element_type=F32)


def _split3(x):
    hi = x.astype(BF16)
    r1 = x - hi.astype(F32)
    mid = r1.astype(BF16)
    lo = (r1 - mid.astype(F32)).astype(BF16)
    return hi, mid, lo


def _dot_f32(a, b):
    a_hi = a.astype(BF16)
    a_lo = (a - a_hi.astype(F32)).astype(BF16)
    b_hi = b.astype(BF16)
    b_lo = (b - b_hi.astype(F32)).astype(BF16)
    return _dot(a_hi, b_hi) + (_dot(a_hi, b_lo) + _dot(a_lo, b_hi))


def _select_cols(x, sel):
    hi, mid, lo = _split3(x)
    return _dot(hi, sel) + (_dot(mid, sel) + _dot(lo, sel))


def _select_rows(sel, x):
    hi, mid, lo = _split3(x)
    return _dot(sel, hi) + (_dot(sel, mid) + _dot(sel, lo))


def _onehot(cond):
    return jnp.where(cond, 1.0, 0.0).astype(BF16)


def _rms(x):
    return x * lax.rsqrt(jnp.mean(x * x, axis=-1, keepdims=True) + EPS)


def _mod_kernel(c_ref, w_ref, b_ref, o_ref):
    c = c_ref[...]
    s = c * jax.nn.sigmoid(c)
    o_ref[0] = _dot_f32(s, w_ref[0]) + b_ref[0]


def _modulation(c_all, w_mod, b_mod):
    depth, d, n6 = w_mod.shape
    bp = c_all.shape[0]
    tn = 1536
    return pl.pallas_call(
        _mod_kernel,
        grid=(depth, n6 // tn),
        in_specs=[
            pl.BlockSpec((bp, d), lambda l, j: (0, 0)),
            pl.BlockSpec((1, d, tn), lambda l, j: (l, 0, j)),
            pl.BlockSpec((1, 1, tn), lambda l, j: (l, 0, j)),
        ],
        out_specs=pl.BlockSpec((1, bp, tn), lambda l, j: (l, 0, j)),
        out_shape=jax.ShapeDtypeStruct((depth, bp, n6), F32),
        compiler_params=_cparams("parallel", "parallel"),
        name="modulation",
    )(c_all, w_mod, b_mod.reshape(depth, 1, n6))


def _inproj_kernel(x_ref, mod_ref, g_ref, cos_ref, sin_ref, w_ref,
                   qt_ref, k_ref, vt_ref, u_ref, ga_ref, gs_ref):
    d = x_ref.shape[1]
    n_heads = k_ref.shape[1]
    d_qk = n_heads * V_DIM
    mod = mod_ref[0]
    h = (_rms(x_ref[...]) * g_ref[...] * (1.0 + mod[1:2]) + mod[0:1]).astype(BF16)

    qk = _dot(h, w_ref[:, 0:2 * d_qk])
    reps = 2 * d_qk // LANES
    cos = jnp.tile(cos_ref[...], (1, reps))
    sin = jnp.tile(sin_ref[...], (1, reps))
    lane = lax.broadcasted_iota(jnp.int32, qk.shape, 1)
    first_half = jnp.bitwise_and(lane, HEAD_DIM // 2) == 0
    half = HEAD_DIM // 2
    rot = jnp.where(first_half,
                    pltpu.roll(qk, 2 * d_qk - half, 1),
                    pltpu.roll(qk, half, 1))
    qk = qk * cos + rot * sin
    tq = qt_ref.shape[4]
    for hd in range(n_heads):
        qh = qk[:, hd * V_DIM:(hd + 1) * V_DIM] * Q_SCALE
        for c in range(qt_ref.shape[2]):
            qt_ref[0, hd, c] = qh[c * tq:(c + 1) * tq].T.astype(BF16)
        k_ref[0, hd] = qk[:, d_qk + hd * V_DIM:d_qk + (hd + 1) * V_DIM].astype(BF16)

    off = 2 * d_qk
    d_v = n_heads * V_DIM
    d_u = u_ref.shape[1]
    vu = _dot(h, w_ref[:, off:off + d_v + d_u])
    tk = vt_ref.shape[4]
    tail_row = lax.broadcasted_iota(jnp.int32, (V_PAD, tk), 0)
    ones_row = jnp.where(tail_row == 0, 1.0, 0.0).astype(BF16)
    for hd in range(n_heads):
        for c in range(vt_ref.shape[2]):
            vt_ref[0, hd, c, :V_DIM, :] = vu[c * tk:(c + 1) * tk, hd * V_DIM:(hd + 1) * V_DIM].T.astype(BF16)
            vt_ref[0, hd, c, V_DIM:, :] = ones_row
    u_ref[...] = vu[:, d_v:].astype(BF16)
    off += d_v + d_u
    ga_ref[...] = jax.nn.sigmoid(_dot(h, w_ref[:, off:off + d])).astype(BF16)
    gs_ref[...] = jax.nn.sigmoid(_dot(h, w_ref[:, off + d:off + 2 * d])).astype(BF16)


def _in_projection(x, mod, g, cos, sin, w_all, layer, batch, seq_len, d_u):
    n, d = x.shape
    tm = min(TOKEN_TILE, seq_len)
    tps = seq_len // tm
    tk = min(ATTN_TK, tm)
    tq = min(ATTN_TQ, tm)
    tok = lambda i: (i, 0)
    bf = lambda *shape: jax.ShapeDtypeStruct(shape, BF16)
    return pl.pallas_call(
        _inproj_kernel,
        grid=(n // tm,),
        in_specs=[
            pl.BlockSpec((tm, d), tok),
            pl.BlockSpec((1, 6, d), lambda i: (i // tps, 0, 0)),
            pl.BlockSpec((1, d), lambda i: (0, 0)),
            pl.BlockSpec((tm, LANES), lambda i: (i % tps, 0)),
            pl.BlockSpec((tm, LANES), lambda i: (i % tps, 0)),
            pl.BlockSpec((None,) + w_all.shape[1:], lambda i: (layer, 0, 0)),
        ],
        out_specs=[
            pl.BlockSpec((1, N_HEADS, tm // tq, V_DIM, tq), lambda i: (i // tps, 0, i % tps, 0, 0)),
            pl.BlockSpec((1, N_HEADS, tm, V_DIM), lambda i: (i // tps, 0, i % tps, 0)),
            pl.BlockSpec((1, N_HEADS, tm // tk, V_DIM + V_PAD, tk), lambda i: (i // tps, 0, i % tps, 0, 0)),
            pl.BlockSpec((tm, d_u), tok),
            pl.BlockSpec((tm, d), tok),
            pl.BlockSpec((tm, d), tok),
        ],
        out_shape=[
            bf(batch, N_HEADS, seq_len // tq, V_DIM, tq),
            bf(batch, N_HEADS, seq_len, V_DIM),
            bf(batch, N_HEADS, seq_len // tk, V_DIM + V_PAD, tk),
            bf(n, d_u), bf(n, d), bf(n, d),
        ],
        compiler_params=_cparams("parallel"),
        name="in_projection",
    )(x, mod, g, cos, sin, w_all)


def _attn_kernel(qt_ref, k_ref, vt_ref, lam_ref, g_ref, o_ref, *scratch, lam_init):
    n_chunks, _, tk = vt_ref.shape[2:]
    n_blocks, _, tq = qt_ref.shape[2:]
    per_set = len(scratch) // 2
    sets = (scratch[:per_set], scratch[per_set:])

    lp = lam_ref[...]
    lam = (jnp.exp(jnp.sum(lp[0:1] * lp[1:2], axis=-1, keepdims=True))
           - jnp.exp(jnp.sum(lp[2:3] * lp[3:4], axis=-1, keepdims=True)) + lam_init)
    gain = g_ref[...] * (1.0 - lam_init)

    def q_operand(j):
        qt = qt_ref[0, 0, j].astype(F32)
        row = lax.broadcasted_iota(jnp.int32, qt.shape, 0)
        return jnp.concatenate([jnp.where(row < HEAD_DIM, qt, 0.0), jnp.where(row >= HEAD_DIM, qt, 0.0)],
                               axis=1).astype(BF16)

    def scores(qtb, c):
        return _dot(k_ref[0, 0, c * tk:(c + 1) * tk, :], qtb)

    def softmax(st, slot, m_old):
        s = st[slot][...]
        m_new = jnp.maximum(m_old, jnp.max(s, axis=0, keepdims=True))
        st[2 + slot][...] = jnp.exp2(s - m_new).astype(BF16)
        return m_new, jnp.exp2(m_old - m_new)

    def attend(st, c, slot, alpha):
        st[4][...] = alpha * st[4][...] + _dot(vt_ref[0, 0, c], st[2 + slot][...])

    def head(st, j):
        qtb = q_operand(j)
        st[4][...] = jnp.zeros(st[4].shape, F32)
        st[0][...] = scores(qtb, 0)
        st[1][...] = scores(qtb, 1)
        return (qtb,) + softmax(st, 0, jnp.full((1, 2 * tq), -jnp.inf, F32))

    def steady(st, carry):
        qtb, m, alpha = carry
        for c in range(1, n_chunks - 1):
            slot = c % 2
            st[1 - slot][...] = scores(qtb, c + 1)
            m, alpha_new = softmax(st, slot, m)
            attend(st, c - 1, 1 - slot, alpha)
            alpha = alpha_new
        return m, alpha

    def tail(st, j, m, alpha):
        last = (n_chunks - 1) % 2
        _, alpha_last = softmax(st, last, m)
        attend(st, n_chunks - 2, 1 - last, alpha)
        attend(st, n_chunks - 1, last, alpha_last)
        acc = st[4][...]
        acc = acc[:V_DIM] / acc[V_DIM:V_DIM + 1]
        o = (acc[:, :tq] - lam * acc[:, tq:]).T
        o_ref[0, pl.ds(pl.multiple_of(j * tq, tq), tq), :] = (_rms(o) * gain).astype(o_ref.dtype)

    def body(jj, carry):
        j0 = 2 * jj
        m0, a0 = steady(sets[0], carry)
        carry1 = head(sets[1], j0 + 1)
        tail(sets[0], j0, m0, a0)
        m1, a1 = steady(sets[1], carry1)
        nxt = head(sets[0], jnp.minimum(j0 + 2, n_blocks - 1))
        tail(sets[1], j0 + 1, m1, a1)
        return nxt

    lax.fori_loop(0, n_blocks // 2, body, head(sets[0], 0))


def _diff_attention(qt, k, vt, lam_params, subln_g, lam_init):
    b, n_heads, l, _ = k.shape
    n_blocks, _, tq = qt.shape[2:]
    n_chunks, v_rows, tk = vt.shape[2:]
    assert n_blocks % 2 == 0 and n_chunks >= 2
    one_set = ([pltpu.VMEM((tk, 2 * tq), F32)] * 2 + [pltpu.VMEM((tk, 2 * tq), BF16)] * 2
               + [pltpu.VMEM((v_rows, 2 * tq), F32)])
    whole = lambda a: pl.BlockSpec((1, 1) + a.shape[2:], lambda bi, h: (bi, h) + (0,) * (a.ndim - 2))
    return pl.pallas_call(
        functools.partial(_attn_kernel, lam_init=lam_init),
        grid=(b, n_heads),
        in_specs=[
            whole(qt), whole(k), whole(vt),
            pl.BlockSpec(lam_params.shape, lambda bi, h: (0, 0)),
            pl.BlockSpec((1, V_DIM), lambda bi, h: (0, 0)),
        ],
        out_specs=pl.BlockSpec((1, l, V_DIM), lambda bi, h: (bi, 0, h)),
        out_shape=jax.ShapeDtypeStruct((b, l, n_heads * V_DIM), BF16),
        scratch_shapes=one_set * 2,
        compiler_params=_cparams("parallel", "parallel"),
        name="diff_attention",
    )(qt, k, vt, lam_params, subln_g)


def _ssm_prep_kernel(arow_ref, acol_ref, bt_ref, ct_ref, m_ref, win_ref, wout_ref, dec_ref, *, chunk):
    t_len = chunk
    w = t_len * GROUP_CH
    w2 = 2 * w
    arow = arow_ref[0, 0]
    acol = acol_ref[0, 0]
    fwd_lanes = lax.broadcasted_iota(jnp.int32, (1, LANES), 1) < N_STATE

    ar = arow[0:1]
    ai = arow[1:2]
    dt = jnp.exp(arow[2:3])
    zr = dt * ar
    zi = dt * ai
    mag = jnp.exp(zr)
    nr = mag * jnp.cos(zi) - 1.0
    ni = mag * jnp.sin(zi)
    den = ar * ar + ai * ai
    fr = (nr * ar + ni * ai) / den
    fi = (ni * ar - nr * ai) / den
    br = bt_ref[0, 0, 0]
    bi = bt_ref[0, 0, 1]
    bbr = fr * br - fi * bi
    bbi = fr * bi + fi * br

    s_of_row = lax.shift_right_logical(lax.broadcasted_iota(jnp.int32, (w, t_len), 0), 4)
    j_of_col = lax.broadcasted_iota(jnp.int32, (w, t_len), 1)
    rep_fwd = _onehot(j_of_col == (t_len - 1 - s_of_row))
    rep_bwd = _onehot(j_of_col == s_of_row)
    n_rows = lax.broadcasted_iota(jnp.int32, (t_len, 1), 0).astype(F32)
    pm = jnp.exp(n_rows * zr)

    def expand(p):
        return jnp.where(fwd_lanes, _select_rows(rep_fwd, p), _select_rows(rep_bwd, p))

    e_re = expand(pm * jnp.cos(n_rows * zi))
    e_im = expand(pm * jnp.sin(n_rows * zi))
    b_re = jnp.tile(bbr, (t_len, 1))
    b_im = jnp.tile(bbi, (t_len, 1))
    win_ref[0, 0, :, :LANES] = (e_re * b_re - e_im * b_im).astype(win_ref.dtype)
    win_ref[0, 0, :, LANES:] = (e_re * b_im + e_im * b_re).astype(win_ref.dtype)
    dm = jnp.exp(t_len * zr)
    dec_ref[0, 0, 0:1, :] = dm * jnp.cos(t_len * zi)
    dec_ref[0, 0, 1:2, :] = dm * jnp.sin(t_len * zi)

    def lane_maps(width):
        lane = lax.broadcasted_iota(jnp.int32, (LANES, width), 1)
        return (lax.shift_right_logical(lane, 4), jnp.bitwise_and(lane, GROUP_CH - 1),
                lax.broadcasted_iota(jnp.int32, (LANES, width), 0))

    lag_idx, ch_idx, jrow = lane_maps(w2)
    lag_idx_w, _, jrow_w = lane_maps(w)
    tile_ch = _onehot(jrow == ch_idx)
    n_lanes = jnp.minimum(lax.broadcasted_iota(jnp.int32, (1, LANES), 1), t_len).astype(F32)

    lag_tables = []
    out_tables = []
    for d in range(2):
        dtc = jnp.exp(acol[:, 4 + d:5 + d])
        zrc = dtc * acol[:, d:d + 1]
        zic = dtc * acol[:, 2 + d:3 + d]
        ptm = jnp.exp(zrc * n_lanes)
        pt_re = ptm * jnp.cos(zic * n_lanes)
        pt_im = ptm * jnp.sin(zic * n_lanes)
        c_re = _select_cols(ct_ref[0, 0, d], tile_ch)
        c_im = _select_cols(ct_ref[0, 0, 2 + d], tile_ch)

        def table(rep):
            width = rep.shape[1]
            p_re = _select_cols(pt_re, rep)
            p_im = _select_cols(pt_im, rep)
            cr = c_re[:, :width]
            ci = c_im[:, :width]
            return p_re * cr - p_im * ci, p_re * ci + p_im * cr

        if d == 0:
            power = lag_idx - (t_len - 1)
        else:
            power = (t_len - 1) - lag_idx
        power = jnp.where(power >= 0, power, -1)
        f_re, f_im = table(_onehot(jrow == power))
        lag_tables.append((f_re, f_im))
        if d == 0:
            out_tables.append((f_re[:, w:], f_im[:, w:]))
        else:
            out_tables.append(table(_onehot(jrow_w == (t_len - lag_idx_w))))

    for i, o in enumerate((out_tables[0][0], out_tables[1][0], -out_tables[0][1], -out_tables[1][1])):
        wout_ref[0, 0, i * N_STATE:(i + 1) * N_STATE, :] = o.astype(wout_ref.dtype)

    lhs = jnp.concatenate([bbr, -bbi], axis=1)
    rhs = jnp.concatenate([lag_tables[0][0], lag_tables[1][0], lag_tables[0][1], lag_tables[1][1]], axis=0)
    strip = _dot_f32(lhs, rhs)

    per_tile = LANES // GROUP_CH
    for r in range(per_tile):
        rolled = strip if r == 0 else pltpu.roll(strip, w2 - r * GROUP_CH, 1)
        for s in range(t_len):
            if (t_len - 1 - s) % per_tile == r:
                q = (t_len - 1 - s) // per_tile
                m_ref[0, 0, s * GROUP_CH:(s + 1) * GROUP_CH, :] = (
                    rolled[:, q * LANES:q * LANES + w].astype(m_ref.dtype))


def _ssm_prep(a_re, a_im, log_dt, b_re, b_im, c_re, c_im, chunk):
    depth, _, n_groups, n_state = a_re.shape
    w = chunk * GROUP_CH
    gd = lambda x: jnp.swapaxes(x, 1, 2)
    ldt = jnp.broadcast_to(gd(log_dt)[..., None], (depth, n_groups, 2, n_state))
    packed = lambda x: x.reshape(depth, n_groups, 1, 2 * n_state)
    arow = jnp.concatenate([packed(gd(a_re)), packed(gd(a_im)), packed(ldt),
                            jnp.zeros((depth, n_groups, 5, 2 * n_state), F32)], axis=2)
    acol = jnp.swapaxes(jnp.concatenate([gd(a_re), gd(a_im), ldt, jnp.zeros_like(ldt)], axis=2), 2, 3)
    bt = jnp.stack([gd(b_re), gd(b_im)], axis=2)
    bt = bt.transpose(0, 1, 2, 5, 3, 4).reshape(depth, n_groups, 2, GROUP_CH, 2 * n_state)
    ct = jnp.swapaxes(jnp.concatenate([gd(c_re), gd(c_im)], axis=2), 3, 4)
    ct = jnp.pad(ct, ((0, 0),) * 4 + ((0, LANES - GROUP_CH),))
    blk = lambda *tail: pl.BlockSpec((1, 1) + tail, lambda l, g: (l, g) + (0,) * len(tail))
    return pl.pallas_call(
        functools.partial(_ssm_prep_kernel, chunk=chunk),
        grid=(depth, n_groups),
        in_specs=[blk(8, LANES), blk(n_state, 8), blk(2, GROUP_CH, LANES), blk(4, n_state, LANES)],
        out_specs=[blk(w, w), blk(w, 4 * n_state), blk(4 * n_state, w), blk(2, LANES)],
        out_shape=[
            jax.ShapeDtypeStruct((depth, n_groups, w, w), BF16),
            jax.ShapeDtypeStruct((depth, n_groups, w, 4 * n_state), BF16),
            jax.ShapeDtypeStruct((depth, n_groups, 4 * n_state, w), BF16),
            jax.ShapeDtypeStruct((depth, n_groups, 2, LANES), F32),
        ],
        compiler_params=_cparams("parallel", "parallel"),
        name="ssm_prep",
    )(arow, acol, bt, ct)


def _ssm_state_kernel(u_ref, win_ref, sre_ref, sim_ref):
    s = _dot(u_ref[0], win_ref[0, 0])
    sre_ref[0] = s[:, :LANES]
    sim_ref[0] = s[:, LANES:]


def _ssm_out_kernel(u_ref, hfr_ref, hfi_ref, hbr_ref, hbi_ref, m_ref, wout_ref, y_ref):
    fwd = lax.broadcasted_iota(jnp.int32, hfr_ref.shape[1:], 1) < N_STATE
    h = jnp.concatenate([jnp.where(fwd, hfr_ref[0], hbr_ref[0]),
                         jnp.where(fwd, hfi_ref[0], hbi_ref[0])], axis=1).astype(BF16)
    y_ref[0] = (_dot(u_ref[0], m_ref[0, 0]) + _dot(h, wout_ref[0, 0])).astype(y_ref.dtype)


def _ssm_scan_kernel(sre_ref, sim_ref, dre_ref, dim_ref, hfr_ref, hfi_ref, hbr_ref, hbi_ref,
                     *, batch, n_chunks):
    n_blk, _, lanes = sre_ref.shape
    fwd = jnp.bitwise_and(lax.broadcasted_iota(jnp.int32, (batch, lanes), 1), N_STATE) == 0

    def step(i, carry):
        rows_f = pl.ds(i, batch, stride=n_chunks)
        rows_b = pl.ds(n_chunks - 1 - i, batch, stride=n_chunks)
        out = []
        for q in range(n_blk):
            xr, xi = carry[2 * q], carry[2 * q + 1]
            hfr_ref[q, rows_f, :] = xr
            hfi_ref[q, rows_f, :] = xi
            hbr_ref[q, rows_b, :] = xr
            hbi_ref[q, rows_b, :] = xi
            sr = jnp.where(fwd, sre_ref[q, rows_f, :], sre_ref[q, rows_b, :])
            si = jnp.where(fwd, sim_ref[q, rows_f, :], sim_ref[q, rows_b, :])
            dr = dre_ref[q]
            di = dim_ref[q]
            out += [dr * xr - di * xi + sr, dr * xi + di * xr + si]
        return tuple(out)

    zeros = jnp.zeros((batch, lanes), F32)
    lax.fori_loop(0, n_chunks, step, (zeros,) * (2 * n_blk))


def _to_groups_kernel(u_ref, ug_ref, scr_ref, *, chunk):
    tm, d_u = u_ref.shape
    n_ch = tm // chunk
    per_tile = LANES // GROUP_CH
    for j in range(d_u // LANES):
        scr_ref[j] = u_ref[:, j * LANES:(j + 1) * LANES].astype(F32)
    for s in range(chunk):
        dst = (s % per_tile) * GROUP_CH
        for j in range(d_u // LANES):
            slab = scr_ref[j, pl.ds(s, n_ch, stride=chunk), :]
            for gg in range(per_tile):
                shift = (dst - gg * GROUP_CH) % LANES
                moved = (pltpu.roll(slab, shift, 1) if shift else slab).astype(ug_ref.dtype)
                ug_ref[j * per_tile + gg, :, s * GROUP_CH:(s + 1) * GROUP_CH] = moved[:, dst:dst + GROUP_CH]


def _from_groups_kernel(yg_ref, y_ref, scr_ref, *, chunk):
    tm, d_u = y_ref.shape
    n_ch = tm // chunk
    per_tile = LANES // GROUP_CH
    group_of_lane = lax.shift_right_logical(lax.broadcasted_iota(jnp.int32, (n_ch, LANES), 1), 4)
    for s in range(chunk):
        src = (s % per_tile) * GROUP_CH
        base = (s // per_tile) * LANES
        for j in range(d_u // LANES):
            slab = jnp.zeros((n_ch, LANES), F32)
            for gg in range(per_tile):
                v = yg_ref[j * per_tile + gg, :, base:base + LANES].astype(F32)
                shift = (gg * GROUP_CH - src) % LANES
                v = pltpu.roll(v, shift, 1) if shift else v
                slab = jnp.where(group_of_lane == gg, v, slab)
            scr_ref[j, pl.ds(s, n_ch, stride=chunk), :] = slab
    for j in range(d_u // LANES):
        y_ref[:, j * LANES:(j + 1) * LANES] = scr_ref[j].astype(y_ref.dtype)


def _group_relayout(x, to_groups, batch, seq_len, chunk, d_u):
    n_groups = d_u // GROUP_CH
    w = chunk * GROUP_CH
    n_chunks = seq_len // chunk
    tm = min(RELAYOUT_TILE, seq_len)
    tps = seq_len // tm
    tok_spec = pl.BlockSpec((tm, d_u), lambda i: (i, 0))
    grp_spec = pl.BlockSpec((n_groups, tm // chunk, w), lambda i: (0, i, 0))
    grp_shape = (n_groups, batch * n_chunks, w)
    scratch = [pltpu.VMEM((d_u // LANES, tm, LANES), F32)]
    if to_groups:
        return pl.pallas_call(
            functools.partial(_to_groups_kernel, chunk=chunk),
            grid=(batch * tps,), in_specs=[tok_spec], out_specs=grp_spec,
            out_shape=jax.ShapeDtypeStruct(grp_shape, x.dtype), scratch_shapes=scratch,
            compiler_params=_cparams("parallel"), name="ssm_to_groups",
        )(x)
    return pl.pallas_call(
        functools.partial(_from_groups_kernel, chunk=chunk),
        grid=(batch * tps,), in_specs=[grp_spec], out_specs=tok_spec,
        out_shape=jax.ShapeDtypeStruct((batch * seq_len, d_u), x.dtype), scratch_shapes=scratch,
        compiler_params=_cparams("parallel"), name="ssm_from_groups",
    )(x)


def _ssm_branch(u, layer, m_all, win_all, wout_all, dec_all, batch, seq_len, chunk):
    n, d_u = u.shape
    n_groups = d_u // GROUP_CH
    w = chunk * GROUP_CH
    n_chunks = seq_len // chunk
    rows = batch * n_chunks
    st = 4 * N_STATE

    ug = _group_relayout(u, True, batch, seq_len, chunk, d_u)
    state = jax.ShapeDtypeStruct((n_groups, rows, LANES), F32)
    one = lambda g: (g, 0, 0)
    sre, sim = pl.pallas_call(
        _ssm_state_kernel,
        grid=(n_groups,),
        in_specs=[pl.BlockSpec((1, rows, w), one),
                  pl.BlockSpec((1, 1, w, st), lambda g: (layer, g, 0, 0))],
        out_specs=[pl.BlockSpec((1, rows, LANES), one)] * 2,
        out_shape=[state] * 2,
        compiler_params=_cparams("parallel"),
        name="ssm_chunk_state",
    )(ug, win_all)

    dec = dec_all[layer]
    gb = min(SCAN_GROUPS, n_groups)
    hs = pl.pallas_call(
        functools.partial(_ssm_scan_kernel, batch=batch, n_chunks=n_chunks),
        grid=(n_groups // gb,),
        in_specs=[pl.BlockSpec((gb, rows, LANES), one)] * 2 + [pl.BlockSpec((gb, 1, LANES), one)] * 2,
        out_specs=[pl.BlockSpec((gb, rows, LANES), one)] * 4,
        out_shape=[state] * 4,
        compiler_params=_cparams("parallel"),
        name="ssm_chunk_scan",
    )(sre, sim, dec[:, 0:1, :], dec[:, 1:2, :])

    y = pl.pallas_call(
        _ssm_out_kernel,
        grid=(n_groups,),
        in_specs=[pl.BlockSpec((1, rows, w), one)]
                 + [pl.BlockSpec((1, rows, LANES), one)] * 4
                 + [pl.BlockSpec((1, 1, w, w), lambda g: (layer, g, 0, 0)),
                    pl.BlockSpec((1, 1, st, w), lambda g: (layer, g, 0, 0))],
        out_specs=pl.BlockSpec((1, rows, w), one),
        out_shape=jax.ShapeDtypeStruct((n_groups, rows, w), BF16),
        compiler_params=_cparams("parallel"),
        name="ssm_chunk_output",
    )(ug, *hs, m_all, wout_all)
    return _group_relayout(y, False, batch, seq_len, chunk, d_u)


def _merge_kernel(x_ref, mod_ref, o_ref, y_ref, u_ref, ga_ref, gs_ref, d_ref,
                  wglu_ref, bglu_ref, wattn_ref, wo_ref, out_ref):
    d = x_ref.shape[1]
    y = y_ref[...] + d_ref[...] * u_ref[...].astype(F32)
    z = jax.nn.gelu(y, approximate=True).astype(BF16)
    glu = _dot(z, wglu_ref[...]) + bglu_ref[...]
    y_ssm = glu[:, :d] * jax.nn.sigmoid(glu[:, d:])
    y_attn = _dot(o_ref[...], wattn_ref[...])
    merged = (ga_ref[...].astype(F32) * y_attn + gs_ref[...].astype(F32) * y_ssm).astype(BF16)
    out_ref[...] = x_ref[...] + mod_ref[0][2:3] * _dot(merged, wo_ref[...])


def _merge(x, mod, o, y, u, ga, gs, ssm_d, w_glu, b_glu, w_attn, w_o, layer, seq_len, in_place):
    n, d = x.shape
    tm = min(TOKEN_TILE, seq_len)
    tps = seq_len // tm
    tok = lambda i: (i, 0)
    full = lambda a: pl.BlockSpec(a.shape, lambda i: (0, 0))
    of_layer = lambda a: pl.BlockSpec((None,) + a.shape[1:], lambda i: (layer, 0, 0))
    return pl.pallas_call(
        _merge_kernel,
        grid=(n // tm,),
        in_specs=[
            pl.BlockSpec((tm, d), tok),
            pl.BlockSpec((1, 6, d), lambda i: (i // tps, 0, 0)),
            pl.BlockSpec((tm, o.shape[1]), tok),
            pl.BlockSpec((tm, y.shape[1]), tok),
            pl.BlockSpec((tm, u.shape[1]), tok),
            pl.BlockSpec((tm, d), tok),
            pl.BlockSpec((tm, d), tok),
            full(ssm_d), of_layer(w_glu), full(b_glu), of_layer(w_attn), of_layer(w_o),
        ],
        out_specs=pl.BlockSpec((tm, d), tok),
        out_shape=jax.ShapeDtypeStruct((n, d), F32),
        input_output_aliases={0: 0} if in_place else {},
        compiler_params=_cparams("parallel"),
        name="merge_out_projection",
    )(x, mod, o, y, u, ga, gs, ssm_d, w_glu, b_glu, w_attn, w_o)


def _ffn_kernel(x_ref, mod_ref, g_ref, wg_ref, wu_ref, wd_ref, fg_ref, out_ref, h_ref, acc_ref,
                *, final_norm):
    j = pl.program_id(1)

    @pl.when(j == 0)
    def _():
        mod = mod_ref[0]
        h_ref[...] = (_rms(x_ref[...]) * g_ref[...] * (1.0 + mod[4:5]) + mod[3:4]).astype(BF16)
        acc_ref[...] = jnp.zeros(acc_ref.shape, F32)

    h = h_ref[...]
    gate = _dot(h, wg_ref[...])
    up = _dot(h, wu_ref[...])
    t = (gate * jax.nn.sigmoid(gate) * up).astype(BF16)
    acc_ref[...] += _dot(t, wd_ref[...])

    @pl.when(j == pl.num_programs(1) - 1)
    def _():
        xn = x_ref[...] + mod_ref[0][5:6] * acc_ref[...]
        if final_norm:
            xn = _rms(xn) * fg_ref[...]
        out_ref[...] = xn


def _ffn(x, mod, g, w_in, w_out, final_g, layer, seq_len, final_norm):
    n, d = x.shape
    d_ff = w_out.shape[1]
    tm = min(TOKEN_TILE, seq_len)
    tps = seq_len // tm
    tf = FFN_TILE
    nf = d_ff // tf
    return pl.pallas_call(
        functools.partial(_ffn_kernel, final_norm=final_norm),
        grid=(n // tm, nf),
        in_specs=[
            pl.BlockSpec((tm, d), lambda i, j: (i, 0)),
            pl.BlockSpec((1, 6, d), lambda i, j: (i // tps, 0, 0)),
            pl.BlockSpec((1, d), lambda i, j: (0, 0)),
            pl.BlockSpec((None, d, tf), lambda i, j: (layer, 0, j)),
            pl.BlockSpec((None, d, tf), lambda i, j: (layer, 0, nf + j)),
            pl.BlockSpec((None, tf, d), lambda i, j: (layer, j, 0)),
            pl.BlockSpec((1, d), lambda i, j: (0, 0)),
        ],
        out_specs=pl.BlockSpec((tm, d), lambda i, j: (i, 0)),
        out_shape=jax.ShapeDtypeStruct((n, d), F32),
        scratch_shapes=[pltpu.VMEM((tm, d), BF16), pltpu.VMEM((tm, d), F32)],
        input_output_aliases={0: 0},
        compiler_params=_cparams("parallel", "arbitrary"),
        name="ffn",
    )(x, mod, g, w_in, w_in, w_out, final_g)


def _rope_tables(seq_len):
    inv = 1.0 / (ROPE_THETA ** (jnp.arange(0, HEAD_DIM, 2, dtype=F32) / HEAD_DIM))
    ang = jnp.arange(seq_len, dtype=F32)[:, None] * inv[None, :]
    cos = jnp.cos(ang)
    sin = jnp.sin(ang)
    reps = LANES // HEAD_DIM
    cos_t = jnp.tile(jnp.concatenate([cos, cos], axis=-1), (1, reps))
    sin_t = jnp.tile(jnp.concatenate([-sin, sin], axis=-1), (1, reps))
    return cos_t, sin_t


def _lambda_init(layer):
    return 0.8 - 0.6 * math.exp(-0.3 * layer)


def _trunk(x, mod_all, p, ssm_ops):
    batch, seq_len, d = x.shape
    depth = p["w_in"].shape[0]
    d_v = N_HEADS * V_DIM
    d_u = p["ssm_d"].shape[1]
    cos, sin = _rope_tables(seq_len)
    x = x.reshape(batch * seq_len, d)
    row = lambda a: a.reshape(1, -1)
    for i in range(depth):
        mod = mod_all[i].reshape(batch, 6, d)
        qt, k, vt, u, ga, gs = _in_projection(x, mod, row(p["norm1_g"][i]), cos, sin, p["w_in"], i,
                                              batch, seq_len, d_u)
        lam_params = jnp.stack([p["lam_q1"][i], p["lam_k1"][i], p["lam_q2"][i], p["lam_k2"][i]])
        o = _diff_attention(qt, k, vt, lam_params, row(p["subln_g"][i]), _lambda_init(i))
        y = _ssm_branch(u, i, *ssm_ops, batch, seq_len, SSM_CHUNK)
        x = _merge(x, mod, o.reshape(batch * seq_len, d_v), y, u, ga, gs, row(p["ssm_d"][i]),
                   p["w_glu"], row(p["b_glu"][i]), p["w_attn_br"], p["w_o"], i, seq_len,
                   in_place=(i > 0))
        x = _ffn(x, mod, row(p["norm2_g"][i]), p["w_ffn_in"], p["w_ffn_out"],
                 row(p["final_g"]), i, seq_len, final_norm=(i == depth - 1))
    return x.reshape(batch, seq_len, d)


def kernel(x_prompt, x_sample, c_prompt, c_sample, w_mod, b_mod, norm1_g, w_in, lam_q1, lam_k1, lam_q2, lam_k2, subln_g, w_attn_br, ssm_a_re, ssm_a_im, ssm_log_dt, ssm_b_re, ssm_b_im, ssm_c_re, ssm_c_im, ssm_d, w_glu, b_glu, w_o, norm2_g, w_ffn_in, w_ffn_out, final_g):
    bp, bs = c_prompt.shape[0], c_sample.shape[0]
    pad = -(bp + bs) % 8
    c_all = jnp.concatenate([c_prompt, c_sample, jnp.zeros((pad, c_prompt.shape[1]), F32)], axis=0)
    mod_all = _modulation(c_all, w_mod, b_mod)
    ssm_ops = _ssm_prep(ssm_a_re, ssm_a_im, ssm_log_dt, ssm_b_re, ssm_b_im, ssm_c_re, ssm_c_im, SSM_CHUNK)
    p = dict(
        norm1_g=norm1_g, w_in=w_in.astype(BF16), lam_q1=lam_q1, lam_k1=lam_k1, lam_q2=lam_q2, lam_k2=lam_k2,
        subln_g=subln_g, w_attn_br=w_attn_br.astype(BF16), ssm_d=ssm_d, w_glu=w_glu.astype(BF16),
        b_glu=b_glu, w_o=w_o.astype(BF16), norm2_g=norm2_g, w_ffn_in=w_ffn_in.astype(BF16),
        w_ffn_out=w_ffn_out.astype(BF16), final_g=final_g)
    y_prompt = _trunk(x_prompt, mod_all[:, :bp], p, ssm_ops)
    y_sample = _trunk(x_sample, mod_all[:, bp:bp + bs], p, ssm_ops)
    return (y_prompt, y_sample)
```

```python
import functools
import math

import jax
import jax.numpy as jnp
from jax import lax
from jax.experimental import pallas as pl
from jax.experimental.pallas import tpu as pltpu

F32 = jnp.float32
BF16 = jnp.bfloat16

N_HEADS = 4
HEAD_DIM = 64
V_DIM = 2 * HEAD_DIM
GROUP_CH = 16
N_STATE = 64
ROPE_THETA = 10000.0
EPS = 1e-6
Q_SCALE = HEAD_DIM ** -0.5 * math.log2(math.e)
V_PAD = 16

VMEM_LIMIT_BYTES = 56 * 1024 * 1024
LANES = 128

TOKEN_TILE = 512
ATTN_TQ = 256
ATTN_TK = 256
SSM_CHUNK = 64
FFN_CHUNK = 256
SCAN_GROUPS = 4
RELAYOUT_TILE = 1024


def _cparams(*sem):
    return pltpu.CompilerParams(dimension_semantics=sem, vmem_limit_bytes=VMEM_LIMIT_BYTES)


def _dot(a, b):
    return jnp.dot(a, b, preferred_element_type=F32)


def _split3(x):
    hi = x.astype(BF16)
    r1 = x - hi.astype(F32)
    mid = r1.astype(BF16)
    lo = (r1 - mid.astype(F32)).astype(BF16)
    return hi, mid, lo


def _dot_f32(a, b):
    a_hi = a.astype(BF16)
    a_lo = (a - a_hi.astype(F32)).astype(BF16)
    b_hi = b.astype(BF16)
    b_lo = (b - b_hi.astype(F32)).astype(BF16)
    return _dot(a_hi, b_hi) + (_dot(a_hi, b_lo) + _dot(a_lo, b_hi))


def _select_cols(x, sel):
    hi, mid, lo = _split3(x)
    return _dot(hi, sel) + (_dot(mid, sel) + _dot(lo, sel))


def _select_rows(sel, x):
    hi, mid, lo = _split3(x)
    return _dot(sel, hi) + (_dot(sel, mid) + _dot(sel, lo))


def _onehot(cond):
    return jnp.where(cond, 1.0, 0.0).astype(BF16)


def _rms(x):
    return x * lax.rsqrt(jnp.mean(x * x, axis=-1, keepdims=True) + EPS)


def _mod_kernel(c_ref, w_ref, b_ref, o_ref):
    c = c_ref[...]
    s = c * jax.nn.sigmoid(c)
    o_ref[0] = _dot_f32(s, w_ref[0]) + b_ref[0]


def _modulation(c_all, w_mod, b_mod):
    depth, d, n6 = w_mod.shape
    bp = c_all.shape[0]
    tn = 1536
    return pl.pallas_call(
        _mod_kernel,
        grid=(depth, n6 // tn),
        in_specs=[
            pl.BlockSpec((bp, d), lambda l, j: (0, 0)),
            pl.BlockSpec((1, d, tn), lambda l, j: (l, 0, j)),
            pl.BlockSpec((1, 1, tn), lambda l, j: (l, 0, j)),
        ],
        out_specs=pl.BlockSpec((1, bp, tn), lambda l, j: (l, 0, j)),
        out_shape=jax.ShapeDtypeStruct((depth, bp, n6), F32),
        compiler_params=_cparams("parallel", "parallel"),
        name="modulation",
    )(c_all, w_mod, b_mod.reshape(depth, 1, n6))


def _inproj_kernel(x_ref, mod_ref, g_ref, cos_ref, sin_ref, w_ref,
                   qt_ref, k_ref, vt_ref, u_ref, ga_ref, gs_ref):
    d = x_ref.shape[1]
    n_heads = k_ref.shape[1]
    d_qk = n_heads * V_DIM
    mod = mod_ref[0]
    h = (_rms(x_ref[...]) * g_ref[...] * (1.0 + mod[1:2]) + mod[0:1]).astype(BF16)

    qk = _dot(h, w_ref[:, 0:2 * d_qk])
    reps = 2 * d_qk // LANES
    cos = jnp.tile(cos_ref[...], (1, reps))
    sin = jnp.tile(sin_ref[...], (1, reps))
    lane = lax.broadcasted_iota(jnp.int32, qk.shape, 1)
    first_half = jnp.bitwise_and(lane, HEAD_DIM // 2) == 0
    half = HEAD_DIM // 2
    rot = jnp.where(first_half,
                    pltpu.roll(qk, 2 * d_qk - half, 1),
                    pltpu.roll(qk, half, 1))
    qk = qk * cos + rot * sin
    tq = qt_ref.shape[4]
    for hd in range(n_heads):
        qh = qk[:, hd * V_DIM:(hd + 1) * V_DIM] * Q_SCALE
        for c in range(qt_ref.shape[2]):
            qt_ref[0, hd, c] = qh[c * tq:(c + 1) * tq].T.astype(BF16)
        k_ref[0, hd] = qk[:, d_qk + hd * V_DIM:d_qk + (hd + 1) * V_DIM].astype(BF16)

    off = 2 * d_qk
    d_v = n_heads * V_DIM
    d_u = u_ref.shape[1]
    vu = _dot(h, w_ref[:, off:off + d_v + d_u])
    tk = vt_ref.shape[4]
    tail_row = lax.broadcasted_iota(jnp.int32, (V_PAD, tk), 0)
    ones_row = jnp.where(tail_row == 0, 1.0, 0.0).astype(BF16)
    for hd in range(n_heads):
        for c in range(vt_ref.shape[2]):
            vt_ref[0, hd, c, :V_DIM, :] = vu[c * tk:(c + 1) * tk, hd * V_DIM:(hd + 1) * V_DIM].T.astype(BF16)
            vt_ref[0, hd, c, V_DIM:, :] = ones_row
    u_ref[...] = vu[:, d_v:].astype(BF16)
    off += d_v + d_u
    ga_ref[...] = jax.nn.sigmoid(_dot(h, w_ref[:, off:off + d])).astype(BF16)
    gs_ref[...] = jax.nn.sigmoid(_dot(h, w_ref[:, off + d:off + 2 * d])).astype(BF16)


def _in_projection(x, mod, g, cos, sin, w_all, layer, batch, seq_len, d_u):
    n, d = x.shape
    tm = min(TOKEN_TILE, seq_len)
    tps = seq_len // tm
    tk = min(ATTN_TK, tm)
    tq = min(ATTN_TQ, tm)
    tok = lambda i: (i, 0)
    bf = lambda *shape: jax.ShapeDtypeStruct(shape, BF16)
    return pl.pallas_call(
        _inproj_kernel,
        grid=(n // tm,),
        in_specs=[
            pl.BlockSpec((tm, d), tok),
            pl.BlockSpec((1, 6, d), lambda i: (i // tps, 0, 0)),
            pl.BlockSpec((1, d), lambda i: (0, 0)),
            pl.BlockSpec((tm, LANES), lambda i: (i % tps, 0)),
            pl.BlockSpec((tm, LANES), lambda i: (i % tps, 0)),
            pl.BlockSpec((None,) + w_all.shape[1:], lambda i: (layer, 0, 0)),
        ],
        out_specs=[
            pl.BlockSpec((1, N_HEADS, tm // tq, V_DIM, tq), lambda i: (i // tps, 0, i % tps, 0, 0)),
            pl.BlockSpec((1, N_HEADS, tm, V_DIM), lambda i: (i // tps, 0, i % tps, 0)),
            pl.BlockSpec((1, N_HEADS, tm // tk, V_DIM + V_PAD, tk), lambda i: (i // tps, 0, i % tps, 0, 0)),
            pl.BlockSpec((tm, d_u), tok),
            pl.BlockSpec((tm, d), tok),
            pl.BlockSpec((tm, d), tok),
        ],
        out_shape=[
            bf(batch, N_HEADS, seq_len // tq, V_DIM, tq),
            bf(batch, N_HEADS, seq_len, V_DIM),
            bf(batch, N_HEADS, seq_len // tk, V_DIM + V_PAD, tk),
            bf(n, d_u), bf(n, d), bf(n, d),
        ],
        compiler_params=_cparams("parallel"),
        name="in_projection",
    )(x, mod, g, cos, sin, w_all)


def _attn_kernel(qt_ref, k_ref, vt_ref, lam_ref, g_ref, o_ref, *scratch, lam_init):
    n_chunks, _, tk = vt_ref.shape[2:]
    n_blocks, _, tq = qt_ref.shape[2:]
    per_set = len(scratch) // 2
    sets = (scratch[:per_set], scratch[per_set:])

    lp = lam_ref[...]
    lam = (jnp.exp(jnp.sum(lp[0:1] * lp[1:2], axis=-1, keepdims=True))
           - jnp.exp(jnp.sum(lp[2:3] * lp[3:4], axis=-1, keepdims=True)) + lam_init)
    gain = g_ref[...] * (1.0 - lam_init)

    def q_operand(j):
        qt = qt_ref[0, 0, j].astype(F32)
        row = lax.broadcasted_iota(jnp.int32, qt.shape, 0)
        return jnp.concatenate([jnp.where(row < HEAD_DIM, qt, 0.0), jnp.where(row >= HEAD_DIM, qt, 0.0)],
                               axis=1).astype(BF16)

    def scores(qtb, c):
        return _dot(k_ref[0, 0, c * tk:(c + 1) * tk, :], qtb)

    def softmax(st, slot, m_old):
        s = st[slot][...]
        m_new = jnp.maximum(m_old, jnp.max(s, axis=0, keepdims=True))
        st[2 + slot][...] = jnp.exp2(s - m_new).astype(BF16)
        return m_new, jnp.exp2(m_old - m_new)

    def attend(st, c, slot, alpha):
        st[4][...] = alpha * st[4][...] + _dot(vt_ref[0, 0, c], st[2 + slot][...])

    def head(st, j):
        qtb = q_operand(j)
        st[4][...] = jnp.zeros(st[4].shape, F32)
        st[0][...] = scores(qtb, 0)
        st[1][...] = scores(qtb, 1)
        return (qtb,) + softmax(st, 0, jnp.full((1, 2 * tq), -jnp.inf, F32))

    def steady(st, carry):
        qtb, m, alpha = carry
        for c in range(1, n_chunks - 1):
            slot = c % 2
            st[1 - slot][...] = scores(qtb, c + 1)
            m, alpha_new = softmax(st, slot, m)
            attend(st, c - 1, 1 - slot, alpha)
            alpha = alpha_new
        return m, alpha

    def tail(st, j, m, alpha):
        last = (n_chunks - 1) % 2
        _, alpha_last = softmax(st, last, m)
        attend(st, n_chunks - 2, 1 - last, alpha)
        attend(st, n_chunks - 1, last, alpha_last)
        acc = st[4][...]
        acc = acc[:V_DIM] / acc[V_DIM:V_DIM + 1]
        o = (acc[:, :tq] - lam * acc[:, tq:]).T
        o_ref[0, pl.ds(pl.multiple_of(j * tq, tq), tq), :] = (_rms(o) * gain).astype(o_ref.dtype)

    def body(jj, carry):
        j0 = 2 * jj
        m0, a0 = steady(sets[0], carry)
        carry1 = head(sets[1], j0 + 1)
        tail(sets[0], j0, m0, a0)
        m1, a1 = steady(sets[1], carry1)
        nxt = head(sets[0], jnp.minimum(j0 + 2, n_blocks - 1))
        tail(sets[1], j0 + 1, m1, a1)
        return nxt

    lax.fori_loop(0, n_blocks // 2, body, head(sets[0], 0))


def _diff_attention(qt, k, vt, lam_params, subln_g, lam_init):
    b, n_heads, l, _ = k.shape
    n_blocks, _, tq = qt.shape[2:]
    n_chunks, v_rows, tk = vt.shape[2:]
    assert n_blocks % 2 == 0 and n_chunks >= 2
    one_set = ([pltpu.VMEM((tk, 2 * tq), F32)] * 2 + [pltpu.VMEM((tk, 2 * tq), BF16)] * 2
               + [pltpu.VMEM((v_rows, 2 * tq), F32)])
    whole = lambda a: pl.BlockSpec((1, 1) + a.shape[2:], lambda bi, h: (bi, h) + (0,) * (a.ndim - 2))
    return pl.pallas_call(
        functools.partial(_attn_kernel, lam_init=lam_init),
        grid=(b, n_heads),
        in_specs=[
            whole(qt), whole(k), whole(vt),
            pl.BlockSpec(lam_params.shape, lambda bi, h: (0, 0)),
            pl.BlockSpec((1, V_DIM), lambda bi, h: (0, 0)),
        ],
        out_specs=pl.BlockSpec((1, l, V_DIM), lambda bi, h: (bi, 0, h)),
        out_shape=jax.ShapeDtypeStruct((b, l, n_heads * V_DIM), BF16),
        scratch_shapes=one_set * 2,
        compiler_params=_cparams("parallel", "parallel"),
        name="diff_attention",
    )(qt, k, vt, lam_params, subln_g)


def _ssm_prep_kernel(arow_ref, acol_ref, bt_ref, ct_ref, m_ref, win_ref, wout_ref, dec_ref, *, chunk):
    t_len = chunk
    w = t_len * GROUP_CH
    w2 = 2 * w
    arow = arow_ref[0, 0]
    acol = acol_ref[0, 0]
    fwd_lanes = lax.broadcasted_iota(jnp.int32, (1, LANES), 1) < N_STATE

    ar = arow[0:1]
    ai = arow[1:2]
    dt = jnp.exp(arow[2:3])
    zr = dt * ar
    zi = dt * ai
    mag = jnp.exp(zr)
    nr = mag * jnp.cos(zi) - 1.0
    ni = mag * jnp.sin(zi)
    den = ar * ar + ai * ai
    fr = (nr * ar + ni * ai) / den
    fi = (ni * ar - nr * ai) / den
    br = bt_ref[0, 0, 0]
    bi = bt_ref[0, 0, 1]
    bbr = fr * br - fi * bi
    bbi = fr * bi + fi * br

    s_of_row = lax.shift_right_logical(lax.broadcasted_iota(jnp.int32, (w, t_len), 0), 4)
    j_of_col = lax.broadcasted_iota(jnp.int32, (w, t_len), 1)
    rep_fwd = _onehot(j_of_col == (t_len - 1 - s_of_row))
    rep_bwd = _onehot(j_of_col == s_of_row)
    n_rows = lax.broadcasted_iota(jnp.int32, (t_len, 1), 0).astype(F32)
    pm = jnp.exp(n_rows * zr)

    def expand(p):
        return jnp.where(fwd_lanes, _select_rows(rep_fwd, p), _select_rows(rep_bwd, p))

    e_re = expand(pm * jnp.cos(n_rows * zi))
    e_im = expand(pm * jnp.sin(n_rows * zi))
    b_re = jnp.tile(bbr, (t_len, 1))
    b_im = jnp.tile(bbi, (t_len, 1))
    win_ref[0, 0, :, :LANES] = (e_re * b_re - e_im * b_im).astype(win_ref.dtype)
    win_ref[0, 0, :, LANES:] = (e_re * b_im + e_im * b_re).astype(win_ref.dtype)
    dm = jnp.exp(t_len * zr)
    dec_ref[0, 0, 0:1, :] = dm * jnp.cos(t_len * zi)
    dec_ref[0, 0, 1:2, :] = dm * jnp.sin(t_len * zi)

    def lane_maps(width):
        lane = lax.broadcasted_iota(jnp.int32, (LANES, width), 1)
        return (lax.shift_right_logical(lane, 4), jnp.bitwise_and(lane, GROUP_CH - 1),
                lax.broadcasted_iota(jnp.int32, (LANES, width), 0))

    lag_idx, ch_idx, jrow = lane_maps(w2)
    lag_idx_w, _, jrow_w = lane_maps(w)
    tile_ch = _onehot(jrow == ch_idx)
    n_lanes = jnp.minimum(lax.broadcasted_iota(jnp.int32, (1, LANES), 1), t_len).astype(F32)

    lag_tables = []
    out_tables = []
    for d in range(2):
        dtc = jnp.exp(acol[:, 4 + d:5 + d])
        zrc = dtc * acol[:, d:d + 1]
        zic = dtc * acol[:, 2 + d:3 + d]
        ptm = jnp.exp(zrc * n_lanes)
        pt_re = ptm * jnp.cos(zic * n_lanes)
        pt_im = ptm * jnp.sin(zic * n_lanes)
        c_re = _select_cols(ct_ref[0, 0, d], tile_ch)
        c_im = _select_cols(ct_ref[0, 0, 2 + d], tile_ch)

        def table(rep):
            width = rep.shape[1]
            p_re = _select_cols(pt_re, rep)
            p_im = _select_cols(pt_im, rep)
            cr = c_re[:, :width]
            ci = c_im[:, :width]
            return p_re * cr - p_im * ci, p_re * ci + p_im * cr

        if d == 0:
            power = lag_idx - (t_len - 1)
        else:
            power = (t_len - 1) - lag_idx
        power = jnp.where(power >= 0, power, -1)
        f_re, f_im = table(_onehot(jrow == power))
        lag_tables.append((f_re, f_im))
        if d == 0:
            out_tables.append((f_re[:, w:], f_im[:, w:]))
        else:
            out_tables.append(table(_onehot(jrow_w == (t_len - lag_idx_w))))

    for i, o in enumerate((out_tables[0][0], out_tables[1][0], -out_tables[0][1], -out_tables[1][1])):
        wout_ref[0, 0, i * N_STATE:(i + 1) * N_STATE, :] = o.astype(wout_ref.dtype)

    lhs = jnp.concatenate([bbr, -bbi], axis=1)
    rhs = jnp.concatenate([lag_tables[0][0], lag_tables[1][0], lag_tables[0][1], lag_tables[1][1]], axis=0)
    strip = _dot_f32(lhs, rhs)

    per_tile = LANES // GROUP_CH
    for r in range(per_tile):
        rolled = strip if r == 0 else pltpu.roll(strip, w2 - r * GROUP_CH, 1)
        for s in range(t_len):
            if (t_len - 1 - s) % per_tile == r:
                q = (t_len - 1 - s) // per_tile
                m_ref[0, 0, s * GROUP_CH:(s + 1) * GROUP_CH, :] = (
                    rolled[:, q * LANES:q * LANES + w].astype(m_ref.dtype))


def _ssm_prep(a_re, a_im, log_dt, b_re, b_im, c_re, c_im, chunk):
    depth, _, n_groups, n_state = a_re.shape
    w = chunk * GROUP_CH
    gd = lambda x: jnp.swapaxes(x, 1, 2)
    ldt = jnp.broadcast_to(gd(log_dt)[..., None], (depth, n_groups, 2, n_state))
    packed = lambda x: x.reshape(depth, n_groups, 1, 2 * n_state)
    arow = jnp.concatenate([packed(gd(a_re)), packed(gd(a_im)), packed(ldt),
                            jnp.zeros((depth, n_groups, 5, 2 * n_state), F32)], axis=2)
    acol = jnp.swapaxes(jnp.concatenate([gd(a_re), gd(a_im), ldt, jnp.zeros_like(ldt)], axis=2), 2, 3)
    bt = jnp.stack([gd(b_re), gd(b_im)], axis=2)
    bt = bt.transpose(0, 1, 2, 5, 3, 4).reshape(depth, n_groups, 2, GROUP_CH, 2 * n_state)
    ct = jnp.swapaxes(jnp.concatenate([gd(c_re), gd(c_im)], axis=2), 3, 4)
    ct = jnp.pad(ct, ((0, 0),) * 4 + ((0, LANES - GROUP_CH),))
    blk = lambda *tail: pl.BlockSpec((1, 1) + tail, lambda l, g: (l, g) + (0,) * len(tail))
    return pl.pallas_call(
        functools.partial(_ssm_prep_kernel, chunk=chunk),
        grid=(depth, n_groups),
        in_specs=[blk(8, LANES), blk(n_state, 8), blk(2, GROUP_CH, LANES), blk(4, n_state, LANES)],
        out_specs=[blk(w, w), blk(w, 4 * n_state), blk(4 * n_state, w), blk(2, LANES)],
        out_shape=[
            jax.ShapeDtypeStruct((depth, n_groups, w, w), BF16),
            jax.ShapeDtypeStruct((depth, n_groups, w, 4 * n_state), BF16),
            jax.ShapeDtypeStruct((depth, n_groups, 4 * n_state, w), BF16),
            jax.ShapeDtypeStruct((depth, n_groups, 2, LANES), F32),
        ],
        compiler_params=_cparams("parallel", "parallel"),
        name="ssm_prep",
    )(arow, acol, bt, ct)


def _ssm_state_kernel(u_ref, win_ref, sre_ref, sim_ref):
    s = _dot(u_ref[0], win_ref[0, 0])
    sre_ref[0] = s[:, :LANES]
    sim_ref[0] = s[:, LANES:]


def _ssm_out_kernel(u_ref, hfr_ref, hfi_ref, hbr_ref, hbi_ref, m_ref, wout_ref, y_ref):
    fwd = lax.broadcasted_iota(jnp.int32, hfr_ref.shape[1:], 1) < N_STATE
    h = jnp.concatenate([jnp.where(fwd, hfr_ref[0], hbr_ref[0]),
                         jnp.where(fwd, hfi_ref[0], hbi_ref[0])], axis=1).astype(BF16)
    y_ref[0] = (_dot(u_ref[0], m_ref[0, 0]) + _dot(h, wout_ref[0, 0])).astype(y_ref.dtype)


def _ssm_scan_kernel(sre_ref, sim_ref, dre_ref, dim_ref, hfr_ref, hfi_ref, hbr_ref, hbi_ref,
                     *, batch, n_chunks):
    n_blk, _, lanes = sre_ref.shape
    fwd = jnp.bitwise_and(lax.broadcasted_iota(jnp.int32, (batch, lanes), 1), N_STATE) == 0

    def step(i, carry):
        rows_f = pl.ds(i, batch, stride=n_chunks)
        rows_b = pl.ds(n_chunks - 1 - i, batch, stride=n_chunks)
        out = []
        for q in range(n_blk):
            xr, xi = carry[2 * q], carry[2 * q + 1]
            hfr_ref[q, rows_f, :] = xr
            hfi_ref[q, rows_f, :] = xi
            hbr_ref[q, rows_b, :] = xr
            hbi_ref[q, rows_b, :] = xi
            sr = jnp.where(fwd, sre_ref[q, rows_f, :], sre_ref[q, rows_b, :])
            si = jnp.where(fwd, sim_ref[q, rows_f, :], sim_ref[q, rows_b, :])
            dr = dre_ref[q]
            di = dim_ref[q]
            out += [dr * xr - di * xi + sr, dr * xi + di * xr + si]
        return tuple(out)

    zeros = jnp.zeros((batch, lanes), F32)
    lax.fori_loop(0, n_chunks, step, (zeros,) * (2 * n_blk))


def _to_groups_kernel(u_ref, ug_ref, scr_ref, *, chunk):
    tm, d_u = u_ref.shape
    n_ch = tm // chunk
    per_tile = LANES // GROUP_CH
    for j in range(d_u // LANES):
        scr_ref[j] = u_ref[:, j * LANES:(j + 1) * LANES].astype(F32)
    for s in range(chunk):
        dst = (s % per_tile) * GROUP_CH
        for j in range(d_u // LANES):
            slab = scr_ref[j, pl.ds(s, n_ch, stride=chunk), :]
            for gg in range(per_tile):
                shift = (dst - gg * GROUP_CH) % LANES
                moved = (pltpu.roll(slab, shift, 1) if shift else slab).astype(ug_ref.dtype)
                ug_ref[j * per_tile + gg, :, s * GROUP_CH:(s + 1) * GROUP_CH] = moved[:, dst:dst + GROUP_CH]


def _from_groups_kernel(yg_ref, y_ref, scr_ref, *, chunk):
    tm, d_u = y_ref.shape
    n_ch = tm // chunk
    per_tile = LANES // GROUP_CH
    group_of_lane = lax.shift_right_logical(lax.broadcasted_iota(jnp.int32, (n_ch, LANES), 1), 4)
    for s in range(chunk):
        src = (s % per_tile) * GROUP_CH
        base = (s // per_tile) * LANES
        for j in range(d_u // LANES):
            slab = jnp.zeros((n_ch, LANES), F32)
            for gg in range(per_tile):
                v = yg_ref[j * per_tile + gg, :, base:base + LANES].astype(F32)
                shift = (gg * GROUP_CH - src) % LANES
                v = pltpu.roll(v, shift, 1) if shift else v
                slab = jnp.where(group_of_lane == gg, v, slab)
            scr_ref[j, pl.ds(s, n_ch, stride=chunk), :] = slab
    for j in range(d_u // LANES):
        y_ref[:, j * LANES:(j + 1) * LANES] = scr_ref[j].astype(y_ref.dtype)


def _group_relayout(x, to_groups, batch, seq_len, chunk, d_u):
    n_groups = d_u // GROUP_CH
    w = chunk * GROUP_CH
    n_chunks = seq_len // chunk
    tm = min(RELAYOUT_TILE, seq_len)
    tps = seq_len // tm
    tok_spec = pl.BlockSpec((tm, d_u), lambda i: (i, 0))
    grp_spec = pl.BlockSpec((n_groups, tm // chunk, w), lambda i: (0, i, 0))
    grp_shape = (n_groups, batch * n_chunks, w)
    scratch = [pltpu.VMEM((d_u // LANES, tm, LANES), F32)]
    if to_groups:
        return pl.pallas_call(
            functools.partial(_to_groups_kernel, chunk=chunk),
            grid=(batch * tps,), in_specs=[tok_spec], out_specs=grp_spec,
            out_shape=jax.ShapeDtypeStruct(grp_shape, x.dtype), scratch_shapes=scratch,
            compiler_params=_cparams("parallel"), name="ssm_to_groups",
        )(x)
    return pl.pallas_call(
        functools.partial(_from_groups_kernel, chunk=chunk),
        grid=(batch * tps,), in_specs=[grp_spec], out_specs=tok_spec,
        out_shape=jax.ShapeDtypeStruct((batch * seq_len, d_u), x.dtype), scratch_shapes=scratch,
        compiler_params=_cparams("parallel"), name="ssm_from_groups",
    )(x)


def _ssm_branch(u, layer, m_all, win_all, wout_all, dec_all, batch, seq_len, chunk):
    n, d_u = u.shape
    n_groups = d_u // GROUP_CH
    w = chunk * GROUP_CH
    n_chunks = seq_len // chunk
    rows = batch * n_chunks
    st = 4 * N_STATE

    ug = _group_relayout(u, True, batch, seq_len, chunk, d_u)
    state = jax.ShapeDtypeStruct((n_groups, rows, LANES), F32)
    one = lambda g: (g, 0, 0)
    sre, sim = pl.pallas_call(
        _ssm_state_kernel,
        grid=(n_groups,),
        in_specs=[pl.BlockSpec((1, rows, w), one),
                  pl.BlockSpec((1, 1, w, st), lambda g: (layer, g, 0, 0))],
        out_specs=[pl.BlockSpec((1, rows, LANES), one)] * 2,
        out_shape=[state] * 2,
        compiler_params=_cparams("parallel"),
        name="ssm_chunk_state",
    )(ug, win_all)

    dec = dec_all[layer]
    gb = min(SCAN_GROUPS, n_groups)
    hs = pl.pallas_call(
        functools.partial(_ssm_scan_kernel, batch=batch, n_chunks=n_chunks),
        grid=(n_groups // gb,),
        in_specs=[pl.BlockSpec((gb, rows, LANES), one)] * 2 + [pl.BlockSpec((gb, 1, LANES), one)] * 2,
        out_specs=[pl.BlockSpec((gb, rows, LANES), one)] * 4,
        out_shape=[state] * 4,
        compiler_params=_cparams("parallel"),
        name="ssm_chunk_scan",
    )(sre, sim, dec[:, 0:1, :], dec[:, 1:2, :])

    y = pl.pallas_call(
        _ssm_out_kernel,
        grid=(n_groups,),
        in_specs=[pl.BlockSpec((1, rows, w), one)]
                 + [pl.BlockSpec((1, rows, LANES), one)] * 4
                 + [pl.BlockSpec((1, 1, w, w), lambda g: (layer, g, 0, 0)),
                    pl.BlockSpec((1, 1, st, w), lambda g: (layer, g, 0, 0))],
        out_specs=pl.BlockSpec((1, rows, w), one),
        out_shape=jax.ShapeDtypeStruct((n_groups, rows, w), BF16),
        compiler_params=_cparams("parallel"),
        name="ssm_chunk_output",
    )(ug, *hs, m_all, wout_all)
    return _group_relayout(y, False, batch, seq_len, chunk, d_u)


def _merge_kernel(x_ref, mod_ref, o_ref, y_ref, u_ref, ga_ref, gs_ref, d_ref,
                  wglu_ref, bglu_ref, wattn_ref, wo_ref, out_ref):
    d = x_ref.shape[1]
    y = y_ref[...] + d_ref[...] * u_ref[...].astype(F32)
    z = jax.nn.gelu(y, approximate=True).astype(BF16)
    glu = _dot(z, wglu_ref[...]) + bglu_ref[...]
    y_ssm = glu[:, :d] * jax.nn.sigmoid(glu[:, d:])
    y_attn = _dot(o_ref[...], wattn_ref[...])
    merged = (ga_ref[...].astype(F32) * y_attn + gs_ref[...].astype(F32) * y_ssm).astype(BF16)
    out_ref[...] = x_ref[...] + mod_ref[0][2:3] * _dot(merged, wo_ref[...])


def _merge(x, mod, o, y, u, ga, gs, ssm_d, w_glu, b_glu, w_attn, w_o, layer, seq_len, in_place):
    n, d = x.shape
    tm = min(TOKEN_TILE, seq_len)
    tps = seq_len // tm
    tok = lambda i: (i, 0)
    full = lambda a: pl.BlockSpec(a.shape, lambda i: (0, 0))
    of_layer = lambda a: pl.BlockSpec((None,) + a.shape[1:], lambda i: (layer, 0, 0))
    return pl.pallas_call(
        _merge_kernel,
        grid=(n // tm,),
        in_specs=[
            pl.BlockSpec((tm, d), tok),
            pl.BlockSpec((1, 6, d), lambda i: (i // tps, 0, 0)),
            pl.BlockSpec((tm, o.shape[1]), tok),
            pl.BlockSpec((tm, y.shape[1]), tok),
            pl.BlockSpec((tm, u.shape[1]), tok),
            pl.BlockSpec((tm, d), tok),
            pl.BlockSpec((tm, d), tok),
            full(ssm_d), of_layer(w_glu), full(b_glu), of_layer(w_attn), of_layer(w_o),
        ],
        out_specs=pl.BlockSpec((tm, d), tok),
        out_shape=jax.ShapeDtypeStruct((n, d), F32),
        input_output_aliases={0: 0} if in_place else {},
        compiler_params=_cparams("parallel"),
        name="merge_out_projection",
    )(x, mod, o, y, u, ga, gs, ssm_d, w_glu, b_glu, w_attn, w_o)


def _ffn_kernel(x_ref, mod_ref, g_ref, win_ref, wd_ref, fg_ref, out_ref, h_ref, t_ref, *, final_norm):
    mod = mod_ref[0]
    d_ff = wd_ref.shape[0]
    h_ref[...] = (_rms(x_ref[...]) * g_ref[...] * (1.0 + mod[4:5]) + mod[3:4]).astype(BF16)
    for c in range(d_ff // FFN_CHUNK):
        cols = slice(c * FFN_CHUNK, (c + 1) * FFN_CHUNK)
        gate = _dot(h_ref[...], win_ref[:, cols])
        up = _dot(h_ref[...], win_ref[:, d_ff + c * FFN_CHUNK:d_ff + (c + 1) * FFN_CHUNK])
        t_ref[:, cols] = (gate * jax.nn.sigmoid(gate) * up).astype(BF16)
    xn = x_ref[...] + mod[5:6] * _dot(t_ref[...], wd_ref[...])
    if final_norm:
        xn = _rms(xn) * fg_ref[...]
    out_ref[...] = xn


def _ffn(x, mod, g, w_in, w_out, final_g, layer, seq_len, final_norm):
    n, d = x.shape
    d_ff = w_out.shape[1]
    assert d_ff % FFN_CHUNK == 0
    tm = min(TOKEN_TILE, seq_len)
    tps = seq_len // tm
    return pl.pallas_call(
        functools.partial(_ffn_kernel, final_norm=final_norm),
        grid=(n // tm,),
        in_specs=[
            pl.BlockSpec((tm, d), lambda i: (i, 0)),
            pl.BlockSpec((1, 6, d), lambda i: (i // tps, 0, 0)),
            pl.BlockSpec((1, d), lambda i: (0, 0)),
            pl.BlockSpec((None, d, 2 * d_ff), lambda i: (layer, 0, 0)),
            pl.BlockSpec((None, d_ff, d), lambda i: (layer, 0, 0)),
            pl.BlockSpec((1, d), lambda i: (0, 0)),
        ],
        out_specs=pl.BlockSpec((tm, d), lambda i: (i, 0)),
        out_shape=jax.ShapeDtypeStruct((n, d), F32),
        scratch_shapes=[pltpu.VMEM((tm, d), BF16), pltpu.VMEM((tm, d_ff), BF16)],
        input_output_aliases={0: 0},
        compiler_params=_cparams("parallel"),
        name="ffn",
    )(x, mod, g, w_in, w_out, final_g)


def _rope_tables(seq_len):
    inv = 1.0 / (ROPE_THETA ** (jnp.arange(0, HEAD_DIM, 2, dtype=F32) / HEAD_DIM))
    ang = jnp.arange(seq_len, dtype=F32)[:, None] * inv[None, :]
    cos = jnp.cos(ang)
    sin = jnp.sin(ang)
    reps = LANES // HEAD_DIM
    cos_t = jnp.tile(jnp.concatenate([cos, cos], axis=-1), (1, reps))
    sin_t = jnp.tile(jnp.concatenate([-sin, sin], axis=-1), (1, reps))
    return cos_t, sin_t


def _lambda_init(layer):
    return 0.8 - 0.6 * math.exp(-0.3 * layer)


def _trunk(x, mod_all, p, ssm_ops):
    batch, seq_len, d = x.shape
    depth = p["w_in"].shape[0]
    d_v = N_HEADS * V_DIM
    d_u = p["ssm_d"].shape[1]
    cos, sin = _rope_tables(seq_len)
    x = x.reshape(batch * seq_len, d)
    row = lambda a: a.reshape(1, -1)
    for i in range(depth):
        mod = mod_all[i].reshape(batch, 6, d)
        qt, k, vt, u, ga, gs = _in_projection(x, mod, row(p["norm1_g"][i]), cos, sin, p["w_in"], i,
                                              batch, seq_len, d_u)
        lam_params = jnp.stack([p["lam_q1"][i], p["lam_k1"][i], p["lam_q2"][i], p["lam_k2"][i]])
        o = _diff_attention(qt, k, vt, lam_params, row(p["subln_g"][i]), _lambda_init(i))
        y = _ssm_branch(u, i, *ssm_ops, batch, seq_len, SSM_CHUNK)
        x = _merge(x, mod, o.reshape(batch * seq_len, d_v), y, u, ga, gs, row(p["ssm_d"][i]),
                   p["w_glu"], row(p["b_glu"][i]), p["w_attn_br"], p["w_o"], i, seq_len,
                   in_place=(i > 0))
        x = _ffn(x, mod, row(p["norm2_g"][i]), p["w_ffn_in"], p["w_ffn_out"],
                 row(p["final_g"]), i, seq_len, final_norm=(i == depth - 1))
    return x.reshape(batch, seq_len, d)


def kernel(x_prompt, x_sample, c_prompt, c_sample, w_mod, b_mod, norm1_g, w_in, lam_q1, lam_k1, lam_q2, lam_k2, subln_g, w_attn_br, ssm_a_re, ssm_a_im, ssm_log_dt, ssm_b_re, ssm_b_im, ssm_c_re, ssm_c_im, ssm_d, w_glu, b_glu, w_o, norm2_g, w_ffn_in, w_ffn_out, final_g):
    bp, bs = c_prompt.shape[0], c_sample.shape[0]
    pad = -(bp + bs) % 8
    c_all = jnp.concatenate([c_prompt, c_sample, jnp.zeros((pad, c_prompt.shape[1]), F32)], axis=0)
    mod_all = _modulation(c_all, w_mod, b_mod)
    ssm_ops = _ssm_prep(ssm_a_re, ssm_a_im, ssm_log_dt, ssm_b_re, ssm_b_im, ssm_c_re, ssm_c_im, SSM_CHUNK)
    p = dict(
        norm1_g=norm1_g, w_in=w_in.astype(BF16), lam_q1=lam_q1, lam_k1=lam_k1, lam_q2=lam_q2, lam_k2=lam_k2,
        subln_g=subln_g, w_attn_br=w_attn_br.astype(BF16), ssm_d=ssm_d, w_glu=w_glu.astype(BF16),
        b_glu=b_glu, w_o=w_o.astype(BF16), norm2_g=norm2_g, w_ffn_in=w_ffn_in.astype(BF16),
        w_ffn_out=w_ffn_out.astype(BF16), final_g=final_g)
    y_prompt = _trunk(x_prompt, mod_all[:, :bp], p, ssm_ops)
    y_sample = _trunk(x_sample, mod_all[:, bp:bp + bs], p, ssm_ops)
    return (y_prompt, y_sample)
```

```python
import functools
import math

import jax
import jax.numpy as jnp
from jax import lax
from jax.experimental import pallas as pl
from jax.experimental.pallas import tpu as pltpu

F32 = jnp.float32
BF16 = jnp.bfloat16

N_HEADS = 4
HEAD_DIM = 64
V_DIM = 2 * HEAD_DIM
GROUP_CH = 16
N_STATE = 64
ROPE_THETA = 10000.0
EPS = 1e-6
Q_SCALE = HEAD_DIM ** -0.5 * math.log2(math.e)
V_PAD = 16

VMEM_LIMIT_BYTES = 56 * 1024 * 1024
LANES = 128

TOKEN_TILE = 512
PROJ_TILE = 1024
ATTN_TQ = 256
ATTN_TK = 256
ATTN_PAR_ROWS = 32
SSM_CHUNK = 64
FFN_CHUNK = 256
SCAN_GROUPS = 4
RELAYOUT_TILE = 1024


def _cparams(*sem):
    return pltpu.CompilerParams(dimension_semantics=sem, vmem_limit_bytes=VMEM_LIMIT_BYTES)


def _dot(a, b):
    return jnp.dot(a, b, preferred_element_type=F32)


def _split3(x):
    hi = x.astype(BF16)
    r1 = x - hi.astype(F32)
    mid = r1.astype(BF16)
    lo = (r1 - mid.astype(F32)).astype(BF16)
    return hi, mid, lo


def _dot_f32(a, b):
    a_hi = a.astype(BF16)
    a_lo = (a - a_hi.astype(F32)).astype(BF16)
    b_hi = b.astype(BF16)
    b_lo = (b - b_hi.astype(F32)).astype(BF16)
    return _dot(a_hi, b_hi) + (_dot(a_hi, b_lo) + _dot(a_lo, b_hi))


def _select_cols(x, sel):
    hi, mid, lo = _split3(x)
    return _dot(hi, sel) + (_dot(mid, sel) + _dot(lo, sel))


def _select_rows(sel, x):
    hi, mid, lo = _split3(x)
    return _dot(sel, hi) + (_dot(sel, mid) + _dot(sel, lo))


def _onehot(cond):
    return jnp.where(cond, 1.0, 0.0).astype(BF16)


def _rms(x):
    return x * lax.rsqrt(jnp.mean(x * x, axis=-1, keepdims=True) + EPS)


def _mod_kernel(c_ref, w_ref, b_ref, o_ref):
    c = c_ref[...]
    s = c * jax.nn.sigmoid(c)
    o_ref[0] = _dot_f32(s, w_ref[0]) + b_ref[0]


def _modulation(c_all, w_mod, b_mod):
    depth, d, n6 = w_mod.shape
    bp = c_all.shape[0]
    tn = 1536
    return pl.pallas_call(
        _mod_kernel,
        grid=(depth, n6 // tn),
        in_specs=[
            pl.BlockSpec((bp, d), lambda l, j: (0, 0)),
            pl.BlockSpec((1, d, tn), lambda l, j: (l, 0, j)),
            pl.BlockSpec((1, 1, tn), lambda l, j: (l, 0, j)),
        ],
        out_specs=pl.BlockSpec((1, bp, tn), lambda l, j: (l, 0, j)),
        out_shape=jax.ShapeDtypeStruct((depth, bp, n6), F32),
        compiler_params=_cparams("parallel", "parallel"),
        name="modulation",
    )(c_all, w_mod, b_mod.reshape(depth, 1, n6))


def _inproj_kernel(x_ref, mod_ref, g_ref, cos_ref, sin_ref, w_ref,
                   qt_ref, k_ref, vt_ref, u_ref, ga_ref, gs_ref):
    d = x_ref.shape[1]
    n_heads = k_ref.shape[1]
    d_qk = n_heads * V_DIM
    mod = mod_ref[0]
    h = (_rms(x_ref[...]) * g_ref[...] * (1.0 + mod[1:2]) + mod[0:1]).astype(BF16)

    qk = _dot(h, w_ref[:, 0:2 * d_qk])
    reps = 2 * d_qk // LANES
    cos = jnp.tile(cos_ref[...], (1, reps))
    sin = jnp.tile(sin_ref[...], (1, reps))
    lane = lax.broadcasted_iota(jnp.int32, qk.shape, 1)
    first_half = jnp.bitwise_and(lane, HEAD_DIM // 2) == 0
    half = HEAD_DIM // 2
    rot = jnp.where(first_half,
                    pltpu.roll(qk, 2 * d_qk - half, 1),
                    pltpu.roll(qk, half, 1))
    qk = qk * cos + rot * sin
    tq = qt_ref.shape[4]
    for hd in range(n_heads):
        qh = qk[:, hd * V_DIM:(hd + 1) * V_DIM] * Q_SCALE
        for c in range(qt_ref.shape[2]):
            qt_ref[0, hd, c] = qh[c * tq:(c + 1) * tq].T.astype(BF16)
        k_ref[0, hd] = qk[:, d_qk + hd * V_DIM:d_qk + (hd + 1) * V_DIM].astype(BF16)

    off = 2 * d_qk
    d_v = n_heads * V_DIM
    d_u = u_ref.shape[1]
    vu = _dot(h, w_ref[:, off:off + d_v + d_u])
    tk = vt_ref.shape[4]
    tail_row = lax.broadcasted_iota(jnp.int32, (V_PAD, tk), 0)
    ones_row = jnp.where(tail_row == 0, 1.0, 0.0).astype(BF16)
    for hd in range(n_heads):
        for c in range(vt_ref.shape[2]):
            vt_ref[0, hd, c, :V_DIM, :] = vu[c * tk:(c + 1) * tk, hd * V_DIM:(hd + 1) * V_DIM].T.astype(BF16)
            vt_ref[0, hd, c, V_DIM:, :] = ones_row
    u_ref[...] = vu[:, d_v:].astype(BF16)
    off += d_v + d_u
    ga_ref[...] = jax.nn.sigmoid(_dot(h, w_ref[:, off:off + d])).astype(BF16)
    gs_ref[...] = jax.nn.sigmoid(_dot(h, w_ref[:, off + d:off + 2 * d])).astype(BF16)


def _in_projection(x, mod, g, cos, sin, w_all, layer, batch, seq_len, d_u):
    n, d = x.shape
    tm = min(PROJ_TILE, seq_len)
    tps = seq_len // tm
    tk = min(ATTN_TK, tm)
    tq = min(ATTN_TQ, tm)
    tok = lambda i: (i, 0)
    bf = lambda *shape: jax.ShapeDtypeStruct(shape, BF16)
    return pl.pallas_call(
        _inproj_kernel,
        grid=(n // tm,),
        in_specs=[
            pl.BlockSpec((tm, d), tok),
            pl.BlockSpec((1, 6, d), lambda i: (i // tps, 0, 0)),
            pl.BlockSpec((1, d), lambda i: (0, 0)),
            pl.BlockSpec((tm, LANES), lambda i: (i % tps, 0)),
            pl.BlockSpec((tm, LANES), lambda i: (i % tps, 0)),
            pl.BlockSpec((None,) + w_all.shape[1:], lambda i: (layer, 0, 0)),
        ],
        out_specs=[
            pl.BlockSpec((1, N_HEADS, tm // tq, V_DIM, tq), lambda i: (i // tps, 0, i % tps, 0, 0)),
            pl.BlockSpec((1, N_HEADS, tm, V_DIM), lambda i: (i // tps, 0, i % tps, 0)),
            pl.BlockSpec((1, N_HEADS, tm // tk, V_DIM + V_PAD, tk), lambda i: (i // tps, 0, i % tps, 0, 0)),
            pl.BlockSpec((tm, d_u), tok),
            pl.BlockSpec((tm, d), tok),
            pl.BlockSpec((tm, d), tok),
        ],
        out_shape=[
            bf(batch, N_HEADS, seq_len // tq, V_DIM, tq),
            bf(batch, N_HEADS, seq_len, V_DIM),
            bf(batch, N_HEADS, seq_len // tk, V_DIM + V_PAD, tk),
            bf(n, d_u), bf(n, d), bf(n, d),
        ],
        compiler_params=_cparams("parallel"),
        name="in_projection",
    )(x, mod, g, cos, sin, w_all)


def _attn_kernel(qt_ref, k_ref, vt_ref, par_ref, o_ref, *scratch, lam_init):
    n_chunks, _, tk = vt_ref.shape[2:]
    n_blocks, _, tq = qt_ref.shape[2:]
    per_set = len(scratch) // 2
    sets = (scratch[:per_set], scratch[per_set:])

    lp = par_ref[0:4, :]
    lam = (jnp.exp(jnp.sum(lp[0:1] * lp[1:2], axis=-1, keepdims=True))
           - jnp.exp(jnp.sum(lp[2:3] * lp[3:4], axis=-1, keepdims=True)) + lam_init)
    gain = par_ref[8:9, :] * (1.0 - lam_init)

    def q_operand(j):
        qt = qt_ref[0, 0, j].astype(F32)
        row = lax.broadcasted_iota(jnp.int32, qt.shape, 0)
        return jnp.concatenate([jnp.where(row < HEAD_DIM, qt, 0.0), jnp.where(row >= HEAD_DIM, qt, 0.0)],
                               axis=1).astype(BF16)

    def scores(qtb, c):
        return _dot(k_ref[0, 0, c * tk:(c + 1) * tk, :], qtb)

    def softmax(st, slot, m_old):
        s = st[slot][...]
        m_new = jnp.maximum(m_old, jnp.max(s, axis=0, keepdims=True))
        st[2 + slot][...] = jnp.exp2(s - m_new).astype(BF16)
        return m_new, jnp.exp2(m_old - m_new)

    def attend(st, c, slot, alpha):
        st[4][...] = alpha * st[4][...] + _dot(vt_ref[0, 0, c], st[2 + slot][...])

    def head(st, j):
        qtb = q_operand(j)
        st[4][...] = jnp.zeros(st[4].shape, F32)
        st[0][...] = scores(qtb, 0)
        st[1][...] = scores(qtb, 1)
        return (qtb,) + softmax(st, 0, jnp.full((1, 2 * tq), -jnp.inf, F32))

    def steady(st, carry):
        qtb, m, alpha = carry
        for c in range(1, n_chunks - 1):
            slot = c % 2
            st[1 - slot][...] = scores(qtb, c + 1)
            m, alpha_new = softmax(st, slot, m)
            attend(st, c - 1, 1 - slot, alpha)
            alpha = alpha_new
        return m, alpha

    def tail(st, j, m, alpha):
        last = (n_chunks - 1) % 2
        _, alpha_last = softmax(st, last, m)
        attend(st, n_chunks - 2, 1 - last, alpha)
        attend(st, n_chunks - 1, last, alpha_last)
        acc = st[4][...]
        acc = acc[:V_DIM] / acc[V_DIM:V_DIM + 1]
        o = (acc[:, :tq] - lam * acc[:, tq:]).T
        o_ref[0, pl.ds(pl.multiple_of(j * tq, tq), tq), :] = (_rms(o) * gain).astype(o_ref.dtype)

    def body(jj, carry):
        j0 = 2 * jj
        m0, a0 = steady(sets[0], carry)
        carry1 = head(sets[1], j0 + 1)
        tail(sets[0], j0, m0, a0)
        m1, a1 = steady(sets[1], carry1)
        nxt = head(sets[0], jnp.minimum(j0 + 2, n_blocks - 1))
        tail(sets[1], j0 + 1, m1, a1)
        return nxt

    lax.fori_loop(0, n_blocks // 2, body, head(sets[0], 0))


def _diff_attention(qt, k, vt, lam_params, subln_g, lam_init):
    par = jnp.zeros((ATTN_PAR_ROWS, LANES), F32)
    par = par.at[0:4, 0:lam_params.shape[1]].set(lam_params).at[8, :].set(subln_g[0])
    b, n_heads, l, _ = k.shape
    n_blocks, _, tq = qt.shape[2:]
    n_chunks, v_rows, tk = vt.shape[2:]
    assert n_blocks % 2 == 0 and n_chunks >= 2
    one_set = ([pltpu.VMEM((tk, 2 * tq), F32)] * 2 + [pltpu.VMEM((tk, 2 * tq), BF16)] * 2
               + [pltpu.VMEM((v_rows, 2 * tq), F32)])
    whole = lambda a: pl.BlockSpec((1, 1) + a.shape[2:], lambda bi, h: (bi, h) + (0,) * (a.ndim - 2))
    return pl.pallas_call(
        functools.partial(_attn_kernel, lam_init=lam_init),
        grid=(b, n_heads),
        in_specs=[
            whole(qt), whole(k), whole(vt),
            pl.BlockSpec(par.shape, lambda bi, h: (0, 0)),
        ],
        out_specs=pl.BlockSpec((1, l, V_DIM), lambda bi, h: (bi, 0, h)),
        out_shape=jax.ShapeDtypeStruct((b, l, n_heads * V_DIM), BF16),
        scratch_shapes=one_set * 2,
        compiler_params=_cparams("parallel", "parallel"),
        name="diff_attention",
    )(qt, k, vt, par)


def _ssm_prep_kernel(arow_ref, acol_ref, bt_ref, ct_ref, m_ref, win_ref, wout_ref, dec_ref, *, chunk):
    t_len = chunk
    w = t_len * GROUP_CH
    w2 = 2 * w
    arow = arow_ref[0, 0]
    acol = acol_ref[0, 0]
    fwd_lanes = lax.broadcasted_iota(jnp.int32, (1, LANES), 1) < N_STATE

    ar = arow[0:1]
    ai = arow[1:2]
    dt = jnp.exp(arow[2:3])
    zr = dt * ar
    zi = dt * ai
    mag = jnp.exp(zr)
    nr = mag * jnp.cos(zi) - 1.0
    ni = mag * jnp.sin(zi)
    den = ar * ar + ai * ai
    fr = (nr * ar + ni * ai) / den
    fi = (ni * ar - nr * ai) / den
    br = bt_ref[0, 0, 0]
    bi = bt_ref[0, 0, 1]
    bbr = fr * br - fi * bi
    bbi = fr * bi + fi * br

    s_of_row = lax.shift_right_logical(lax.broadcasted_iota(jnp.int32, (w, t_len), 0), 4)
    j_of_col = lax.broadcasted_iota(jnp.int32, (w, t_len), 1)
    rep_fwd = _onehot(j_of_col == (t_len - 1 - s_of_row))
    rep_bwd = _onehot(j_of_col == s_of_row)
    n_rows = lax.broadcasted_iota(jnp.int32, (t_len, 1), 0).astype(F32)
    pm = jnp.exp(n_rows * zr)

    def expand(p):
        return jnp.where(fwd_lanes, _select_rows(rep_fwd, p), _select_rows(rep_bwd, p))

    e_re = expand(pm * jnp.cos(n_rows * zi))
    e_im = expand(pm * jnp.sin(n_rows * zi))
    b_re = jnp.tile(bbr, (t_len, 1))
    b_im = jnp.tile(bbi, (t_len, 1))
    win_ref[0, 0, :, :LANES] = (e_re * b_re - e_im * b_im).astype(win_ref.dtype)
    win_ref[0, 0, :, LANES:] = (e_re * b_im + e_im * b_re).astype(win_ref.dtype)
    dm = jnp.exp(t_len * zr)
    dec_ref[0, 0, 0:1, :] = dm * jnp.cos(t_len * zi)
    dec_ref[0, 0, 1:2, :] = dm * jnp.sin(t_len * zi)

    def lane_maps(width):
        lane = lax.broadcasted_iota(jnp.int32, (LANES, width), 1)
        return (lax.shift_right_logical(lane, 4), jnp.bitwise_and(lane, GROUP_CH - 1),
                lax.broadcasted_iota(jnp.int32, (LANES, width), 0))

    lag_idx, ch_idx, jrow = lane_maps(w2)
    lag_idx_w, _, jrow_w = lane_maps(w)
    tile_ch = _onehot(jrow == ch_idx)
    n_lanes = jnp.minimum(lax.broadcasted_iota(jnp.int32, (1, LANES), 1), t_len).astype(F32)

    lag_tables = []
    out_tables = []
    for d in range(2):
        dtc = jnp.exp(acol[:, 4 + d:5 + d])
        zrc = dtc * acol[:, d:d + 1]
        zic = dtc * acol[:, 2 + d:3 + d]
        ptm = jnp.exp(zrc * n_lanes)
        pt_re = ptm * jnp.cos(zic * n_lanes)
        pt_im = ptm * jnp.sin(zic * n_lanes)
        c_re = _select_cols(ct_ref[0, 0, d], tile_ch)
        c_im = _select_cols(ct_ref[0, 0, 2 + d], tile_ch)

        def table(rep):
            width = rep.shape[1]
            p_re = _select_cols(pt_re, rep)
            p_im = _select_cols(pt_im, rep)
            cr = c_re[:, :width]
            ci = c_im[:, :width]
            return p_re * cr - p_im * ci, p_re * ci + p_im * cr

        if d == 0:
            power = lag_idx - (t_len - 1)
        else:
            power = (t_len - 1) - lag_idx
        power = jnp.where(power >= 0, power, -1)
        f_re, f_im = table(_onehot(jrow == power))
        lag_tables.append((f_re, f_im))
        if d == 0:
            out_tables.append((f_re[:, w:], f_im[:, w:]))
        else:
            out_tables.append(table(_onehot(jrow_w == (t_len - lag_idx_w))))

    for i, o in enumerate((out_tables[0][0], out_tables[1][0], -out_tables[0][1], -out_tables[1][1])):
        wout_ref[0, 0, i * N_STATE:(i + 1) * N_STATE, :] = o.astype(wout_ref.dtype)

    lhs = jnp.concatenate([bbr, -bbi], axis=1)
    rhs = jnp.concatenate([lag_tables[0][0], lag_tables[1][0], lag_tables[0][1], lag_tables[1][1]], axis=0)
    strip = _dot_f32(lhs, rhs)

    per_tile = LANES // GROUP_CH
    for r in range(per_tile):
        rolled = strip if r == 0 else pltpu.roll(strip, w2 - r * GROUP_CH, 1)
        for s in range(t_len):
            if (t_len - 1 - s) % per_tile == r:
                q = (t_len - 1 - s) // per_tile
                m_ref[0, 0, s * GROUP_CH:(s + 1) * GROUP_CH, :] = (
                    rolled[:, q * LANES:q * LANES + w].astype(m_ref.dtype))


def _ssm_prep(a_re, a_im, log_dt, b_re, b_im, c_re, c_im, chunk):
    depth, _, n_groups, n_state = a_re.shape
    w = chunk * GROUP_CH
    gd = lambda x: jnp.swapaxes(x, 1, 2)
    ldt = jnp.broadcast_to(gd(log_dt)[..., None], (depth, n_groups, 2, n_state))
    packed = lambda x: x.reshape(depth, n_groups, 1, 2 * n_state)
    arow = jnp.concatenate([packed(gd(a_re)), packed(gd(a_im)), packed(ldt),
                            jnp.zeros((depth, n_groups, 5, 2 * n_state), F32)], axis=2)
    acol = jnp.swapaxes(jnp.concatenate([gd(a_re), gd(a_im), ldt, jnp.zeros_like(ldt)], axis=2), 2, 3)
    bt = jnp.stack([gd(b_re), gd(b_im)], axis=2)
    bt = bt.transpose(0, 1, 2, 5, 3, 4).reshape(depth, n_groups, 2, GROUP_CH, 2 * n_state)
    ct = jnp.swapaxes(jnp.concatenate([gd(c_re), gd(c_im)], axis=2), 3, 4)
    ct = jnp.pad(ct, ((0, 0),) * 4 + ((0, LANES - GROUP_CH),))
    blk = lambda *tail: pl.BlockSpec((1, 1) + tail, lambda l, g: (l, g) + (0,) * len(tail))
    return pl.pallas_call(
        functools.partial(_ssm_prep_kernel, chunk=chunk),
        grid=(depth, n_groups),
        in_specs=[blk(8, LANES), blk(n_state, 8), blk(2, GROUP_CH, LANES), blk(4, n_state, LANES)],
        out_specs=[blk(w, w), blk(w, 4 * n_state), blk(4 * n_state, w), blk(2, LANES)],
        out_shape=[
            jax.ShapeDtypeStruct((depth, n_groups, w, w), BF16),
            jax.ShapeDtypeStruct((depth, n_groups, w, 4 * n_state), BF16),
            jax.ShapeDtypeStruct((depth, n_groups, 4 * n_state, w), BF16),
            jax.ShapeDtypeStruct((depth, n_groups, 2, LANES), F32),
        ],
        compiler_params=_cparams("parallel", "parallel"),
        name="ssm_prep",
    )(arow, acol, bt, ct)


def _ssm_state_kernel(u_ref, win_ref, sre_ref, sim_ref):
    s = _dot(u_ref[0], win_ref[0, 0])
    sre_ref[0] = s[:, :LANES]
    sim_ref[0] = s[:, LANES:]


def _ssm_out_kernel(u_ref, hfr_ref, hfi_ref, hbr_ref, hbi_ref, m_ref, wout_ref, y_ref):
    fwd = lax.broadcasted_iota(jnp.int32, hfr_ref.shape[1:], 1) < N_STATE
    h = jnp.concatenate([jnp.where(fwd, hfr_ref[0], hbr_ref[0]),
                         jnp.where(fwd, hfi_ref[0], hbi_ref[0])], axis=1).astype(BF16)
    y_ref[0] = (_dot(u_ref[0], m_ref[0, 0]) + _dot(h, wout_ref[0, 0])).astype(y_ref.dtype)


def _ssm_scan_kernel(sre_ref, sim_ref, dre_ref, dim_ref, hfr_ref, hfi_ref, hbr_ref, hbi_ref,
                     *, batch, n_chunks):
    n_blk, _, lanes = sre_ref.shape
    fwd = jnp.bitwise_and(lax.broadcasted_iota(jnp.int32, (batch, lanes), 1), N_STATE) == 0

    def step(i, carry):
        rows_f = pl.ds(i, batch, stride=n_chunks)
        rows_b = pl.ds(n_chunks - 1 - i, batch, stride=n_chunks)
        out = []
        for q in range(n_blk):
            xr, xi = carry[2 * q], carry[2 * q + 1]
            hfr_ref[q, rows_f, :] = xr
            hfi_ref[q, rows_f, :] = xi
            hbr_ref[q, rows_b, :] = xr
            hbi_ref[q, rows_b, :] = xi
            sr = jnp.where(fwd, sre_ref[q, rows_f, :], sre_ref[q, rows_b, :])
            si = jnp.where(fwd, sim_ref[q, rows_f, :], sim_ref[q, rows_b, :])
            dr = dre_ref[q]
            di = dim_ref[q]
            out += [dr * xr - di * xi + sr, dr * xi + di * xr + si]
        return tuple(out)

    zeros = jnp.zeros((batch, lanes), F32)
    lax.fori_loop(0, n_chunks, step, (zeros,) * (2 * n_blk))


def _to_groups_kernel(u_ref, ug_ref, scr_ref, *, chunk):
    tm, d_u = u_ref.shape
    n_ch = tm // chunk
    per_tile = LANES // GROUP_CH
    for j in range(d_u // LANES):
        scr_ref[j] = u_ref[:, j * LANES:(j + 1) * LANES].astype(F32)
    for s in range(chunk):
        dst = (s % per_tile) * GROUP_CH
        for j in range(d_u // LANES):
            slab = scr_ref[j, pl.ds(s, n_ch, stride=chunk), :]
            for gg in range(per_tile):
                shift = (dst - gg * GROUP_CH) % LANES
                moved = (pltpu.roll(slab, shift, 1) if shift else slab).astype(ug_ref.dtype)
                ug_ref[j * per_tile + gg, :, s * GROUP_CH:(s + 1) * GROUP_CH] = moved[:, dst:dst + GROUP_CH]


def _from_groups_kernel(yg_ref, y_ref, scr_ref, *, chunk):
    tm, d_u = y_ref.shape
    n_ch = tm // chunk
    per_tile = LANES // GROUP_CH
    group_of_lane = lax.shift_right_logical(lax.broadcasted_iota(jnp.int32, (n_ch, LANES), 1), 4)
    for s in range(chunk):
        src = (s % per_tile) * GROUP_CH
        base = (s // per_tile) * LANES
        for j in range(d_u // LANES):
            slab = jnp.zeros((n_ch, LANES), F32)
            for gg in range(per_tile):
                v = yg_ref[j * per_tile + gg, :, base:base + LANES].astype(F32)
                shift = (gg * GROUP_CH - src) % LANES
                v = pltpu.roll(v, shift, 1) if shift else v
                slab = jnp.where(group_of_lane == gg, v, slab)
            scr_ref[j, pl.ds(s, n_ch, stride=chunk), :] = slab
    for j in range(d_u // LANES):
        y_ref[:, j * LANES:(j + 1) * LANES] = scr_ref[j].astype(y_ref.dtype)


def _group_relayout(x, to_groups, batch, seq_len, chunk, d_u):
    n_groups = d_u // GROUP_CH
    w = chunk * GROUP_CH
    n_chunks = seq_len // chunk
    tm = min(RELAYOUT_TILE, seq_len)
    tps = seq_len // tm
    tok_spec = pl.BlockSpec((tm, d_u), lambda i: (i, 0))
    grp_spec = pl.BlockSpec((n_groups, tm // chunk, w), lambda i: (0, i, 0))
    grp_shape = (n_groups, batch * n_chunks, w)
    scratch = [pltpu.VMEM((d_u // LANES, tm, LANES), F32)]
    if to_groups:
        return pl.pallas_call(
            functools.partial(_to_groups_kernel, chunk=chunk),
            grid=(batch * tps,), in_specs=[tok_spec], out_specs=grp_spec,
            out_shape=jax.ShapeDtypeStruct(grp_shape, x.dtype), scratch_shapes=scratch,
            compiler_params=_cparams("parallel"), name="ssm_to_groups",
        )(x)
    return pl.pallas_call(
        functools.partial(_from_groups_kernel, chunk=chunk),
        grid=(batch * tps,), in_specs=[grp_spec], out_specs=tok_spec,
        out_shape=jax.ShapeDtypeStruct((batch * seq_len, d_u), x.dtype), scratch_shapes=scratch,
        compiler_params=_cparams("parallel"), name="ssm_from_groups",
    )(x)


def _ssm_branch(u, layer, m_all, win_all, wout_all, dec_all, batch, seq_len, chunk):
    n, d_u = u.shape
    n_groups = d_u // GROUP_CH
    w = chunk * GROUP_CH
    n_chunks = seq_len // chunk
    rows = batch * n_chunks
    st = 4 * N_STATE

    ug = _group_relayout(u, True, batch, seq_len, chunk, d_u)
    state = jax.ShapeDtypeStruct((n_groups, rows, LANES), F32)
    one = lambda g: (g, 0, 0)
    sre, sim = pl.pallas_call(
        _ssm_state_kernel,
        grid=(n_groups,),
        in_specs=[pl.BlockSpec((1, rows, w), one),
                  pl.BlockSpec((1, 1, w, st), lambda g: (layer, g, 0, 0))],
        out_specs=[pl.BlockSpec((1, rows, LANES), one)] * 2,
        out_shape=[state] * 2,
        compiler_params=_cparams("parallel"),
        name="ssm_chunk_state",
    )(ug, win_all)

    dec = dec_all[layer]
    gb = min(SCAN_GROUPS, n_groups)
    hs = pl.pallas_call(
        functools.partial(_ssm_scan_kernel, batch=batch, n_chunks=n_chunks),
        grid=(n_groups // gb,),
        in_specs=[pl.BlockSpec((gb, rows, LANES), one)] * 2 + [pl.BlockSpec((gb, 1, LANES), one)] * 2,
        out_specs=[pl.BlockSpec((gb, rows, LANES), one)] * 4,
        out_shape=[state] * 4,
        compiler_params=_cparams("parallel"),
        name="ssm_chunk_scan",
    )(sre, sim, dec[:, 0:1, :], dec[:, 1:2, :])

    y = pl.pallas_call(
        _ssm_out_kernel,
        grid=(n_groups,),
        in_specs=[pl.BlockSpec((1, rows, w), one)]
                 + [pl.BlockSpec((1, rows, LANES), one)] * 4
                 + [pl.BlockSpec((1, 1, w, w), lambda g: (layer, g, 0, 0)),
                    pl.BlockSpec((1, 1, st, w), lambda g: (layer, g, 0, 0))],
        out_specs=pl.BlockSpec((1, rows, w), one),
        out_shape=jax.ShapeDtypeStruct((n_groups, rows, w), BF16),
        compiler_params=_cparams("parallel"),
        name="ssm_chunk_output",
    )(ug, *hs, m_all, wout_all)
    return _group_relayout(y, False, batch, seq_len, chunk, d_u)


def _merge_kernel(x_ref, mod_ref, o_ref, y_ref, u_ref, ga_ref, gs_ref, d_ref,
                  wglu_ref, bglu_ref, wattn_ref, wo_ref, out_ref):
    d = x_ref.shape[1]
    y = y_ref[...] + d_ref[...] * u_ref[...].astype(F32)
    z = jax.nn.gelu(y, approximate=True).astype(BF16)
    glu = _dot(z, wglu_ref[...]) + bglu_ref[...]
    y_ssm = glu[:, :d] * jax.nn.sigmoid(glu[:, d:])
    y_attn = _dot(o_ref[...], wattn_ref[...])
    merged = (ga_ref[...].astype(F32) * y_attn + gs_ref[...].astype(F32) * y_ssm).astype(BF16)
    out_ref[...] = x_ref[...] + mod_ref[0][2:3] * _dot(merged, wo_ref[...])


def _merge(x, mod, o, y, u, ga, gs, ssm_d, w_glu, b_glu, w_attn, w_o, layer, seq_len, in_place):
    n, d = x.shape
    tm = min(PROJ_TILE, seq_len)
    tps = seq_len // tm
    tok = lambda i: (i, 0)
    full = lambda a: pl.BlockSpec(a.shape, lambda i: (0, 0))
    of_layer = lambda a: pl.BlockSpec((None,) + a.shape[1:], lambda i: (layer, 0, 0))
    return pl.pallas_call(
        _merge_kernel,
        grid=(n // tm,),
        in_specs=[
            pl.BlockSpec((tm, d), tok),
            pl.BlockSpec((1, 6, d), lambda i: (i // tps, 0, 0)),
            pl.BlockSpec((tm, o.shape[1]), tok),
            pl.BlockSpec((tm, y.shape[1]), tok),
            pl.BlockSpec((tm, u.shape[1]), tok),
            pl.BlockSpec((tm, d), tok),
            pl.BlockSpec((tm, d), tok),
            full(ssm_d), of_layer(w_glu), full(b_glu), of_layer(w_attn), of_layer(w_o),
        ],
        out_specs=pl.BlockSpec((tm, d), tok),
        out_shape=jax.ShapeDtypeStruct((n, d), F32),
        input_output_aliases={0: 0} if in_place else {},
        compiler_params=_cparams("parallel"),
        name="merge_out_projection",
    )(x, mod, o, y, u, ga, gs, ssm_d, w_glu, b_glu, w_attn, w_o)


def _ffn_kernel(x_ref, mod_ref, g_ref, win_ref, wd_ref, fg_ref, out_ref, h_ref, t_ref, *, final_norm):
    mod = mod_ref[0]
    d_ff = wd_ref.shape[0]
    h_ref[...] = (_rms(x_ref[...]) * g_ref[...] * (1.0 + mod[4:5]) + mod[3:4]).astype(BF16)
    for c in range(d_ff // FFN_CHUNK):
        cols = slice(c * FFN_CHUNK, (c + 1) * FFN_CHUNK)
        gate = _dot(h_ref[...], win_ref[:, cols])
        up = _dot(h_ref[...], win_ref[:, d_ff + c * FFN_CHUNK:d_ff + (c + 1) * FFN_CHUNK])
        t_ref[:, cols] = (gate * jax.nn.sigmoid(gate) * up).astype(BF16)
    xn = x_ref[...] + mod[5:6] * _dot(t_ref[...], wd_ref[...])
    if final_norm:
        xn = _rms(xn) * fg_ref[...]
    out_ref[...] = xn


def _ffn(x, mod, g, w_in, w_out, final_g, layer, seq_len, final_norm):
    n, d = x.shape
    d_ff = w_out.shape[1]
    assert d_ff % FFN_CHUNK == 0
    tm = min(TOKEN_TILE, seq_len)
    tps = seq_len // tm
    return pl.pallas_call(
        functools.partial(_ffn_kernel, final_norm=final_norm),
        grid=(n // tm,),
        in_specs=[
            pl.BlockSpec((tm, d), lambda i: (i, 0)),
            pl.BlockSpec((1, 6, d), lambda i: (i // tps, 0, 0)),
            pl.BlockSpec((1, d), lambda i: (0, 0)),
            pl.BlockSpec((None, d, 2 * d_ff), lambda i: (layer, 0, 0)),
            pl.BlockSpec((None, d_ff, d), lambda i: (layer, 0, 0)),
            pl.BlockSpec((1, d), lambda i: (0, 0)),
        ],
        out_specs=pl.BlockSpec((tm, d), lambda i: (i, 0)),
        out_shape=jax.ShapeDtypeStruct((n, d), F32),
        scratch_shapes=[pltpu.VMEM((tm, d), BF16), pltpu.VMEM((tm, d_ff), BF16)],
        input_output_aliases={0: 0},
        compiler_params=_cparams("parallel"),
        name="ffn",
    )(x, mod, g, w_in, w_out, final_g)


def _rope_tables(seq_len):
    inv = 1.0 / (ROPE_THETA ** (jnp.arange(0, HEAD_DIM, 2, dtype=F32) / HEAD_DIM))
    ang = jnp.arange(seq_len, dtype=F32)[:, None] * inv[None, :]
    cos = jnp.cos(ang)
    sin = jnp.sin(ang)
    reps = LANES // HEAD_DIM
    cos_t = jnp.tile(jnp.concatenate([cos, cos], axis=-1), (1, reps))
    sin_t = jnp.tile(jnp.concatenate([-sin, sin], axis=-1), (1, reps))
    return cos_t, sin_t


def _lambda_init(layer):
    return 0.8 - 0.6 * math.exp(-0.3 * layer)


def _trunk(x, mod_all, p, ssm_ops):
    batch, seq_len, d = x.shape
    depth = p["w_in"].shape[0]
    d_v = N_HEADS * V_DIM
    d_u = p["ssm_d"].shape[1]
    cos, sin = _rope_tables(seq_len)
    x = x.reshape(batch * seq_len, d)
    row = lambda a: a.reshape(1, -1)
    for i in range(depth):
        mod = mod_all[i].reshape(batch, 6, d)
        qt, k, vt, u, ga, gs = _in_projection(x, mod, row(p["norm1_g"][i]), cos, sin, p["w_in"], i,
                                              batch, seq_len, d_u)
        lam_params = jnp.stack([p["lam_q1"][i], p["lam_k1"][i], p["lam_q2"][i], p["lam_k2"][i]])
        o = _diff_attention(qt, k, vt, lam_params, row(p["subln_g"][i]), _lambda_init(i))
        y = _ssm_branch(u, i, *ssm_ops, batch, seq_len, SSM_CHUNK)
        x = _merge(x, mod, o.reshape(batch * seq_len, d_v), y, u, ga, gs, row(p["ssm_d"][i]),
                   p["w_glu"], row(p["b_glu"][i]), p["w_attn_br"], p["w_o"], i, seq_len,
                   in_place=(i > 0))
        x = _ffn(x, mod, row(p["norm2_g"][i]), p["w_ffn_in"], p["w_ffn_out"],
                 row(p["final_g"]), i, seq_len, final_norm=(i == depth - 1))
    return x.reshape(batch, seq_len, d)


def kernel(x_prompt, x_sample, c_prompt, c_sample, w_mod, b_mod, norm1_g, w_in, lam_q1, lam_k1, lam_q2, lam_k2, subln_g, w_attn_br, ssm_a_re, ssm_a_im, ssm_log_dt, ssm_b_re, ssm_b_im, ssm_c_re, ssm_c_im, ssm_d, w_glu, b_glu, w_o, norm2_g, w_ffn_in, w_ffn_out, final_g):
    bp, bs = c_prompt.shape[0], c_sample.shape[0]
    pad = -(bp + bs) % 8
    c_all = jnp.concatenate([c_prompt, c_sample, jnp.zeros((pad, c_prompt.shape[1]), F32)], axis=0)
    mod_all = _modulation(c_all, w_mod, b_mod)
    ssm_ops = _ssm_prep(ssm_a_re, ssm_a_im, ssm_log_dt, ssm_b_re, ssm_b_im, ssm_c_re, ssm_c_im, SSM_CHUNK)
    p = dict(
        norm1_g=norm1_g, w_in=w_in.astype(BF16), lam_q1=lam_q1, lam_k1=lam_k1, lam_q2=lam_q2, lam_k2=lam_k2,
        subln_g=subln_g, w_attn_br=w_attn_br.astype(BF16), ssm_d=ssm_d, w_glu=w_glu.astype(BF16),
        b_glu=b_glu, w_o=w_o.astype(BF16), norm2_g=norm2_g, w_ffn_in=w_ffn_in.astype(BF16),
        w_ffn_out=w_ffn_out.astype(BF16), final_g=final_g)
    y_prompt = _trunk(x_prompt, mod_all[:, :bp], p, ssm_ops)
    y_sample = _trunk(x_sample, mod_all[:, bp:bp + bs], p, ssm_ops)
    return (y_prompt, y_sample)
```

```python
import functools
import math

import jax
import jax.numpy as jnp
from jax import lax
from jax.experimental import pallas as pl
from jax.experimental.pallas import tpu as pltpu

F32 = jnp.float32
BF16 = jnp.bfloat16

N_HEADS = 4
HEAD_DIM = 64
V_DIM = 2 * HEAD_DIM
GROUP_CH = 16
N_STATE = 64
ROPE_THETA = 10000.0
EPS = 1e-6
Q_SCALE = HEAD_DIM ** -0.5 * math.log2(math.e)
V_PAD = 16

VMEM_LIMIT_BYTES = 56 * 1024 * 1024
LANES = 128

TOKEN_TILE = 512
PROJ_TILE = 1024
ATTN_TQ = 256
ATTN_TK = 256
ATTN_MAX_UNROLLED_CHUNKS = 64
SSM_CHUNK = 64
FFN_CHUNK = 256
SCAN_GROUPS = 4
RELAYOUT_TILE = 1024


def _cparams(*sem):
    return pltpu.CompilerParams(dimension_semantics=sem, vmem_limit_bytes=VMEM_LIMIT_BYTES)


def _dot(a, b):
    return jnp.dot(a, b, preferred_element_type=F32)


def _split3(x):
    hi = x.astype(BF16)
    r1 = x - hi.astype(F32)
    mid = r1.astype(BF16)
    lo = (r1 - mid.astype(F32)).astype(BF16)
    return hi, mid, lo


def _dot_f32(a, b):
    a_hi = a.astype(BF16)
    a_lo = (a - a_hi.astype(F32)).astype(BF16)
    b_hi = b.astype(BF16)
    b_lo = (b - b_hi.astype(F32)).astype(BF16)
    return _dot(a_hi, b_hi) + (_dot(a_hi, b_lo) + _dot(a_lo, b_hi))


def _select_cols(x, sel):
    hi, mid, lo = _split3(x)
    return _dot(hi, sel) + (_dot(mid, sel) + _dot(lo, sel))


def _select_rows(sel, x):
    hi, mid, lo = _split3(x)
    return _dot(sel, hi) + (_dot(sel, mid) + _dot(sel, lo))


def _onehot(cond):
    return jnp.where(cond, 1.0, 0.0).astype(BF16)


def _rms(x):
    return x * lax.rsqrt(jnp.mean(x * x, axis=-1, keepdims=True) + EPS)


def _mod_kernel(c_ref, w_ref, b_ref, o_ref):
    c = c_ref[...]
    s = c * jax.nn.sigmoid(c)
    o_ref[0] = _dot_f32(s, w_ref[0]) + b_ref[0]


def _modulation(c_all, w_mod, b_mod):
    depth, d, n6 = w_mod.shape
    bp = c_all.shape[0]
    tn = 1536
    return pl.pallas_call(
        _mod_kernel,
        grid=(depth, n6 // tn),
        in_specs=[
            pl.BlockSpec((bp, d), lambda l, j: (0, 0)),
            pl.BlockSpec((1, d, tn), lambda l, j: (l, 0, j)),
            pl.BlockSpec((1, 1, tn), lambda l, j: (l, 0, j)),
        ],
        out_specs=pl.BlockSpec((1, bp, tn), lambda l, j: (l, 0, j)),
        out_shape=jax.ShapeDtypeStruct((depth, bp, n6), F32),
        compiler_params=_cparams("parallel", "parallel"),
        name="modulation",
    )(c_all, w_mod, b_mod.reshape(depth, 1, n6))


def _inproj_kernel(x_ref, mod_ref, g_ref, cos_ref, sin_ref, w_ref,
                   qt_ref, k_ref, vt_ref, u_ref, ga_ref, gs_ref):
    d = x_ref.shape[1]
    n_heads = k_ref.shape[1]
    d_qk = n_heads * V_DIM
    mod = mod_ref[0]
    h = (_rms(x_ref[...]) * g_ref[...] * (1.0 + mod[1:2]) + mod[0:1]).astype(BF16)

    qk = _dot(h, w_ref[:, 0:2 * d_qk])
    reps = 2 * d_qk // LANES
    cos = jnp.tile(cos_ref[...], (1, reps))
    sin = jnp.tile(sin_ref[...], (1, reps))
    lane = lax.broadcasted_iota(jnp.int32, qk.shape, 1)
    first_half = jnp.bitwise_and(lane, HEAD_DIM // 2) == 0
    half = HEAD_DIM // 2
    rot = jnp.where(first_half,
                    pltpu.roll(qk, 2 * d_qk - half, 1),
                    pltpu.roll(qk, half, 1))
    qk = qk * cos + rot * sin
    tq = qt_ref.shape[4]
    for hd in range(n_heads):
        qh = qk[:, hd * V_DIM:(hd + 1) * V_DIM] * Q_SCALE
        for c in range(qt_ref.shape[2]):
            qt_ref[0, hd, c] = qh[c * tq:(c + 1) * tq].T.astype(BF16)
        k_ref[0, hd] = qk[:, d_qk + hd * V_DIM:d_qk + (hd + 1) * V_DIM].astype(BF16)

    off = 2 * d_qk
    d_v = n_heads * V_DIM
    d_u = u_ref.shape[1]
    vu = _dot(h, w_ref[:, off:off + d_v + d_u])
    tk = vt_ref.shape[4]
    tail_row = lax.broadcasted_iota(jnp.int32, (V_PAD, tk), 0)
    ones_row = jnp.where(tail_row == 0, 1.0, 0.0).astype(BF16)
    for hd in range(n_heads):
        for c in range(vt_ref.shape[2]):
            vt_ref[0, hd, c, :V_DIM, :] = vu[c * tk:(c + 1) * tk, hd * V_DIM:(hd + 1) * V_DIM].T.astype(BF16)
            vt_ref[0, hd, c, V_DIM:, :] = ones_row
    u_ref[...] = vu[:, d_v:].astype(BF16)
    off += d_v + d_u
    ga_ref[...] = jax.nn.sigmoid(_dot(h, w_ref[:, off:off + d])).astype(BF16)
    gs_ref[...] = jax.nn.sigmoid(_dot(h, w_ref[:, off + d:off + 2 * d])).astype(BF16)


def _in_projection(x, mod, g, cos, sin, w_all, layer, batch, seq_len, d_u):
    n, d = x.shape
    tm = min(PROJ_TILE, seq_len)
    tps = seq_len // tm
    tk = min(ATTN_TK, tm)
    tq = min(ATTN_TQ, tm)
    tok = lambda i: (i, 0)
    bf = lambda *shape: jax.ShapeDtypeStruct(shape, BF16)
    return pl.pallas_call(
        _inproj_kernel,
        grid=(n // tm,),
        in_specs=[
            pl.BlockSpec((tm, d), tok),
            pl.BlockSpec((1, 6, d), lambda i: (i // tps, 0, 0)),
            pl.BlockSpec((1, d), lambda i: (0, 0)),
            pl.BlockSpec((tm, LANES), lambda i: (i % tps, 0)),
            pl.BlockSpec((tm, LANES), lambda i: (i % tps, 0)),
            pl.BlockSpec((None,) + w_all.shape[1:], lambda i: (layer, 0, 0)),
        ],
        out_specs=[
            pl.BlockSpec((1, N_HEADS, tm // tq, V_DIM, tq), lambda i: (i // tps, 0, i % tps, 0, 0)),
            pl.BlockSpec((1, N_HEADS, tm, V_DIM), lambda i: (i // tps, 0, i % tps, 0)),
            pl.BlockSpec((1, N_HEADS, tm // tk, V_DIM + V_PAD, tk), lambda i: (i // tps, 0, i % tps, 0, 0)),
            pl.BlockSpec((tm, d_u), tok),
            pl.BlockSpec((tm, d), tok),
            pl.BlockSpec((tm, d), tok),
        ],
        out_shape=[
            bf(batch, N_HEADS, seq_len // tq, V_DIM, tq),
            bf(batch, N_HEADS, seq_len, V_DIM),
            bf(batch, N_HEADS, seq_len // tk, V_DIM + V_PAD, tk),
            bf(n, d_u), bf(n, d), bf(n, d),
        ],
        compiler_params=_cparams("parallel"),
        name="in_projection",
    )(x, mod, g, cos, sin, w_all)


def _attn_kernel(qt_ref, k_ref, vt_ref, lam_ref, g_ref, o_ref, *scratch, lam_init):
    n_chunks, _, tk = vt_ref.shape[2:]
    n_blocks, _, tq = qt_ref.shape[2:]
    per_set = len(scratch) // 2
    sets = (scratch[:per_set], scratch[per_set:])

    lp = lam_ref[...]
    lam = (jnp.exp(jnp.sum(lp[0:1] * lp[1:2], axis=-1, keepdims=True))
           - jnp.exp(jnp.sum(lp[2:3] * lp[3:4], axis=-1, keepdims=True)) + lam_init)
    gain = g_ref[...] * (1.0 - lam_init)

    def q_operand(j):
        qt = qt_ref[0, 0, j].astype(F32)
        row = lax.broadcasted_iota(jnp.int32, qt.shape, 0)
        return jnp.concatenate([jnp.where(row < HEAD_DIM, qt, 0.0), jnp.where(row >= HEAD_DIM, qt, 0.0)],
                               axis=1).astype(BF16)

    def scores(qtb, c):
        return _dot(k_ref[0, 0, c * tk:(c + 1) * tk, :], qtb)

    def softmax(st, slot, m_old):
        s = st[slot][...]
        m_new = jnp.maximum(m_old, jnp.max(s, axis=0, keepdims=True))
        st[2 + slot][...] = jnp.exp2(s - m_new).astype(BF16)
        return m_new, jnp.exp2(m_old - m_new)

    def attend(st, c, slot, alpha):
        st[4][...] = alpha * st[4][...] + _dot(vt_ref[0, 0, c], st[2 + slot][...])

    def head(st, j):
        qtb = q_operand(j)
        st[4][...] = jnp.zeros(st[4].shape, F32)
        st[0][...] = scores(qtb, 0)
        st[1][...] = scores(qtb, 1)
        return (qtb,) + softmax(st, 0, jnp.full((1, 2 * tq), -jnp.inf, F32))

    def steady(st, carry):
        qtb, m, alpha = carry
        for c in range(1, n_chunks - 1):
            slot = c % 2
            st[1 - slot][...] = scores(qtb, c + 1)
            m, alpha_new = softmax(st, slot, m)
            attend(st, c - 1, 1 - slot, alpha)
            alpha = alpha_new
        return m, alpha

    def tail(st, j, m, alpha):
        last = (n_chunks - 1) % 2
        _, alpha_last = softmax(st, last, m)
        attend(st, n_chunks - 2, 1 - last, alpha)
        attend(st, n_chunks - 1, last, alpha_last)
        acc = st[4][...]
        acc = acc[:V_DIM] / acc[V_DIM:V_DIM + 1]
        o = (acc[:, :tq] - lam * acc[:, tq:]).T
        o_ref[0, pl.ds(pl.multiple_of(j * tq, tq), tq), :] = (_rms(o) * gain).astype(o_ref.dtype)

    per_iter = 2
    while n_blocks % (2 * per_iter) == 0 and 2 * per_iter * n_chunks <= ATTN_MAX_UNROLLED_CHUNKS:
        per_iter *= 2

    def body(jj, carry):
        j0 = per_iter * jj
        for i in range(per_iter):
            cur, nxt = sets[i % 2], sets[(i + 1) % 2]
            m, alpha = steady(cur, carry)
            carry = head(nxt, jnp.minimum(j0 + i + 1, n_blocks - 1))
            tail(cur, j0 + i, m, alpha)
        return carry

    lax.fori_loop(0, n_blocks // per_iter, body, head(sets[0], 0))


def _diff_attention(qt, k, vt, lam_params, subln_g, lam_init):
    b, n_heads, l, _ = k.shape
    n_blocks, _, tq = qt.shape[2:]
    n_chunks, v_rows, tk = vt.shape[2:]
    assert n_blocks % 2 == 0 and n_chunks >= 2
    one_set = ([pltpu.VMEM((tk, 2 * tq), F32)] * 2 + [pltpu.VMEM((tk, 2 * tq), BF16)] * 2
               + [pltpu.VMEM((v_rows, 2 * tq), F32)])
    whole = lambda a: pl.BlockSpec((1, 1) + a.shape[2:], lambda bi, h: (bi, h) + (0,) * (a.ndim - 2))
    return pl.pallas_call(
        functools.partial(_attn_kernel, lam_init=lam_init),
        grid=(b, n_heads),
        in_specs=[
            whole(qt), whole(k), whole(vt),
            pl.BlockSpec(lam_params.shape, lambda bi, h: (0, 0)),
            pl.BlockSpec((1, V_DIM), lambda bi, h: (0, 0)),
        ],
        out_specs=pl.BlockSpec((1, l, V_DIM), lambda bi, h: (bi, 0, h)),
        out_shape=jax.ShapeDtypeStruct((b, l, n_heads * V_DIM), BF16),
        scratch_shapes=one_set * 2,
        compiler_params=_cparams("parallel", "parallel"),
        name="diff_attention",
    )(qt, k, vt, lam_params, subln_g)


def _ssm_prep_kernel(arow_ref, acol_ref, bt_ref, ct_ref, m_ref, win_ref, wout_ref, dec_ref, *, chunk):
    t_len = chunk
    w = t_len * GROUP_CH
    w2 = 2 * w
    arow = arow_ref[0, 0]
    acol = acol_ref[0, 0]
    fwd_lanes = lax.broadcasted_iota(jnp.int32, (1, LANES), 1) < N_STATE

    ar = arow[0:1]
    ai = arow[1:2]
    dt = jnp.exp(arow[2:3])
    zr = dt * ar
    zi = dt * ai
    mag = jnp.exp(zr)
    nr = mag * jnp.cos(zi) - 1.0
    ni = mag * jnp.sin(zi)
    den = ar * ar + ai * ai
    fr = (nr * ar + ni * ai) / den
    fi = (ni * ar - nr * ai) / den
    br = bt_ref[0, 0, 0]
    bi = bt_ref[0, 0, 1]
    bbr = fr * br - fi * bi
    bbi = fr * bi + fi * br

    s_of_row = lax.shift_right_logical(lax.broadcasted_iota(jnp.int32, (w, t_len), 0), 4)
    j_of_col = lax.broadcasted_iota(jnp.int32, (w, t_len), 1)
    rep_fwd = _onehot(j_of_col == (t_len - 1 - s_of_row))
    rep_bwd = _onehot(j_of_col == s_of_row)
    n_rows = lax.broadcasted_iota(jnp.int32, (t_len, 1), 0).astype(F32)
    pm = jnp.exp(n_rows * zr)

    def expand(p):
        return jnp.where(fwd_lanes, _select_rows(rep_fwd, p), _select_rows(rep_bwd, p))

    e_re = expand(pm * jnp.cos(n_rows * zi))
    e_im = expand(pm * jnp.sin(n_rows * zi))
    b_re = jnp.tile(bbr, (t_len, 1))
    b_im = jnp.tile(bbi, (t_len, 1))
    win_ref[0, 0, :, :LANES] = (e_re * b_re - e_im * b_im).astype(win_ref.dtype)
    win_ref[0, 0, :, LANES:] = (e_re * b_im + e_im * b_re).astype(win_ref.dtype)
    dm = jnp.exp(t_len * zr)
    dec_ref[0, 0, 0:1, :] = dm * jnp.cos(t_len * zi)
    dec_ref[0, 0, 1:2, :] = dm * jnp.sin(t_len * zi)

    def lane_maps(width):
        lane = lax.broadcasted_iota(jnp.int32, (LANES, width), 1)
        return (lax.shift_right_logical(lane, 4), jnp.bitwise_and(lane, GROUP_CH - 1),
                lax.broadcasted_iota(jnp.int32, (LANES, width), 0))

    lag_idx, ch_idx, jrow = lane_maps(w2)
    lag_idx_w, _, jrow_w = lane_maps(w)
    tile_ch = _onehot(jrow == ch_idx)
    n_lanes = jnp.minimum(lax.broadcasted_iota(jnp.int32, (1, LANES), 1), t_len).astype(F32)

    lag_tables = []
    out_tables = []
    for d in range(2):
        dtc = jnp.exp(acol[:, 4 + d:5 + d])
        zrc = dtc * acol[:, d:d + 1]
        zic = dtc * acol[:, 2 + d:3 + d]
        ptm = jnp.exp(zrc * n_lanes)
        pt_re = ptm * jnp.cos(zic * n_lanes)
        pt_im = ptm * jnp.sin(zic * n_lanes)
        c_re = _select_cols(ct_ref[0, 0, d], tile_ch)
        c_im = _select_cols(ct_ref[0, 0, 2 + d], tile_ch)

        def table(rep):
            width = rep.shape[1]
            p_re = _select_cols(pt_re, rep)
            p_im = _select_cols(pt_im, rep)
            cr = c_re[:, :width]
            ci = c_im[:, :width]
            return p_re * cr - p_im * ci, p_re * ci + p_im * cr

        if d == 0:
            power = lag_idx - (t_len - 1)
        else:
            power = (t_len - 1) - lag_idx
        power = jnp.where(power >= 0, power, -1)
        f_re, f_im = table(_onehot(jrow == power))
        lag_tables.append((f_re, f_im))
        if d == 0:
            out_tables.append((f_re[:, w:], f_im[:, w:]))
        else:
            out_tables.append(table(_onehot(jrow_w == (t_len - lag_idx_w))))

    for i, o in enumerate((out_tables[0][0], out_tables[1][0], -out_tables[0][1], -out_tables[1][1])):
        wout_ref[0, 0, i * N_STATE:(i + 1) * N_STATE, :] = o.astype(wout_ref.dtype)

    lhs = jnp.concatenate([bbr, -bbi], axis=1)
    rhs = jnp.concatenate([lag_tables[0][0], lag_tables[1][0], lag_tables[0][1], lag_tables[1][1]], axis=0)
    strip = _dot_f32(lhs, rhs)

    per_tile = LANES // GROUP_CH
    for r in range(per_tile):
        rolled = strip if r == 0 else pltpu.roll(strip, w2 - r * GROUP_CH, 1)
        for s in range(t_len):
            if (t_len - 1 - s) % per_tile == r:
                q = (t_len - 1 - s) // per_tile
                m_ref[0, 0, s * GROUP_CH:(s + 1) * GROUP_CH, :] = (
                    rolled[:, q * LANES:q * LANES + w].astype(m_ref.dtype))


def _ssm_prep(a_re, a_im, log_dt, b_re, b_im, c_re, c_im, chunk):
    depth, _, n_groups, n_state = a_re.shape
    w = chunk * GROUP_CH
    gd = lambda x: jnp.swapaxes(x, 1, 2)
    ldt = jnp.broadcast_to(gd(log_dt)[..., None], (depth, n_groups, 2, n_state))
    packed = lambda x: x.reshape(depth, n_groups, 1, 2 * n_state)
    arow = jnp.concatenate([packed(gd(a_re)), packed(gd(a_im)), packed(ldt),
                            jnp.zeros((depth, n_groups, 5, 2 * n_state), F32)], axis=2)
    acol = jnp.swapaxes(jnp.concatenate([gd(a_re), gd(a_im), ldt, jnp.zeros_like(ldt)], axis=2), 2, 3)
    bt = jnp.stack([gd(b_re), gd(b_im)], axis=2)
    bt = bt.transpose(0, 1, 2, 5, 3, 4).reshape(depth, n_groups, 2, GROUP_CH, 2 * n_state)
    ct = jnp.swapaxes(jnp.concatenate([gd(c_re), gd(c_im)], axis=2), 3, 4)
    ct = jnp.pad(ct, ((0, 0),) * 4 + ((0, LANES - GROUP_CH),))
    blk = lambda *tail: pl.BlockSpec((1, 1) + tail, lambda l, g: (l, g) + (0,) * len(tail))
    return pl.pallas_call(
        functools.partial(_ssm_prep_kernel, chunk=chunk),
        grid=(depth, n_groups),
        in_specs=[blk(8, LANES), blk(n_state, 8), blk(2, GROUP_CH, LANES), blk(4, n_state, LANES)],
        out_specs=[blk(w, w), blk(w, 4 * n_state), blk(4 * n_state, w), blk(2, LANES)],
        out_shape=[
            jax.ShapeDtypeStruct((depth, n_groups, w, w), BF16),
            jax.ShapeDtypeStruct((depth, n_groups, w, 4 * n_state), BF16),
            jax.ShapeDtypeStruct((depth, n_groups, 4 * n_state, w), BF16),
            jax.ShapeDtypeStruct((depth, n_groups, 2, LANES), F32),
        ],
        compiler_params=_cparams("parallel", "parallel"),
        name="ssm_prep",
    )(arow, acol, bt, ct)


def _ssm_state_kernel(u_ref, win_ref, sre_ref, sim_ref):
    s = _dot(u_ref[0], win_ref[0, 0])
    sre_ref[0] = s[:, :LANES]
    sim_ref[0] = s[:, LANES:]


def _ssm_out_kernel(u_ref, hfr_ref, hfi_ref, hbr_ref, hbi_ref, m_ref, wout_ref, y_ref):
    fwd = lax.broadcasted_iota(jnp.int32, hfr_ref.shape[1:], 1) < N_STATE
    h = jnp.concatenate([jnp.where(fwd, hfr_ref[0], hbr_ref[0]),
                         jnp.where(fwd, hfi_ref[0], hbi_ref[0])], axis=1).astype(BF16)
    y_ref[0] = (_dot(u_ref[0], m_ref[0, 0]) + _dot(h, wout_ref[0, 0])).astype(y_ref.dtype)


def _ssm_scan_kernel(sre_ref, sim_ref, dre_ref, dim_ref, hfr_ref, hfi_ref, hbr_ref, hbi_ref,
                     *, batch, n_chunks):
    n_blk, _, lanes = sre_ref.shape
    fwd = jnp.bitwise_and(lax.broadcasted_iota(jnp.int32, (batch, lanes), 1), N_STATE) == 0

    def step(i, carry):
        rows_f = pl.ds(i, batch, stride=n_chunks)
        rows_b = pl.ds(n_chunks - 1 - i, batch, stride=n_chunks)
        out = []
        for q in range(n_blk):
            xr, xi = carry[2 * q], carry[2 * q + 1]
            hfr_ref[q, rows_f, :] = xr
            hfi_ref[q, rows_f, :] = xi
            hbr_ref[q, rows_b, :] = xr
            hbi_ref[q, rows_b, :] = xi
            sr = jnp.where(fwd, sre_ref[q, rows_f, :], sre_ref[q, rows_b, :])
            si = jnp.where(fwd, sim_ref[q, rows_f, :], sim_ref[q, rows_b, :])
            dr = dre_ref[q]
            di = dim_ref[q]
            out += [dr * xr - di * xi + sr, dr * xi + di * xr + si]
        return tuple(out)

    zeros = jnp.zeros((batch, lanes), F32)
    lax.fori_loop(0, n_chunks, step, (zeros,) * (2 * n_blk))


def _to_groups_kernel(u_ref, ug_ref, scr_ref, *, chunk):
    tm, d_u = u_ref.shape
    n_ch = tm // chunk
    per_tile = LANES // GROUP_CH
    for j in range(d_u // LANES):
        scr_ref[j] = u_ref[:, j * LANES:(j + 1) * LANES].astype(F32)
    for s in range(chunk):
        dst = (s % per_tile) * GROUP_CH
        for j in range(d_u // LANES):
            slab = scr_ref[j, pl.ds(s, n_ch, stride=chunk), :]
            for gg in range(per_tile):
                shift = (dst - gg * GROUP_CH) % LANES
                moved = (pltpu.roll(slab, shift, 1) if shift else slab).astype(ug_ref.dtype)
                ug_ref[j * per_tile + gg, :, s * GROUP_CH:(s + 1) * GROUP_CH] = moved[:, dst:dst + GROUP_CH]


def _from_groups_kernel(yg_ref, y_ref, scr_ref, *, chunk):
    tm, d_u = y_ref.shape
    n_ch = tm // chunk
    per_tile = LANES // GROUP_CH
    group_of_lane = lax.shift_right_logical(lax.broadcasted_iota(jnp.int32, (n_ch, LANES), 1), 4)
    for s in range(chunk):
        src = (s % per_tile) * GROUP_CH
        base = (s // per_tile) * LANES
        for j in range(d_u // LANES):
            slab = jnp.zeros((n_ch, LANES), F32)
            for gg in range(per_tile):
                v = yg_ref[j * per_tile + gg, :, base:base + LANES].astype(F32)
                shift = (gg * GROUP_CH - src) % LANES
                v = pltpu.roll(v, shift, 1) if shift else v
                slab = jnp.where(group_of_lane == gg, v, slab)
            scr_ref[j, pl.ds(s, n_ch, stride=chunk), :] = slab
    for j in range(d_u // LANES):
        y_ref[:, j * LANES:(j + 1) * LANES] = scr_ref[j].astype(y_ref.dtype)


def _group_relayout(x, to_groups, batch, seq_len, chunk, d_u):
    n_groups = d_u // GROUP_CH
    w = chunk * GROUP_CH
    n_chunks = seq_len // chunk
    tm = min(RELAYOUT_TILE, seq_len)
    tps = seq_len // tm
    tok_spec = pl.BlockSpec((tm, d_u), lambda i: (i, 0))
    grp_spec = pl.BlockSpec((n_groups, tm // chunk, w), lambda i: (0, i, 0))
    grp_shape = (n_groups, batch * n_chunks, w)
    scratch = [pltpu.VMEM((d_u // LANES, tm, LANES), F32)]
    if to_groups:
        return pl.pallas_call(
            functools.partial(_to_groups_kernel, chunk=chunk),
            grid=(batch * tps,), in_specs=[tok_spec], out_specs=grp_spec,
            out_shape=jax.ShapeDtypeStruct(grp_shape, x.dtype), scratch_shapes=scratch,
            compiler_params=_cparams("parallel"), name="ssm_to_groups",
        )(x)
    return pl.pallas_call(
        functools.partial(_from_groups_kernel, chunk=chunk),
        grid=(batch * tps,), in_specs=[grp_spec], out_specs=tok_spec,
        out_shape=jax.ShapeDtypeStruct((batch * seq_len, d_u), x.dtype), scratch_shapes=scratch,
        compiler_params=_cparams("parallel"), name="ssm_from_groups",
    )(x)


def _ssm_branch(u, layer, m_all, win_all, wout_all, dec_all, batch, seq_len, chunk):
    n, d_u = u.shape
    n_groups = d_u // GROUP_CH
    w = chunk * GROUP_CH
    n_chunks = seq_len // chunk
    rows = batch * n_chunks
    st = 4 * N_STATE

    ug = _group_relayout(u, True, batch, seq_len, chunk, d_u)
    state = jax.ShapeDtypeStruct((n_groups, rows, LANES), F32)
    one = lambda g: (g, 0, 0)
    sre, sim = pl.pallas_call(
        _ssm_state_kernel,
        grid=(n_groups,),
        in_specs=[pl.BlockSpec((1, rows, w), one),
                  pl.BlockSpec((1, 1, w, st), lambda g: (layer, g, 0, 0))],
        out_specs=[pl.BlockSpec((1, rows, LANES), one)] * 2,
        out_shape=[state] * 2,
        compiler_params=_cparams("parallel"),
        name="ssm_chunk_state",
    )(ug, win_all)

    dec = dec_all[layer]
    gb = min(SCAN_GROUPS, n_groups)
    hs = pl.pallas_call(
        functools.partial(_ssm_scan_kernel, batch=batch, n_chunks=n_chunks),
        grid=(n_groups // gb,),
        in_specs=[pl.BlockSpec((gb, rows, LANES), one)] * 2 + [pl.BlockSpec((gb, 1, LANES), one)] * 2,
        out_specs=[pl.BlockSpec((gb, rows, LANES), one)] * 4,
        out_shape=[state] * 4,
        compiler_params=_cparams("parallel"),
        name="ssm_chunk_scan",
    )(sre, sim, dec[:, 0:1, :], dec[:, 1:2, :])

    y = pl.pallas_call(
        _ssm_out_kernel,
        grid=(n_groups,),
        in_specs=[pl.BlockSpec((1, rows, w), one)]
                 + [pl.BlockSpec((1, rows, LANES), one)] * 4
                 + [pl.BlockSpec((1, 1, w, w), lambda g: (layer, g, 0, 0)),
                    pl.BlockSpec((1, 1, st, w), lambda g: (layer, g, 0, 0))],
        out_specs=pl.BlockSpec((1, rows, w), one),
        out_shape=jax.ShapeDtypeStruct((n_groups, rows, w), BF16),
        compiler_params=_cparams("parallel"),
        name="ssm_chunk_output",
    )(ug, *hs, m_all, wout_all)
    return _group_relayout(y, False, batch, seq_len, chunk, d_u)


def _merge_kernel(x_ref, mod_ref, o_ref, y_ref, u_ref, ga_ref, gs_ref, d_ref,
                  wglu_ref, bglu_ref, wattn_ref, wo_ref, out_ref):
    d = x_ref.shape[1]
    y = y_ref[...] + d_ref[...] * u_ref[...].astype(F32)
    z = jax.nn.gelu(y, approximate=True).astype(BF16)
    glu = _dot(z, wglu_ref[...]) + bglu_ref[...]
    y_ssm = glu[:, :d] * jax.nn.sigmoid(glu[:, d:])
    y_attn = _dot(o_ref[...], wattn_ref[...])
    merged = (ga_ref[...].astype(F32) * y_attn + gs_ref[...].astype(F32) * y_ssm).astype(BF16)
    out_ref[...] = x_ref[...] + mod_ref[0][2:3] * _dot(merged, wo_ref[...])


def _merge(x, mod, o, y, u, ga, gs, ssm_d, w_glu, b_glu, w_attn, w_o, layer, seq_len, in_place):
    n, d = x.shape
    tm = min(PROJ_TILE, seq_len)
    tps = seq_len // tm
    tok = lambda i: (i, 0)
    full = lambda a: pl.BlockSpec(a.shape, lambda i: (0, 0))
    of_layer = lambda a: pl.BlockSpec((None,) + a.shape[1:], lambda i: (layer, 0, 0))
    return pl.pallas_call(
        _merge_kernel,
        grid=(n // tm,),
        in_specs=[
            pl.BlockSpec((tm, d), tok),
            pl.BlockSpec((1, 6, d), lambda i: (i // tps, 0, 0)),
            pl.BlockSpec((tm, o.shape[1]), tok),
            pl.BlockSpec((tm, y.shape[1]), tok),
            pl.BlockSpec((tm, u.shape[1]), tok),
            pl.BlockSpec((tm, d), tok),
            pl.BlockSpec((tm, d), tok),
            full(ssm_d), of_layer(w_glu), full(b_glu), of_layer(w_attn), of_layer(w_o),
        ],
        out_specs=pl.BlockSpec((tm, d), tok),
        out_shape=jax.ShapeDtypeStruct((n, d), F32),
        input_output_aliases={0: 0} if in_place else {},
        compiler_params=_cparams("parallel"),
        name="merge_out_projection",
    )(x, mod, o, y, u, ga, gs, ssm_d, w_glu, b_glu, w_attn, w_o)


def _ffn_kernel(x_ref, mod_ref, g_ref, win_ref, wd_ref, fg_ref, out_ref, h_ref, t_ref, *, final_norm):
    mod = mod_ref[0]
    d_ff = wd_ref.shape[0]
    h_ref[...] = (_rms(x_ref[...]) * g_ref[...] * (1.0 + mod[4:5]) + mod[3:4]).astype(BF16)
    for c in range(d_ff // FFN_CHUNK):
        cols = slice(c * FFN_CHUNK, (c + 1) * FFN_CHUNK)
        gate = _dot(h_ref[...], win_ref[:, cols])
        up = _dot(h_ref[...], win_ref[:, d_ff + c * FFN_CHUNK:d_ff + (c + 1) * FFN_CHUNK])
        t_ref[:, cols] = (gate * jax.nn.sigmoid(gate) * up).astype(BF16)
    xn = x_ref[...] + mod[5:6] * _dot(t_ref[...], wd_ref[...])
    if final_norm:
        xn = _rms(xn) * fg_ref[...]
    out_ref[...] = xn


def _ffn(x, mod, g, w_in, w_out, final_g, layer, seq_len, final_norm):
    n, d = x.shape
    d_ff = w_out.shape[1]
    assert d_ff % FFN_CHUNK == 0
    tm = min(TOKEN_TILE, seq_len)
    tps = seq_len // tm
    return pl.pallas_call(
        functools.partial(_ffn_kernel, final_norm=final_norm),
        grid=(n // tm,),
        in_specs=[
            pl.BlockSpec((tm, d), lambda i: (i, 0)),
            pl.BlockSpec((1, 6, d), lambda i: (i // tps, 0, 0)),
            pl.BlockSpec((1, d), lambda i: (0, 0)),
            pl.BlockSpec((None, d, 2 * d_ff), lambda i: (layer, 0, 0)),
            pl.BlockSpec((None, d_ff, d), lambda i: (layer, 0, 0)),
            pl.BlockSpec((1, d), lambda i: (0, 0)),
        ],
        out_specs=pl.BlockSpec((tm, d), lambda i: (i, 0)),
        out_shape=jax.ShapeDtypeStruct((n, d), F32),
        scratch_shapes=[pltpu.VMEM((tm, d), BF16), pltpu.VMEM((tm, d_ff), BF16)],
        input_output_aliases={0: 0},
        compiler_params=_cparams("parallel"),
        name="ffn",
    )(x, mod, g, w_in, w_out, final_g)


def _rope_tables(seq_len):
    inv = 1.0 / (ROPE_THETA ** (jnp.arange(0, HEAD_DIM, 2, dtype=F32) / HEAD_DIM))
    ang = jnp.arange(seq_len, dtype=F32)[:, None] * inv[None, :]
    cos = jnp.cos(ang)
    sin = jnp.sin(ang)
    reps = LANES // HEAD_DIM
    cos_t = jnp.tile(jnp.concatenate([cos, cos], axis=-1), (1, reps))
    sin_t = jnp.tile(jnp.concatenate([-sin, sin], axis=-1), (1, reps))
    return cos_t, sin_t


def _lambda_init(layer):
    return 0.8 - 0.6 * math.exp(-0.3 * layer)


def _trunk(x, mod_all, p, ssm_ops):
    batch, seq_len, d = x.shape
    depth = p["w_in"].shape[0]
    d_v = N_HEADS * V_DIM
    d_u = p["ssm_d"].shape[1]
    cos, sin = _rope_tables(seq_len)
    x = x.reshape(batch * seq_len, d)
    row = lambda a: a.reshape(1, -1)
    for i in range(depth):
        mod = mod_all[i].reshape(batch, 6, d)
        qt, k, vt, u, ga, gs = _in_projection(x, mod, row(p["norm1_g"][i]), cos, sin, p["w_in"], i,
                                              batch, seq_len, d_u)
        lam_params = jnp.stack([p["lam_q1"][i], p["lam_k1"][i], p["lam_q2"][i], p["lam_k2"][i]])
        o = _diff_attention(qt, k, vt, lam_params, row(p["subln_g"][i]), _lambda_init(i))
        y = _ssm_branch(u, i, *ssm_ops, batch, seq_len, SSM_CHUNK)
        x = _merge(x, mod, o.reshape(batch * seq_len, d_v), y, u, ga, gs, row(p["ssm_d"][i]),
                   p["w_glu"], row(p["b_glu"][i]), p["w_attn_br"], p["w_o"], i, seq_len,
                   in_place=(i > 0))
        x = _ffn(x, mod, row(p["norm2_g"][i]), p["w_ffn_in"], p["w_ffn_out"],
                 row(p["final_g"]), i, seq_len, final_norm=(i == depth - 1))
    return x.reshape(batch, seq_len, d)


def kernel(x_prompt, x_sample, c_prompt, c_sample, w_mod, b_mod, norm1_g, w_in, lam_q1, lam_k1, lam_q2, lam_k2, subln_g, w_attn_br, ssm_a_re, ssm_a_im, ssm_log_dt, ssm_b_re, ssm_b_im, ssm_c_re, ssm_c_im, ssm_d, w_glu, b_glu, w_o, norm2_g, w_ffn_in, w_ffn_out, final_g):
    bp, bs = c_prompt.shape[0], c_sample.shape[0]
    pad = -(bp + bs) % 8
    c_all = jnp.concatenate([c_prompt, c_sample, jnp.zeros((pad, c_prompt.shape[1]), F32)], axis=0)
    mod_all = _modulation(c_all, w_mod, b_mod)
    ssm_ops = _ssm_prep(ssm_a_re, ssm_a_im, ssm_log_dt, ssm_b_re, ssm_b_im, ssm_c_re, ssm_c_im, SSM_CHUNK)
    p = dict(
        norm1_g=norm1_g, w_in=w_in.astype(BF16), lam_q1=lam_q1, lam_k1=lam_k1, lam_q2=lam_q2, lam_k2=lam_k2,
        subln_g=subln_g, w_attn_br=w_attn_br.astype(BF16), ssm_d=ssm_d, w_glu=w_glu.astype(BF16),
        b_glu=b_glu, w_o=w_o.astype(BF16), norm2_g=norm2_g, w_ffn_in=w_ffn_in.astype(BF16),
        w_ffn_out=w_ffn_out.astype(BF16), final_g=final_g)
    y_prompt = _trunk(x_prompt, mod_all[:, :bp], p, ssm_ops)
    y_sample = _trunk(x_sample, mod_all[:, bp:bp + bs], p, ssm_ops)
    return (y_prompt, y_sample)
```

```python
import functools
import math

import jax
import jax.numpy as jnp
from jax import lax
from jax.experimental import pallas as pl
from jax.experimental.pallas import tpu as pltpu

F32 = jnp.float32
BF16 = jnp.bfloat16

N_HEADS = 4
HEAD_DIM = 64
V_DIM = 2 * HEAD_DIM
GROUP_CH = 16
GROUP_SHIFT = GROUP_CH.bit_length() - 1
N_STATE = 64
ROPE_THETA = 10000.0
EPS = 1e-6
Q_SCALE = HEAD_DIM ** -0.5 * math.log2(math.e)
V_PAD = 16

V7X_VMEM_BYTES = 64 * 1024 * 1024
VMEM_LIMIT_BYTES = V7X_VMEM_BYTES * 7 // 8
LANES = 128

TOKEN_TILE = 512
PROJ_TILE = 1024
ATTN_TQ = 256
ATTN_TK = 256
ATTN_MAX_UNROLLED_CHUNKS = 64
SSM_CHUNK = 64
FFN_CHUNK = 256
SCAN_GROUPS = 4
RELAYOUT_TILE = 1024


def _cparams(*sem):
    return pltpu.CompilerParams(dimension_semantics=sem, vmem_limit_bytes=VMEM_LIMIT_BYTES)


def _dot(a, b):
    return jnp.dot(a, b, preferred_element_type=F32)


def _split3(x):
    hi = x.astype(BF16)
    r1 = x - hi.astype(F32)
    mid = r1.astype(BF16)
    lo = (r1 - mid.astype(F32)).astype(BF16)
    return hi, mid, lo


def _dot_f32(a, b):
    a_hi = a.astype(BF16)
    a_lo = (a - a_hi.astype(F32)).astype(BF16)
    b_hi = b.astype(BF16)
    b_lo = (b - b_hi.astype(F32)).astype(BF16)
    return _dot(a_hi, b_hi) + (_dot(a_hi, b_lo) + _dot(a_lo, b_hi))


def _select_cols(x, sel):
    hi, mid, lo = _split3(x)
    return _dot(hi, sel) + (_dot(mid, sel) + _dot(lo, sel))


def _select_rows(sel, x):
    hi, mid, lo = _split3(x)
    return _dot(sel, hi) + (_dot(sel, mid) + _dot(sel, lo))


def _onehot(cond):
    return jnp.where(cond, 1.0, 0.0).astype(BF16)


def _rms(x):
    return x * lax.rsqrt(jnp.mean(x * x, axis=-1, keepdims=True) + EPS)


def _mod_kernel(c_ref, w_ref, b_ref, o_ref):
    c = c_ref[...]
    s = c * jax.nn.sigmoid(c)
    o_ref[0] = _dot_f32(s, w_ref[0]) + b_ref[0]


def _modulation(c_all, w_mod, b_mod):
    depth, d, n6 = w_mod.shape
    bp = c_all.shape[0]
    tn = 1536
    return pl.pallas_call(
        _mod_kernel,
        grid=(depth, n6 // tn),
        in_specs=[
            pl.BlockSpec((bp, d), lambda l, j: (0, 0)),
            pl.BlockSpec((1, d, tn), lambda l, j: (l, 0, j)),
            pl.BlockSpec((1, 1, tn), lambda l, j: (l, 0, j)),
        ],
        out_specs=pl.BlockSpec((1, bp, tn), lambda l, j: (l, 0, j)),
        out_shape=jax.ShapeDtypeStruct((depth, bp, n6), F32),
        compiler_params=_cparams("parallel", "parallel"),
        name="modulation",
    )(c_all, w_mod, b_mod.reshape(depth, 1, n6))


def _inproj_kernel(x_ref, mod_ref, g_ref, cos_ref, sin_ref, w_ref,
                   qt_ref, k_ref, vt_ref, u_ref, ga_ref, gs_ref):
    d = x_ref.shape[1]
    n_heads = k_ref.shape[1]
    d_qk = n_heads * V_DIM
    mod = mod_ref[0]
    h = (_rms(x_ref[...]) * g_ref[...] * (1.0 + mod[1:2]) + mod[0:1]).astype(BF16)

    qk = _dot(h, w_ref[:, 0:2 * d_qk])
    reps = 2 * d_qk // LANES
    cos = jnp.tile(cos_ref[...], (1, reps))
    sin = jnp.tile(sin_ref[...], (1, reps))
    lane = lax.broadcasted_iota(jnp.int32, qk.shape, 1)
    first_half = jnp.bitwise_and(lane, HEAD_DIM // 2) == 0
    half = HEAD_DIM // 2
    rot = jnp.where(first_half,
                    pltpu.roll(qk, 2 * d_qk - half, 1),
                    pltpu.roll(qk, half, 1))
    qk = qk * cos + rot * sin
    tq = qt_ref.shape[4]
    for hd in range(n_heads):
        qh = qk[:, hd * V_DIM:(hd + 1) * V_DIM] * Q_SCALE
        for c in range(qt_ref.shape[2]):
            qt_ref[0, hd, c] = qh[c * tq:(c + 1) * tq].T.astype(BF16)
        k_ref[0, hd] = qk[:, d_qk + hd * V_DIM:d_qk + (hd + 1) * V_DIM].astype(BF16)

    off = 2 * d_qk
    d_v = n_heads * V_DIM
    d_u = u_ref.shape[1]
    vu = _dot(h, w_ref[:, off:off + d_v + d_u])
    tk = vt_ref.shape[4]
    tail_row = lax.broadcasted_iota(jnp.int32, (V_PAD, tk), 0)
    ones_row = jnp.where(tail_row == 0, 1.0, 0.0).astype(BF16)
    for hd in range(n_heads):
        for c in range(vt_ref.shape[2]):
            vt_ref[0, hd, c, :V_DIM, :] = vu[c * tk:(c + 1) * tk, hd * V_DIM:(hd + 1) * V_DIM].T.astype(BF16)
            vt_ref[0, hd, c, V_DIM:, :] = ones_row
    u_ref[...] = vu[:, d_v:].astype(BF16)
    off += d_v + d_u
    ga_ref[...] = jax.nn.sigmoid(_dot(h, w_ref[:, off:off + d])).astype(BF16)
    gs_ref[...] = jax.nn.sigmoid(_dot(h, w_ref[:, off + d:off + 2 * d])).astype(BF16)


def _in_projection(x, mod, g, cos, sin, w_all, layer, batch, seq_len, d_u):
    n, d = x.shape
    tm = min(PROJ_TILE, seq_len)
    tps = seq_len // tm
    tk = min(ATTN_TK, tm)
    tq = min(ATTN_TQ, tm)
    tok = lambda i: (i, 0)
    bf = lambda *shape: jax.ShapeDtypeStruct(shape, BF16)
    return pl.pallas_call(
        _inproj_kernel,
        grid=(n // tm,),
        in_specs=[
            pl.BlockSpec((tm, d), tok),
            pl.BlockSpec((1, 6, d), lambda i: (i // tps, 0, 0)),
            pl.BlockSpec((1, d), lambda i: (0, 0)),
            pl.BlockSpec((tm, LANES), lambda i: (i % tps, 0)),
            pl.BlockSpec((tm, LANES), lambda i: (i % tps, 0)),
            pl.BlockSpec((None,) + w_all.shape[1:], lambda i: (layer, 0, 0)),
        ],
        out_specs=[
            pl.BlockSpec((1, N_HEADS, tm // tq, V_DIM, tq), lambda i: (i // tps, 0, i % tps, 0, 0)),
            pl.BlockSpec((1, N_HEADS, tm, V_DIM), lambda i: (i // tps, 0, i % tps, 0)),
            pl.BlockSpec((1, N_HEADS, tm // tk, V_DIM + V_PAD, tk), lambda i: (i // tps, 0, i % tps, 0, 0)),
            pl.BlockSpec((tm, d_u), tok),
            pl.BlockSpec((tm, d), tok),
            pl.BlockSpec((tm, d), tok),
        ],
        out_shape=[
            bf(batch, N_HEADS, seq_len // tq, V_DIM, tq),
            bf(batch, N_HEADS, seq_len, V_DIM),
            bf(batch, N_HEADS, seq_len // tk, V_DIM + V_PAD, tk),
            bf(n, d_u), bf(n, d), bf(n, d),
        ],
        compiler_params=_cparams("parallel"),
        name="in_projection",
    )(x, mod, g, cos, sin, w_all)


def _attn_kernel(qt_ref, k_ref, vt_ref, lam_ref, g_ref, o_ref, *scratch, lam_init):
    n_chunks, _, tk = vt_ref.shape[2:]
    n_blocks, _, tq = qt_ref.shape[2:]
    per_set = len(scratch) // 2
    sets = (scratch[:per_set], scratch[per_set:])

    lp = lam_ref[...]
    lam = (jnp.exp(jnp.sum(lp[0:1] * lp[1:2], axis=-1, keepdims=True))
           - jnp.exp(jnp.sum(lp[2:3] * lp[3:4], axis=-1, keepdims=True)) + lam_init)
    gain = g_ref[...] * (1.0 - lam_init)

    def q_operand(j):
        qt = qt_ref[0, 0, j].astype(F32)
        row = lax.broadcasted_iota(jnp.int32, qt.shape, 0)
        return jnp.concatenate([jnp.where(row < HEAD_DIM, qt, 0.0), jnp.where(row >= HEAD_DIM, qt, 0.0)],
                               axis=1).astype(BF16)

    def scores(qtb, c):
        return _dot(k_ref[0, 0, c * tk:(c + 1) * tk, :], qtb)

    def softmax(st, slot, m_old):
        s = st[slot][...]
        m_new = jnp.maximum(m_old, jnp.max(s, axis=0, keepdims=True))
        st[2 + slot][...] = jnp.exp2(s - m_new).astype(BF16)
        return m_new, jnp.exp2(m_old - m_new)

    def attend(st, c, slot, alpha):
        st[4][...] = alpha * st[4][...] + _dot(vt_ref[0, 0, c], st[2 + slot][...])

    def head(st, j):
        qtb = q_operand(j)
        st[4][...] = jnp.zeros(st[4].shape, F32)
        st[0][...] = scores(qtb, 0)
        st[1][...] = scores(qtb, 1)
        return (qtb,) + softmax(st, 0, jnp.full((1, 2 * tq), -jnp.inf, F32))

    def steady(st, carry):
        qtb, m, alpha = carry
        for c in range(1, n_chunks - 1):
            slot = c % 2
            st[1 - slot][...] = scores(qtb, c + 1)
            m, alpha_new = softmax(st, slot, m)
            attend(st, c - 1, 1 - slot, alpha)
            alpha = alpha_new
        return m, alpha

    def tail(st, j, m, alpha):
        last = (n_chunks - 1) % 2
        _, alpha_last = softmax(st, last, m)
        attend(st, n_chunks - 2, 1 - last, alpha)
        attend(st, n_chunks - 1, last, alpha_last)
        acc = st[4][...]
        acc = acc[:V_DIM] / acc[V_DIM:V_DIM + 1]
        o = (acc[:, :tq] - lam * acc[:, tq:]).T
        o_ref[0, pl.ds(pl.multiple_of(j * tq, tq), tq), :] = (_rms(o) * gain).astype(o_ref.dtype)

    per_iter = 2
    while n_blocks % (2 * per_iter) == 0 and 2 * per_iter * n_chunks <= ATTN_MAX_UNROLLED_CHUNKS:
        per_iter *= 2

    def body(jj, carry):
        j0 = per_iter * jj
        for i in range(per_iter):
            cur, nxt = sets[i % 2], sets[(i + 1) % 2]
            m, alpha = steady(cur, carry)
            carry = head(nxt, jnp.minimum(j0 + i + 1, n_blocks - 1))
            tail(cur, j0 + i, m, alpha)
        return carry

    lax.fori_loop(0, n_blocks // per_iter, body, head(sets[0], 0))


def _diff_attention(qt, k, vt, lam_params, subln_g, lam_init):
    b, n_heads, l, _ = k.shape
    n_blocks, _, tq = qt.shape[2:]
    n_chunks, v_rows, tk = vt.shape[2:]
    assert n_blocks % 2 == 0 and n_chunks >= 2
    one_set = ([pltpu.VMEM((tk, 2 * tq), F32)] * 2 + [pltpu.VMEM((tk, 2 * tq), BF16)] * 2
               + [pltpu.VMEM((v_rows, 2 * tq), F32)])
    whole = lambda a: pl.BlockSpec((1, 1) + a.shape[2:], lambda bi, h: (bi, h) + (0,) * (a.ndim - 2))
    return pl.pallas_call(
        functools.partial(_attn_kernel, lam_init=lam_init),
        grid=(b, n_heads),
        in_specs=[
            whole(qt), whole(k), whole(vt),
            pl.BlockSpec(lam_params.shape, lambda bi, h: (0, 0)),
            pl.BlockSpec((1, V_DIM), lambda bi, h: (0, 0)),
        ],
        out_specs=pl.BlockSpec((1, l, V_DIM), lambda bi, h: (bi, 0, h)),
        out_shape=jax.ShapeDtypeStruct((b, l, n_heads * V_DIM), BF16),
        scratch_shapes=one_set * 2,
        compiler_params=_cparams("parallel", "parallel"),
        name="diff_attention",
    )(qt, k, vt, lam_params, subln_g)


def _ssm_prep_kernel(arow_ref, acol_ref, bt_ref, ct_ref, m_ref, win_ref, wout_ref, dec_ref, *, chunk):
    t_len = chunk
    w = t_len * GROUP_CH
    w2 = 2 * w
    arow = arow_ref[0, 0]
    acol = acol_ref[0, 0]
    fwd_lanes = lax.broadcasted_iota(jnp.int32, (1, LANES), 1) < N_STATE

    ar = arow[0:1]
    ai = arow[1:2]
    dt = jnp.exp(arow[2:3])
    zr = dt * ar
    zi = dt * ai
    mag = jnp.exp(zr)
    nr = mag * jnp.cos(zi) - 1.0
    ni = mag * jnp.sin(zi)
    den = ar * ar + ai * ai
    fr = (nr * ar + ni * ai) / den
    fi = (ni * ar - nr * ai) / den
    br = bt_ref[0, 0, 0]
    bi = bt_ref[0, 0, 1]
    bbr = fr * br - fi * bi
    bbi = fr * bi + fi * br

    s_of_row = lax.shift_right_logical(lax.broadcasted_iota(jnp.int32, (w, t_len), 0), GROUP_SHIFT)
    j_of_col = lax.broadcasted_iota(jnp.int32, (w, t_len), 1)
    rep_fwd = _onehot(j_of_col == (t_len - 1 - s_of_row))
    rep_bwd = _onehot(j_of_col == s_of_row)
    n_rows = lax.broadcasted_iota(jnp.int32, (t_len, 1), 0).astype(F32)
    pm = jnp.exp(n_rows * zr)

    def expand(p):
        return jnp.where(fwd_lanes, _select_rows(rep_fwd, p), _select_rows(rep_bwd, p))

    e_re = expand(pm * jnp.cos(n_rows * zi))
    e_im = expand(pm * jnp.sin(n_rows * zi))
    b_re = jnp.tile(bbr, (t_len, 1))
    b_im = jnp.tile(bbi, (t_len, 1))
    win_ref[0, 0, :, :LANES] = (e_re * b_re - e_im * b_im).astype(win_ref.dtype)
    win_ref[0, 0, :, LANES:] = (e_re * b_im + e_im * b_re).astype(win_ref.dtype)
    dm = jnp.exp(t_len * zr)
    dec_ref[0, 0, 0:1, :] = dm * jnp.cos(t_len * zi)
    dec_ref[0, 0, 1:2, :] = dm * jnp.sin(t_len * zi)

    def lane_maps(width):
        lane = lax.broadcasted_iota(jnp.int32, (LANES, width), 1)
        return (lax.shift_right_logical(lane, GROUP_SHIFT), jnp.bitwise_and(lane, GROUP_CH - 1),
                lax.broadcasted_iota(jnp.int32, (LANES, width), 0))

    lag_idx, ch_idx, jrow = lane_maps(w2)
    lag_idx_w, _, jrow_w = lane_maps(w)
    tile_ch = _onehot(jrow == ch_idx)
    n_lanes = jnp.minimum(lax.broadcasted_iota(jnp.int32, (1, LANES), 1), t_len).astype(F32)

    lag_tables = []
    out_tables = []
    for d in range(2):
        dtc = jnp.exp(acol[:, 4 + d:5 + d])
        zrc = dtc * acol[:, d:d + 1]
        zic = dtc * acol[:, 2 + d:3 + d]
        ptm = jnp.exp(zrc * n_lanes)
        pt_re = ptm * jnp.cos(zic * n_lanes)
        pt_im = ptm * jnp.sin(zic * n_lanes)
        c_re = _select_cols(ct_ref[0, 0, d], tile_ch)
        c_im = _select_cols(ct_ref[0, 0, 2 + d], tile_ch)

        def table(rep):
            width = rep.shape[1]
            p_re = _select_cols(pt_re, rep)
            p_im = _select_cols(pt_im, rep)
            cr = c_re[:, :width]
            ci = c_im[:, :width]
            return p_re * cr - p_im * ci, p_re * ci + p_im * cr

        if d == 0:
            power = lag_idx - (t_len - 1)
        else:
            power = (t_len - 1) - lag_idx
        power = jnp.where(power >= 0, power, -1)
        f_re, f_im = table(_onehot(jrow == power))
        lag_tables.append((f_re, f_im))
        if d == 0:
            out_tables.append((f_re[:, w:], f_im[:, w:]))
        else:
            out_tables.append(table(_onehot(jrow_w == (t_len - lag_idx_w))))

    for i, o in enumerate((out_tables[0][0], out_tables[1][0], -out_tables[0][1], -out_tables[1][1])):
        wout_ref[0, 0, i * N_STATE:(i + 1) * N_STATE, :] = o.astype(wout_ref.dtype)

    lhs = jnp.concatenate([bbr, -bbi], axis=1)
    rhs = jnp.concatenate([lag_tables[0][0], lag_tables[1][0], lag_tables[0][1], lag_tables[1][1]], axis=0)
    strip = _dot_f32(lhs, rhs)

    per_tile = LANES // GROUP_CH
    for r in range(per_tile):
        rolled = strip if r == 0 else pltpu.roll(strip, w2 - r * GROUP_CH, 1)
        for s in range(t_len):
            if (t_len - 1 - s) % per_tile == r:
                q = (t_len - 1 - s) // per_tile
                m_ref[0, 0, s * GROUP_CH:(s + 1) * GROUP_CH, :] = (
                    rolled[:, q * LANES:q * LANES + w].astype(m_ref.dtype))


def _ssm_prep(a_re, a_im, log_dt, b_re, b_im, c_re, c_im, chunk):
    depth, _, n_groups, n_state = a_re.shape
    w = chunk * GROUP_CH
    gd = lambda x: jnp.swapaxes(x, 1, 2)
    ldt = jnp.broadcast_to(gd(log_dt)[..., None], (depth, n_groups, 2, n_state))
    packed = lambda x: x.reshape(depth, n_groups, 1, 2 * n_state)
    arow = jnp.concatenate([packed(gd(a_re)), packed(gd(a_im)), packed(ldt),
                            jnp.zeros((depth, n_groups, 5, 2 * n_state), F32)], axis=2)
    acol = jnp.swapaxes(jnp.concatenate([gd(a_re), gd(a_im), ldt, jnp.zeros_like(ldt)], axis=2), 2, 3)
    bt = jnp.stack([gd(b_re), gd(b_im)], axis=2)
    bt = bt.transpose(0, 1, 2, 5, 3, 4).reshape(depth, n_groups, 2, GROUP_CH, 2 * n_state)
    ct = jnp.swapaxes(jnp.concatenate([gd(c_re), gd(c_im)], axis=2), 3, 4)
    ct = jnp.pad(ct, ((0, 0),) * 4 + ((0, LANES - GROUP_CH),))
    blk = lambda *tail: pl.BlockSpec((1, 1) + tail, lambda l, g: (l, g) + (0,) * len(tail))
    return pl.pallas_call(
        functools.partial(_ssm_prep_kernel, chunk=chunk),
        grid=(depth, n_groups),
        in_specs=[blk(8, LANES), blk(n_state, 8), blk(2, GROUP_CH, LANES), blk(4, n_state, LANES)],
        out_specs=[blk(w, w), blk(w, 4 * n_state), blk(4 * n_state, w), blk(2, LANES)],
        out_shape=[
            jax.ShapeDtypeStruct((depth, n_groups, w, w), BF16),
            jax.ShapeDtypeStruct((depth, n_groups, w, 4 * n_state), BF16),
            jax.ShapeDtypeStruct((depth, n_groups, 4 * n_state, w), BF16),
            jax.ShapeDtypeStruct((depth, n_groups, 2, LANES), F32),
        ],
        compiler_params=_cparams("parallel", "parallel"),
        name="ssm_prep",
    )(arow, acol, bt, ct)


def _ssm_state_kernel(u_ref, win_ref, sre_ref, sim_ref):
    s = _dot(u_ref[0], win_ref[0, 0])
    sre_ref[0] = s[:, :LANES]
    sim_ref[0] = s[:, LANES:]


def _ssm_out_kernel(u_ref, hfr_ref, hfi_ref, hbr_ref, hbi_ref, m_ref, wout_ref, y_ref):
    fwd = lax.broadcasted_iota(jnp.int32, hfr_ref.shape[1:], 1) < N_STATE
    h = jnp.concatenate([jnp.where(fwd, hfr_ref[0], hbr_ref[0]),
                         jnp.where(fwd, hfi_ref[0], hbi_ref[0])], axis=1).astype(BF16)
    y_ref[0] = (_dot(u_ref[0], m_ref[0, 0]) + _dot(h, wout_ref[0, 0])).astype(y_ref.dtype)


def _ssm_scan_kernel(sre_ref, sim_ref, dre_ref, dim_ref, hfr_ref, hfi_ref, hbr_ref, hbi_ref,
                     *, batch, n_chunks):
    n_blk, _, lanes = sre_ref.shape
    fwd = jnp.bitwise_and(lax.broadcasted_iota(jnp.int32, (batch, lanes), 1), N_STATE) == 0

    def step(i, carry):
        rows_f = pl.ds(i, batch, stride=n_chunks)
        rows_b = pl.ds(n_chunks - 1 - i, batch, stride=n_chunks)
        out = []
        for q in range(n_blk):
            xr, xi = carry[2 * q], carry[2 * q + 1]
            hfr_ref[q, rows_f, :] = xr
            hfi_ref[q, rows_f, :] = xi
            hbr_ref[q, rows_b, :] = xr
            hbi_ref[q, rows_b, :] = xi
            sr = jnp.where(fwd, sre_ref[q, rows_f, :], sre_ref[q, rows_b, :])
            si = jnp.where(fwd, sim_ref[q, rows_f, :], sim_ref[q, rows_b, :])
            dr = dre_ref[q]
            di = dim_ref[q]
            out += [dr * xr - di * xi + sr, dr * xi + di * xr + si]
        return tuple(out)

    zeros = jnp.zeros((batch, lanes), F32)
    lax.fori_loop(0, n_chunks, step, (zeros,) * (2 * n_blk))


def _to_groups_kernel(u_ref, ug_ref, scr_ref, *, chunk):
    tm, d_u = u_ref.shape
    n_ch = tm // chunk
    per_tile = LANES // GROUP_CH
    for j in range(d_u // LANES):
        scr_ref[j] = u_ref[:, j * LANES:(j + 1) * LANES].astype(F32)
    for s in range(chunk):
        dst = (s % per_tile) * GROUP_CH
        for j in range(d_u // LANES):
            slab = scr_ref[j, pl.ds(s, n_ch, stride=chunk), :]
            for gg in range(per_tile):
                shift = (dst - gg * GROUP_CH) % LANES
                moved = (pltpu.roll(slab, shift, 1) if shift else slab).astype(ug_ref.dtype)
                ug_ref[j * per_tile + gg, :, s * GROUP_CH:(s + 1) * GROUP_CH] = moved[:, dst:dst + GROUP_CH]


def _from_groups_kernel(yg_ref, y_ref, scr_ref, *, chunk):
    tm, d_u = y_ref.shape
    n_ch = tm // chunk
    per_tile = LANES // GROUP_CH
    group_of_lane = lax.shift_right_logical(lax.broadcasted_iota(jnp.int32, (n_ch, LANES), 1), GROUP_SHIFT)
    for s in range(chunk):
        src = (s % per_tile) * GROUP_CH
        base = (s // per_tile) * LANES
        for j in range(d_u // LANES):
            slab = jnp.zeros((n_ch, LANES), F32)
            for gg in range(per_tile):
                v = yg_ref[j * per_tile + gg, :, base:base + LANES].astype(F32)
                shift = (gg * GROUP_CH - src) % LANES
                v = pltpu.roll(v, shift, 1) if shift else v
                slab = jnp.where(group_of_lane == gg, v, slab)
            scr_ref[j, pl.ds(s, n_ch, stride=chunk), :] = slab
    for j in range(d_u // LANES):
        y_ref[:, j * LANES:(j + 1) * LANES] = scr_ref[j].astype(y_ref.dtype)


def _group_relayout(x, to_groups, batch, seq_len, chunk, d_u):
    n_groups = d_u // GROUP_CH
    w = chunk * GROUP_CH
    n_chunks = seq_len // chunk
    tm = min(RELAYOUT_TILE, seq_len)
    tps = seq_len // tm
    tok_spec = pl.BlockSpec((tm, d_u), lambda i: (i, 0))
    grp_spec = pl.BlockSpec((n_groups, tm // chunk, w), lambda i: (0, i, 0))
    grp_shape = (n_groups, batch * n_chunks, w)
    scratch = [pltpu.VMEM((d_u // LANES, tm, LANES), F32)]
    if to_groups:
        return pl.pallas_call(
            functools.partial(_to_groups_kernel, chunk=chunk),
            grid=(batch * tps,), in_specs=[tok_spec], out_specs=grp_spec,
            out_shape=jax.ShapeDtypeStruct(grp_shape, x.dtype), scratch_shapes=scratch,
            compiler_params=_cparams("parallel"), name="ssm_to_groups",
        )(x)
    return pl.pallas_call(
        functools.partial(_from_groups_kernel, chunk=chunk),
        grid=(batch * tps,), in_specs=[grp_spec], out_specs=tok_spec,
        out_shape=jax.ShapeDtypeStruct((batch * seq_len, d_u), x.dtype), scratch_shapes=scratch,
        compiler_params=_cparams("parallel"), name="ssm_from_groups",
    )(x)


def _ssm_branch(u, layer, m_all, win_all, wout_all, dec_all, batch, seq_len, chunk):
    n, d_u = u.shape
    n_groups = d_u // GROUP_CH
    w = chunk * GROUP_CH
    n_chunks = seq_len // chunk
    rows = batch * n_chunks
    st = 4 * N_STATE

    ug = _group_relayout(u, True, batch, seq_len, chunk, d_u)
    state = jax.ShapeDtypeStruct((n_groups, rows, LANES), F32)
    one = lambda g: (g, 0, 0)
    sre, sim = pl.pallas_call(
        _ssm_state_kernel,
        grid=(n_groups,),
        in_specs=[pl.BlockSpec((1, rows, w), one),
                  pl.BlockSpec((1, 1, w, st), lambda g: (layer, g, 0, 0))],
        out_specs=[pl.BlockSpec((1, rows, LANES), one)] * 2,
        out_shape=[state] * 2,
        compiler_params=_cparams("parallel"),
        name="ssm_chunk_state",
    )(ug, win_all)

    dec = dec_all[layer]
    gb = min(SCAN_GROUPS, n_groups)
    hs = pl.pallas_call(
        functools.partial(_ssm_scan_kernel, batch=batch, n_chunks=n_chunks),
        grid=(n_groups // gb,),
        in_specs=[pl.BlockSpec((gb, rows, LANES), one)] * 2 + [pl.BlockSpec((gb, 1, LANES), one)] * 2,
        out_specs=[pl.BlockSpec((gb, rows, LANES), one)] * 4,
        out_shape=[state] * 4,
        compiler_params=_cparams("parallel"),
        name="ssm_chunk_scan",
    )(sre, sim, dec[:, 0:1, :], dec[:, 1:2, :])

    y = pl.pallas_call(
        _ssm_out_kernel,
        grid=(n_groups,),
        in_specs=[pl.BlockSpec((1, rows, w), one)]
                 + [pl.BlockSpec((1, rows, LANES), one)] * 4
                 + [pl.BlockSpec((1, 1, w, w), lambda g: (layer, g, 0, 0)),
                    pl.BlockSpec((1, 1, st, w), lambda g: (layer, g, 0, 0))],
        out_specs=pl.BlockSpec((1, rows, w), one),
        out_shape=jax.ShapeDtypeStruct((n_groups, rows, w), BF16),
        compiler_params=_cparams("parallel"),
        name="ssm_chunk_output",
    )(ug, *hs, m_all, wout_all)
    return _group_relayout(y, False, batch, seq_len, chunk, d_u)


def _merge_kernel(x_ref, mod_ref, o_ref, y_ref, u_ref, ga_ref, gs_ref, d_ref,
                  wglu_ref, bglu_ref, wattn_ref, wo_ref, out_ref):
    d = x_ref.shape[1]
    y = y_ref[...] + d_ref[...] * u_ref[...].astype(F32)
    z = jax.nn.gelu(y, approximate=True).astype(BF16)
    glu = _dot(z, wglu_ref[...]) + bglu_ref[...]
    y_ssm = glu[:, :d] * jax.nn.sigmoid(glu[:, d:])
    y_attn = _dot(o_ref[...], wattn_ref[...])
    merged = (ga_ref[...].astype(F32) * y_attn + gs_ref[...].astype(F32) * y_ssm).astype(BF16)
    out_ref[...] = x_ref[...] + mod_ref[0][2:3] * _dot(merged, wo_ref[...])


def _merge(x, mod, o, y, u, ga, gs, ssm_d, w_glu, b_glu, w_attn, w_o, layer, seq_len, in_place):
    n, d = x.shape
    tm = min(PROJ_TILE, seq_len)
    tps = seq_len // tm
    tok = lambda i: (i, 0)
    full = lambda a: pl.BlockSpec(a.shape, lambda i: (0, 0))
    of_layer = lambda a: pl.BlockSpec((None,) + a.shape[1:], lambda i: (layer, 0, 0))
    return pl.pallas_call(
        _merge_kernel,
        grid=(n // tm,),
        in_specs=[
            pl.BlockSpec((tm, d), tok),
            pl.BlockSpec((1, 6, d), lambda i: (i // tps, 0, 0)),
            pl.BlockSpec((tm, o.shape[1]), tok),
            pl.BlockSpec((tm, y.shape[1]), tok),
            pl.BlockSpec((tm, u.shape[1]), tok),
            pl.BlockSpec((tm, d), tok),
            pl.BlockSpec((tm, d), tok),
            full(ssm_d), of_layer(w_glu), full(b_glu), of_layer(w_attn), of_layer(w_o),
        ],
        out_specs=pl.BlockSpec((tm, d), tok),
        out_shape=jax.ShapeDtypeStruct((n, d), F32),
        input_output_aliases={0: 0} if in_place else {},
        compiler_params=_cparams("parallel"),
        name="merge_out_projection",
    )(x, mod, o, y, u, ga, gs, ssm_d, w_glu, b_glu, w_attn, w_o)


def _ffn_kernel(x_ref, mod_ref, g_ref, win_ref, wd_ref, fg_ref, out_ref, h_ref, t_ref, *, final_norm):
    mod = mod_ref[0]
    d_ff = wd_ref.shape[0]
    h_ref[...] = (_rms(x_ref[...]) * g_ref[...] * (1.0 + mod[4:5]) + mod[3:4]).astype(BF16)
    for c in range(d_ff // FFN_CHUNK):
        cols = slice(c * FFN_CHUNK, (c + 1) * FFN_CHUNK)
        gate = _dot(h_ref[...], win_ref[:, cols])
        up = _dot(h_ref[...], win_ref[:, d_ff + c * FFN_CHUNK:d_ff + (c + 1) * FFN_CHUNK])
        t_ref[:, cols] = (gate * jax.nn.sigmoid(gate) * up).astype(BF16)
    xn = x_ref[...] + mod[5:6] * _dot(t_ref[...], wd_ref[...])
    if final_norm:
        xn = _rms(xn) * fg_ref[...]
    out_ref[...] = xn


def _ffn(x, mod, g, w_in, w_out, final_g, layer, seq_len, final_norm):
    n, d = x.shape
    d_ff = w_out.shape[1]
    assert d_ff % FFN_CHUNK == 0
    tm = min(TOKEN_TILE, seq_len)
    tps = seq_len // tm
    return pl.pallas_call(
        functools.partial(_ffn_kernel, final_norm=final_norm),
        grid=(n // tm,),
        in_specs=[
            pl.BlockSpec((tm, d), lambda i: (i, 0)),
            pl.BlockSpec((1, 6, d), lambda i: (i // tps, 0, 0)),
            pl.BlockSpec((1, d), lambda i: (0, 0)),
            pl.BlockSpec((None, d, 2 * d_ff), lambda i: (layer, 0, 0)),
            pl.BlockSpec((None, d_ff, d), lambda i: (layer, 0, 0)),
            pl.BlockSpec((1, d), lambda i: (0, 0)),
        ],
        out_specs=pl.BlockSpec((tm, d), lambda i: (i, 0)),
        out_shape=jax.ShapeDtypeStruct((n, d), F32),
        scratch_shapes=[pltpu.VMEM((tm, d), BF16), pltpu.VMEM((tm, d_ff), BF16)],
        input_output_aliases={0: 0},
        compiler_params=_cparams("parallel"),
        name="ffn",
    )(x, mod, g, w_in, w_out, final_g)


def _rope_tables(seq_len):
    inv = 1.0 / (ROPE_THETA ** (jnp.arange(0, HEAD_DIM, 2, dtype=F32) / HEAD_DIM))
    ang = jnp.arange(seq_len, dtype=F32)[:, None] * inv[None, :]
    cos = jnp.cos(ang)
    sin = jnp.sin(ang)
    reps = LANES // HEAD_DIM
    cos_t = jnp.tile(jnp.concatenate([cos, cos], axis=-1), (1, reps))
    sin_t = jnp.tile(jnp.concatenate([-sin, sin], axis=-1), (1, reps))
    return cos_t, sin_t


def _lambda_init(layer):
    return 0.8 - 0.6 * math.exp(-0.3 * layer)


def _trunk(x, mod_all, p, ssm_ops):
    batch, seq_len, d = x.shape
    depth = p["w_in"].shape[0]
    d_v = N_HEADS * V_DIM
    d_u = p["ssm_d"].shape[1]
    for tile in (PROJ_TILE, TOKEN_TILE, RELAYOUT_TILE, 2 * ATTN_TQ, 2 * ATTN_TK, SSM_CHUNK):
        assert seq_len % min(tile, seq_len) == 0, (seq_len, tile)
    cos, sin = _rope_tables(seq_len)
    x = x.reshape(batch * seq_len, d)
    row = lambda a: a.reshape(1, -1)
    for i in range(depth):
        mod = mod_all[i].reshape(batch, 6, d)
        qt, k, vt, u, ga, gs = _in_projection(x, mod, row(p["norm1_g"][i]), cos, sin, p["w_in"], i,
                                              batch, seq_len, d_u)
        lam_params = jnp.stack([p["lam_q1"][i], p["lam_k1"][i], p["lam_q2"][i], p["lam_k2"][i]])
        o = _diff_attention(qt, k, vt, lam_params, row(p["subln_g"][i]), _lambda_init(i))
        y = _ssm_branch(u, i, *ssm_ops, batch, seq_len, SSM_CHUNK)
        x = _merge(x, mod, o.reshape(batch * seq_len, d_v), y, u, ga, gs, row(p["ssm_d"][i]),
                   p["w_glu"], row(p["b_glu"][i]), p["w_attn_br"], p["w_o"], i, seq_len,
                   in_place=(i > 0))
        x = _ffn(x, mod, row(p["norm2_g"][i]), p["w_ffn_in"], p["w_ffn_out"],
                 row(p["final_g"]), i, seq_len, final_norm=(i == depth - 1))
    return x.reshape(batch, seq_len, d)


def kernel(x_prompt, x_sample, c_prompt, c_sample, w_mod, b_mod, norm1_g, w_in, lam_q1, lam_k1, lam_q2, lam_k2, subln_g, w_attn_br, ssm_a_re, ssm_a_im, ssm_log_dt, ssm_b_re, ssm_b_im, ssm_c_re, ssm_c_im, ssm_d, w_glu, b_glu, w_o, norm2_g, w_ffn_in, w_ffn_out, final_g):
    bp, bs = c_prompt.shape[0], c_sample.shape[0]
    pad = -(bp + bs) % 8
    c_all = jnp.concatenate([c_prompt, c_sample, jnp.zeros((pad, c_prompt.shape[1]), F32)], axis=0)
    mod_all = _modulation(c_all, w_mod, b_mod)
    ssm_ops = _ssm_prep(ssm_a_re, ssm_a_im, ssm_log_dt, ssm_b_re, ssm_b_im, ssm_c_re, ssm_c_im, SSM_CHUNK)
    p = dict(
        norm1_g=norm1_g, w_in=w_in.astype(BF16), lam_q1=lam_q1, lam_k1=lam_k1, lam_q2=lam_q2, lam_k2=lam_k2,
        subln_g=subln_g, w_attn_br=w_attn_br.astype(BF16), ssm_d=ssm_d, w_glu=w_glu.astype(BF16),
        b_glu=b_glu, w_o=w_o.astype(BF16), norm2_g=norm2_g, w_ffn_in=w_ffn_in.astype(BF16),
        w_ffn_out=w_ffn_out.astype(BF16), final_g=final_g)
    y_prompt = _trunk(x_prompt, mod_all[:, :bp], p, ssm_ops)
    y_sample = _trunk(x_sample, mod_all[:, bp:bp + bs], p, ssm_ops)
    return (y_prompt, y_sample)
```

```python
import functools
import math

import jax
import jax.numpy as jnp
from jax import lax
from jax.experimental import pallas as pl
from jax.experimental.pallas import tpu as pltpu

F32 = jnp.float32
BF16 = jnp.bfloat16

N_HEADS = 4
HEAD_DIM = 64
V_DIM = 2 * HEAD_DIM
GROUP_CH = 16
GROUP_SHIFT = GROUP_CH.bit_length() - 1
N_STATE = 64
ROPE_THETA = 10000.0
EPS = 1e-6
Q_SCALE = HEAD_DIM ** -0.5 * math.log2(math.e)
V_PAD = 16

V7X_VMEM_BYTES = 64 * 1024 * 1024
VMEM_LIMIT_BYTES = V7X_VMEM_BYTES * 7 // 8
LANES = 128

TOKEN_TILE = 512
PROJ_TILE = 1024
ATTN_TQ = 256
ATTN_TK = 256
ATTN_MAX_UNROLLED_CHUNKS = 64
SSM_CHUNK = 64
FFN_CHUNK = 256
SCAN_GROUPS = 4
RELAYOUT_TILE = 1024


def _cparams(*sem):
    return pltpu.CompilerParams(dimension_semantics=sem, vmem_limit_bytes=VMEM_LIMIT_BYTES)


def _dot(a, b):
    return jnp.dot(a, b, preferred_element_type=F32)


def _split3(x):
    hi = x.astype(BF16)
    r1 = x - hi.astype(F32)
    mid = r1.astype(BF16)
    lo = (r1 - mid.astype(F32)).astype(BF16)
    return hi, mid, lo


def _dot_f32(a, b):
    a_hi = a.astype(BF16)
    a_lo = (a - a_hi.astype(F32)).astype(BF16)
    b_hi = b.astype(BF16)
    b_lo = (b - b_hi.astype(F32)).astype(BF16)
    return _dot(a_hi, b_hi) + (_dot(a_hi, b_lo) + _dot(a_lo, b_hi))


def _select_cols(x, sel):
    hi, mid, lo = _split3(x)
    return _dot(hi, sel) + (_dot(mid, sel) + _dot(lo, sel))


def _select_rows(sel, x):
    hi, mid, lo = _split3(x)
    return _dot(sel, hi) + (_dot(sel, mid) + _dot(sel, lo))


def _onehot(cond):
    return jnp.where(cond, 1.0, 0.0).astype(BF16)


def _rms(x):
    return x * lax.rsqrt(jnp.mean(x * x, axis=-1, keepdims=True) + EPS)


def _mod_kernel(c_ref, w_ref, b_ref, o_ref):
    c = c_ref[...]
    s = c * jax.nn.sigmoid(c)
    o_ref[0] = _dot_f32(s, w_ref[0]) + b_ref[0]


def _modulation(c_all, w_mod, b_mod):
    depth, d, n6 = w_mod.shape
    bp = c_all.shape[0]
    tn = 1536
    return pl.pallas_call(
        _mod_kernel,
        grid=(depth, n6 // tn),
        in_specs=[
            pl.BlockSpec((bp, d), lambda l, j: (0, 0)),
            pl.BlockSpec((1, d, tn), lambda l, j: (l, 0, j)),
            pl.BlockSpec((1, 1, tn), lambda l, j: (l, 0, j)),
        ],
        out_specs=pl.BlockSpec((1, bp, tn), lambda l, j: (l, 0, j)),
        out_shape=jax.ShapeDtypeStruct((depth, bp, n6), F32),
        compiler_params=_cparams("parallel", "parallel"),
        name="modulation",
    )(c_all, w_mod, b_mod.reshape(depth, 1, n6))


def _inproj_kernel(x_ref, mod_ref, g_ref, cos_ref, sin_ref, w_ref,
                   qt_ref, k_ref, vt_ref, u_ref, ug_ref, ga_ref, gs_ref, u_scr_ref):
    d = x_ref.shape[1]
    n_heads = k_ref.shape[1]
    d_qk = n_heads * V_DIM
    mod = mod_ref[0]
    h = (_rms(x_ref[...]) * g_ref[...] * (1.0 + mod[1:2]) + mod[0:1]).astype(BF16)

    qk = _dot(h, w_ref[:, 0:2 * d_qk])
    reps = 2 * d_qk // LANES
    cos = jnp.tile(cos_ref[...], (1, reps))
    sin = jnp.tile(sin_ref[...], (1, reps))
    lane = lax.broadcasted_iota(jnp.int32, qk.shape, 1)
    first_half = jnp.bitwise_and(lane, HEAD_DIM // 2) == 0
    half = HEAD_DIM // 2
    rot = jnp.where(first_half,
                    pltpu.roll(qk, 2 * d_qk - half, 1),
                    pltpu.roll(qk, half, 1))
    qk = qk * cos + rot * sin
    tq = qt_ref.shape[4]
    for hd in range(n_heads):
        qh = qk[:, hd * V_DIM:(hd + 1) * V_DIM] * Q_SCALE
        for c in range(qt_ref.shape[2]):
            qt_ref[0, hd, c] = qh[c * tq:(c + 1) * tq].T.astype(BF16)
        k_ref[0, hd] = qk[:, d_qk + hd * V_DIM:d_qk + (hd + 1) * V_DIM].astype(BF16)

    off = 2 * d_qk
    d_v = n_heads * V_DIM
    d_u = u_ref.shape[1]
    vu = _dot(h, w_ref[:, off:off + d_v + d_u])
    tk = vt_ref.shape[4]
    tail_row = lax.broadcasted_iota(jnp.int32, (V_PAD, tk), 0)
    ones_row = jnp.where(tail_row == 0, 1.0, 0.0).astype(BF16)
    for hd in range(n_heads):
        for c in range(vt_ref.shape[2]):
            vt_ref[0, hd, c, :V_DIM, :] = vu[c * tk:(c + 1) * tk, hd * V_DIM:(hd + 1) * V_DIM].T.astype(BF16)
            vt_ref[0, hd, c, V_DIM:, :] = ones_row
    u_ref[...] = vu[:, d_v:].astype(BF16)
    for j in range(d_u // LANES):
        u_scr_ref[j] = vu[:, d_v + j * LANES:d_v + (j + 1) * LANES]
    _scatter_to_groups(u_scr_ref, ug_ref, SSM_CHUNK)
    off += d_v + d_u
    ga_ref[...] = jax.nn.sigmoid(_dot(h, w_ref[:, off:off + d])).astype(BF16)
    gs_ref[...] = jax.nn.sigmoid(_dot(h, w_ref[:, off + d:off + 2 * d])).astype(BF16)


def _in_projection(x, mod, g, cos, sin, w_all, layer, batch, seq_len, d_u):
    n, d = x.shape
    tm = min(PROJ_TILE, seq_len)
    tps = seq_len // tm
    tk = min(ATTN_TK, tm)
    tq = min(ATTN_TQ, tm)
    tok = lambda i: (i, 0)
    bf = lambda *shape: jax.ShapeDtypeStruct(shape, BF16)
    return pl.pallas_call(
        _inproj_kernel,
        grid=(n // tm,),
        in_specs=[
            pl.BlockSpec((tm, d), tok),
            pl.BlockSpec((1, 6, d), lambda i: (i // tps, 0, 0)),
            pl.BlockSpec((1, d), lambda i: (0, 0)),
            pl.BlockSpec((tm, LANES), lambda i: (i % tps, 0)),
            pl.BlockSpec((tm, LANES), lambda i: (i % tps, 0)),
            pl.BlockSpec((None,) + w_all.shape[1:], lambda i: (layer, 0, 0)),
        ],
        out_specs=[
            pl.BlockSpec((1, N_HEADS, tm // tq, V_DIM, tq), lambda i: (i // tps, 0, i % tps, 0, 0)),
            pl.BlockSpec((1, N_HEADS, tm, V_DIM), lambda i: (i // tps, 0, i % tps, 0)),
            pl.BlockSpec((1, N_HEADS, tm // tk, V_DIM + V_PAD, tk), lambda i: (i // tps, 0, i % tps, 0, 0)),
            pl.BlockSpec((tm, d_u), tok),
            pl.BlockSpec((d_u // GROUP_CH, tm // SSM_CHUNK, SSM_CHUNK * GROUP_CH), lambda i: (0, i, 0)),
            pl.BlockSpec((tm, d), tok),
            pl.BlockSpec((tm, d), tok),
        ],
        out_shape=[
            bf(batch, N_HEADS, seq_len // tq, V_DIM, tq),
            bf(batch, N_HEADS, seq_len, V_DIM),
            bf(batch, N_HEADS, seq_len // tk, V_DIM + V_PAD, tk),
            bf(n, d_u), bf(d_u // GROUP_CH, n // SSM_CHUNK, SSM_CHUNK * GROUP_CH), bf(n, d), bf(n, d),
        ],
        scratch_shapes=[pltpu.VMEM((d_u // LANES, tm, LANES), F32)],
        compiler_params=_cparams("parallel"),
        name="in_projection",
    )(x, mod, g, cos, sin, w_all)


def _attn_kernel(qt_ref, k_ref, vt_ref, lam_ref, g_ref, o_ref, *scratch, lam_init):
    n_chunks, _, tk = vt_ref.shape[2:]
    n_blocks, _, tq = qt_ref.shape[2:]
    per_set = len(scratch) // 2
    sets = (scratch[:per_set], scratch[per_set:])

    lp = lam_ref[...]
    lam = (jnp.exp(jnp.sum(lp[0:1] * lp[1:2], axis=-1, keepdims=True))
           - jnp.exp(jnp.sum(lp[2:3] * lp[3:4], axis=-1, keepdims=True)) + lam_init)
    gain = g_ref[...] * (1.0 - lam_init)

    def q_operand(j):
        qt = qt_ref[0, 0, j].astype(F32)
        row = lax.broadcasted_iota(jnp.int32, qt.shape, 0)
        return jnp.concatenate([jnp.where(row < HEAD_DIM, qt, 0.0), jnp.where(row >= HEAD_DIM, qt, 0.0)],
                               axis=1).astype(BF16)

    def scores(qtb, c):
        return _dot(k_ref[0, 0, c * tk:(c + 1) * tk, :], qtb)

    def softmax(st, slot, m_old):
        s = st[slot][...]
        m_new = jnp.maximum(m_old, jnp.max(s, axis=0, keepdims=True))
        st[2 + slot][...] = jnp.exp2(s - m_new).astype(BF16)
        return m_new, jnp.exp2(m_old - m_new)

    def attend(st, c, slot, alpha):
        st[4][...] = alpha * st[4][...] + _dot(vt_ref[0, 0, c], st[2 + slot][...])

    def head(st, j):
        qtb = q_operand(j)
        st[4][...] = jnp.zeros(st[4].shape, F32)
        st[0][...] = scores(qtb, 0)
        st[1][...] = scores(qtb, 1)
        return (qtb,) + softmax(st, 0, jnp.full((1, 2 * tq), -jnp.inf, F32))

    def steady(st, carry):
        qtb, m, alpha = carry
        for c in range(1, n_chunks - 1):
            slot = c % 2
            st[1 - slot][...] = scores(qtb, c + 1)
            m, alpha_new = softmax(st, slot, m)
            attend(st, c - 1, 1 - slot, alpha)
            alpha = alpha_new
        return m, alpha

    def tail(st, j, m, alpha):
        last = (n_chunks - 1) % 2
        _, alpha_last = softmax(st, last, m)
        attend(st, n_chunks - 2, 1 - last, alpha)
        attend(st, n_chunks - 1, last, alpha_last)
        acc = st[4][...]
        acc = acc[:V_DIM] / acc[V_DIM:V_DIM + 1]
        o = (acc[:, :tq] - lam * acc[:, tq:]).T
        o_ref[0, pl.ds(pl.multiple_of(j * tq, tq), tq), :] = (_rms(o) * gain).astype(o_ref.dtype)

    per_iter = 2
    while n_blocks % (2 * per_iter) == 0 and 2 * per_iter * n_chunks <= ATTN_MAX_UNROLLED_CHUNKS:
        per_iter *= 2

    def body(jj, carry):
        j0 = per_iter * jj
        for i in range(per_iter):
            cur, nxt = sets[i % 2], sets[(i + 1) % 2]
            m, alpha = steady(cur, carry)
            carry = head(nxt, jnp.minimum(j0 + i + 1, n_blocks - 1))
            tail(cur, j0 + i, m, alpha)
        return carry

    lax.fori_loop(0, n_blocks // per_iter, body, head(sets[0], 0))


def _diff_attention(qt, k, vt, lam_params, subln_g, lam_init):
    b, n_heads, l, _ = k.shape
    n_blocks, _, tq = qt.shape[2:]
    n_chunks, v_rows, tk = vt.shape[2:]
    assert n_blocks % 2 == 0 and n_chunks >= 2
    one_set = ([pltpu.VMEM((tk, 2 * tq), F32)] * 2 + [pltpu.VMEM((tk, 2 * tq), BF16)] * 2
               + [pltpu.VMEM((v_rows, 2 * tq), F32)])
    whole = lambda a: pl.BlockSpec((1, 1) + a.shape[2:], lambda bi, h: (bi, h) + (0,) * (a.ndim - 2))
    return pl.pallas_call(
        functools.partial(_attn_kernel, lam_init=lam_init),
        grid=(b, n_heads),
        in_specs=[
            whole(qt), whole(k), whole(vt),
            pl.BlockSpec(lam_params.shape, lambda bi, h: (0, 0)),
            pl.BlockSpec((1, V_DIM), lambda bi, h: (0, 0)),
        ],
        out_specs=pl.BlockSpec((1, l, V_DIM), lambda bi, h: (bi, 0, h)),
        out_shape=jax.ShapeDtypeStruct((b, l, n_heads * V_DIM), BF16),
        scratch_shapes=one_set * 2,
        compiler_params=_cparams("parallel", "parallel"),
        name="diff_attention",
    )(qt, k, vt, lam_params, subln_g)


def _ssm_prep_kernel(arow_ref, acol_ref, bt_ref, ct_ref, m_ref, win_ref, wout_ref, dec_ref, *, chunk):
    t_len = chunk
    w = t_len * GROUP_CH
    w2 = 2 * w
    arow = arow_ref[0, 0]
    acol = acol_ref[0, 0]
    fwd_lanes = lax.broadcasted_iota(jnp.int32, (1, LANES), 1) < N_STATE

    ar = arow[0:1]
    ai = arow[1:2]
    dt = jnp.exp(arow[2:3])
    zr = dt * ar
    zi = dt * ai
    mag = jnp.exp(zr)
    nr = mag * jnp.cos(zi) - 1.0
    ni = mag * jnp.sin(zi)
    den = ar * ar + ai * ai
    fr = (nr * ar + ni * ai) / den
    fi = (ni * ar - nr * ai) / den
    br = bt_ref[0, 0, 0]
    bi = bt_ref[0, 0, 1]
    bbr = fr * br - fi * bi
    bbi = fr * bi + fi * br

    s_of_row = lax.shift_right_logical(lax.broadcasted_iota(jnp.int32, (w, t_len), 0), GROUP_SHIFT)
    j_of_col = lax.broadcasted_iota(jnp.int32, (w, t_len), 1)
    rep_fwd = _onehot(j_of_col == (t_len - 1 - s_of_row))
    rep_bwd = _onehot(j_of_col == s_of_row)
    n_rows = lax.broadcasted_iota(jnp.int32, (t_len, 1), 0).astype(F32)
    pm = jnp.exp(n_rows * zr)

    def expand(p):
        return jnp.where(fwd_lanes, _select_rows(rep_fwd, p), _select_rows(rep_bwd, p))

    e_re = expand(pm * jnp.cos(n_rows * zi))
    e_im = expand(pm * jnp.sin(n_rows * zi))
    b_re = jnp.tile(bbr, (t_len, 1))
    b_im = jnp.tile(bbi, (t_len, 1))
    win_ref[0, 0, :, :LANES] = (e_re * b_re - e_im * b_im).astype(win_ref.dtype)
    win_ref[0, 0, :, LANES:] = (e_re * b_im + e_im * b_re).astype(win_ref.dtype)
    dm = jnp.exp(t_len * zr)
    dec_ref[0, 0, 0:1, :] = dm * jnp.cos(t_len * zi)
    dec_ref[0, 0, 1:2, :] = dm * jnp.sin(t_len * zi)

    def lane_maps(width):
        lane = lax.broadcasted_iota(jnp.int32, (LANES, width), 1)
        return (lax.shift_right_logical(lane, GROUP_SHIFT), jnp.bitwise_and(lane, GROUP_CH - 1),
                lax.broadcasted_iota(jnp.int32, (LANES, width), 0))

    lag_idx, ch_idx, jrow = lane_maps(w2)
    lag_idx_w, _, jrow_w = lane_maps(w)
    tile_ch = _onehot(jrow == ch_idx)
    n_lanes = jnp.minimum(lax.broadcasted_iota(jnp.int32, (1, LANES), 1), t_len).astype(F32)

    lag_tables = []
    out_tables = []
    for d in range(2):
        dtc = jnp.exp(acol[:, 4 + d:5 + d])
        zrc = dtc * acol[:, d:d + 1]
        zic = dtc * acol[:, 2 + d:3 + d]
        ptm = jnp.exp(zrc * n_lanes)
        pt_re = ptm * jnp.cos(zic * n_lanes)
        pt_im = ptm * jnp.sin(zic * n_lanes)
        c_re = _select_cols(ct_ref[0, 0, d], tile_ch)
        c_im = _select_cols(ct_ref[0, 0, 2 + d], tile_ch)

        def table(rep):
            width = rep.shape[1]
            p_re = _select_cols(pt_re, rep)
            p_im = _select_cols(pt_im, rep)
            cr = c_re[:, :width]
            ci = c_im[:, :width]
            return p_re * cr - p_im * ci, p_re * ci + p_im * cr

        if d == 0:
            power = lag_idx - (t_len - 1)
        else:
            power = (t_len - 1) - lag_idx
        power = jnp.where(power >= 0, power, -1)
        f_re, f_im = table(_onehot(jrow == power))
        lag_tables.append((f_re, f_im))
        if d == 0:
            out_tables.append((f_re[:, w:], f_im[:, w:]))
        else:
            out_tables.append(table(_onehot(jrow_w == (t_len - lag_idx_w))))

    for i, o in enumerate((out_tables[0][0], out_tables[1][0], -out_tables[0][1], -out_tables[1][1])):
        wout_ref[0, 0, i * N_STATE:(i + 1) * N_STATE, :] = o.astype(wout_ref.dtype)

    lhs = jnp.concatenate([bbr, -bbi], axis=1)
    rhs = jnp.concatenate([lag_tables[0][0], lag_tables[1][0], lag_tables[0][1], lag_tables[1][1]], axis=0)
    strip = _dot_f32(lhs, rhs)

    per_tile = LANES // GROUP_CH
    for r in range(per_tile):
        rolled = strip if r == 0 else pltpu.roll(strip, w2 - r * GROUP_CH, 1)
        for s in range(t_len):
            if (t_len - 1 - s) % per_tile == r:
                q = (t_len - 1 - s) // per_tile
                m_ref[0, 0, s * GROUP_CH:(s + 1) * GROUP_CH, :] = (
                    rolled[:, q * LANES:q * LANES + w].astype(m_ref.dtype))


def _ssm_prep(a_re, a_im, log_dt, b_re, b_im, c_re, c_im, chunk):
    depth, _, n_groups, n_state = a_re.shape
    w = chunk * GROUP_CH
    gd = lambda x: jnp.swapaxes(x, 1, 2)
    ldt = jnp.broadcast_to(gd(log_dt)[..., None], (depth, n_groups, 2, n_state))
    packed = lambda x: x.reshape(depth, n_groups, 1, 2 * n_state)
    arow = jnp.concatenate([packed(gd(a_re)), packed(gd(a_im)), packed(ldt),
                            jnp.zeros((depth, n_groups, 5, 2 * n_state), F32)], axis=2)
    acol = jnp.swapaxes(jnp.concatenate([gd(a_re), gd(a_im), ldt, jnp.zeros_like(ldt)], axis=2), 2, 3)
    bt = jnp.stack([gd(b_re), gd(b_im)], axis=2)
    bt = bt.transpose(0, 1, 2, 5, 3, 4).reshape(depth, n_groups, 2, GROUP_CH, 2 * n_state)
    ct = jnp.swapaxes(jnp.concatenate([gd(c_re), gd(c_im)], axis=2), 3, 4)
    ct = jnp.pad(ct, ((0, 0),) * 4 + ((0, LANES - GROUP_CH),))
    blk = lambda *tail: pl.BlockSpec((1, 1) + tail, lambda l, g: (l, g) + (0,) * len(tail))
    return pl.pallas_call(
        functools.partial(_ssm_prep_kernel, chunk=chunk),
        grid=(depth, n_groups),
        in_specs=[blk(8, LANES), blk(n_state, 8), blk(2, GROUP_CH, LANES), blk(4, n_state, LANES)],
        out_specs=[blk(w, w), blk(w, 4 * n_state), blk(4 * n_state, w), blk(2, LANES)],
        out_shape=[
            jax.ShapeDtypeStruct((depth, n_groups, w, w), BF16),
            jax.ShapeDtypeStruct((depth, n_groups, w, 4 * n_state), BF16),
            jax.ShapeDtypeStruct((depth, n_groups, 4 * n_state, w), BF16),
            jax.ShapeDtypeStruct((depth, n_groups, 2, LANES), F32),
        ],
        compiler_params=_cparams("parallel", "parallel"),
        name="ssm_prep",
    )(arow, acol, bt, ct)


def _ssm_state_kernel(u_ref, win_ref, sre_ref, sim_ref):
    s = _dot(u_ref[0], win_ref[0, 0])
    sre_ref[0] = s[:, :LANES]
    sim_ref[0] = s[:, LANES:]


def _ssm_out_kernel(u_ref, hfr_ref, hfi_ref, hbr_ref, hbi_ref, m_ref, wout_ref, y_ref):
    fwd = lax.broadcasted_iota(jnp.int32, hfr_ref.shape[1:], 1) < N_STATE
    h = jnp.concatenate([jnp.where(fwd, hfr_ref[0], hbr_ref[0]),
                         jnp.where(fwd, hfi_ref[0], hbi_ref[0])], axis=1).astype(BF16)
    y_ref[0] = (_dot(u_ref[0], m_ref[0, 0]) + _dot(h, wout_ref[0, 0])).astype(y_ref.dtype)


def _ssm_scan_kernel(sre_ref, sim_ref, dre_ref, dim_ref, hfr_ref, hfi_ref, hbr_ref, hbi_ref,
                     *, batch, n_chunks):
    n_blk, _, lanes = sre_ref.shape
    fwd = jnp.bitwise_and(lax.broadcasted_iota(jnp.int32, (batch, lanes), 1), N_STATE) == 0

    def step(i, carry):
        rows_f = pl.ds(i, batch, stride=n_chunks)
        rows_b = pl.ds(n_chunks - 1 - i, batch, stride=n_chunks)
        out = []
        for q in range(n_blk):
            xr, xi = carry[2 * q], carry[2 * q + 1]
            hfr_ref[q, rows_f, :] = xr
            hfi_ref[q, rows_f, :] = xi
            hbr_ref[q, rows_b, :] = xr
            hbi_ref[q, rows_b, :] = xi
            sr = jnp.where(fwd, sre_ref[q, rows_f, :], sre_ref[q, rows_b, :])
            si = jnp.where(fwd, sim_ref[q, rows_f, :], sim_ref[q, rows_b, :])
            dr = dre_ref[q]
            di = dim_ref[q]
            out += [dr * xr - di * xi + sr, dr * xi + di * xr + si]
        return tuple(out)

    zeros = jnp.zeros((batch, lanes), F32)
    lax.fori_loop(0, n_chunks, step, (zeros,) * (2 * n_blk))


def _scatter_to_groups(scr_ref, ug_ref, chunk):
    n_tiles, tm, _ = scr_ref.shape
    n_ch = tm // chunk
    per_tile = LANES // GROUP_CH
    for s in range(chunk):
        dst = (s % per_tile) * GROUP_CH
        for j in range(n_tiles):
            slab = scr_ref[j, pl.ds(s, n_ch, stride=chunk), :]
            for gg in range(per_tile):
                shift = (dst - gg * GROUP_CH) % LANES
                moved = (pltpu.roll(slab, shift, 1) if shift else slab).astype(ug_ref.dtype)
                ug_ref[j * per_tile + gg, :, s * GROUP_CH:(s + 1) * GROUP_CH] = moved[:, dst:dst + GROUP_CH]


def _from_groups_kernel(yg_ref, y_ref, scr_ref, *, chunk):
    tm, d_u = y_ref.shape
    n_ch = tm // chunk
    per_tile = LANES // GROUP_CH
    group_of_lane = lax.shift_right_logical(lax.broadcasted_iota(jnp.int32, (n_ch, LANES), 1), GROUP_SHIFT)
    for s in range(chunk):
        src = (s % per_tile) * GROUP_CH
        base = (s // per_tile) * LANES
        for j in range(d_u // LANES):
            slab = jnp.zeros((n_ch, LANES), F32)
            for gg in range(per_tile):
                v = yg_ref[j * per_tile + gg, :, base:base + LANES].astype(F32)
                shift = (gg * GROUP_CH - src) % LANES
                v = pltpu.roll(v, shift, 1) if shift else v
                slab = jnp.where(group_of_lane == gg, v, slab)
            scr_ref[j, pl.ds(s, n_ch, stride=chunk), :] = slab
    for j in range(d_u // LANES):
        y_ref[:, j * LANES:(j + 1) * LANES] = scr_ref[j].astype(y_ref.dtype)


def _from_groups(yg, batch, seq_len, chunk):
    n_groups, _, w = yg.shape
    d_u = n_groups * GROUP_CH
    tm = min(RELAYOUT_TILE, seq_len)
    return pl.pallas_call(
        functools.partial(_from_groups_kernel, chunk=chunk),
        grid=(batch * seq_len // tm,),
        in_specs=[pl.BlockSpec((n_groups, tm // chunk, w), lambda i: (0, i, 0))],
        out_specs=pl.BlockSpec((tm, d_u), lambda i: (i, 0)),
        out_shape=jax.ShapeDtypeStruct((batch * seq_len, d_u), yg.dtype),
        scratch_shapes=[pltpu.VMEM((d_u // LANES, tm, LANES), F32)],
        compiler_params=_cparams("parallel"), name="ssm_from_groups",
    )(yg)


def _ssm_branch(ug, layer, m_all, win_all, wout_all, dec_all, batch, seq_len, chunk):
    n_groups, rows, w = ug.shape
    n_chunks = seq_len // chunk
    st = 4 * N_STATE

    state = jax.ShapeDtypeStruct((n_groups, rows, LANES), F32)
    one = lambda g: (g, 0, 0)
    sre, sim = pl.pallas_call(
        _ssm_state_kernel,
        grid=(n_groups,),
        in_specs=[pl.BlockSpec((1, rows, w), one),
                  pl.BlockSpec((1, 1, w, st), lambda g: (layer, g, 0, 0))],
        out_specs=[pl.BlockSpec((1, rows, LANES), one)] * 2,
        out_shape=[state] * 2,
        compiler_params=_cparams("parallel"),
        name="ssm_chunk_state",
    )(ug, win_all)

    dec = dec_all[layer]
    gb = min(SCAN_GROUPS, n_groups)
    hs = pl.pallas_call(
        functools.partial(_ssm_scan_kernel, batch=batch, n_chunks=n_chunks),
        grid=(n_groups // gb,),
        in_specs=[pl.BlockSpec((gb, rows, LANES), one)] * 2 + [pl.BlockSpec((gb, 1, LANES), one)] * 2,
        out_specs=[pl.BlockSpec((gb, rows, LANES), one)] * 4,
        out_shape=[state] * 4,
        compiler_params=_cparams("parallel"),
        name="ssm_chunk_scan",
    )(sre, sim, dec[:, 0:1, :], dec[:, 1:2, :])

    y = pl.pallas_call(
        _ssm_out_kernel,
        grid=(n_groups,),
        in_specs=[pl.BlockSpec((1, rows, w), one)]
                 + [pl.BlockSpec((1, rows, LANES), one)] * 4
                 + [pl.BlockSpec((1, 1, w, w), lambda g: (layer, g, 0, 0)),
                    pl.BlockSpec((1, 1, st, w), lambda g: (layer, g, 0, 0))],
        out_specs=pl.BlockSpec((1, rows, w), one),
        out_shape=jax.ShapeDtypeStruct((n_groups, rows, w), BF16),
        compiler_params=_cparams("parallel"),
        name="ssm_chunk_output",
    )(ug, *hs, m_all, wout_all)
    return _from_groups(y, batch, seq_len, chunk)


def _merge_kernel(x_ref, mod_ref, o_ref, y_ref, u_ref, ga_ref, gs_ref, d_ref,
                  wglu_ref, bglu_ref, wattn_ref, wo_ref, out_ref):
    d = x_ref.shape[1]
    y = y_ref[...] + d_ref[...] * u_ref[...].astype(F32)
    z = jax.nn.gelu(y, approximate=True).astype(BF16)
    glu = _dot(z, wglu_ref[...]) + bglu_ref[...]
    y_ssm = glu[:, :d] * jax.nn.sigmoid(glu[:, d:])
    y_attn = _dot(o_ref[...], wattn_ref[...])
    merged = (ga_ref[...].astype(F32) * y_attn + gs_ref[...].astype(F32) * y_ssm).astype(BF16)
    out_ref[...] = x_ref[...] + mod_ref[0][2:3] * _dot(merged, wo_ref[...])


def _merge(x, mod, o, y, u, ga, gs, ssm_d, w_glu, b_glu, w_attn, w_o, layer, seq_len, in_place):
    n, d = x.shape
    tm = min(PROJ_TILE, seq_len)
    tps = seq_len // tm
    tok = lambda i: (i, 0)
    full = lambda a: pl.BlockSpec(a.shape, lambda i: (0, 0))
    of_layer = lambda a: pl.BlockSpec((None,) + a.shape[1:], lambda i: (layer, 0, 0))
    return pl.pallas_call(
        _merge_kernel,
        grid=(n // tm,),
        in_specs=[
            pl.BlockSpec((tm, d), tok),
            pl.BlockSpec((1, 6, d), lambda i: (i // tps, 0, 0)),
            pl.BlockSpec((tm, o.shape[1]), tok),
            pl.BlockSpec((tm, y.shape[1]), tok),
            pl.BlockSpec((tm, u.shape[1]), tok),
            pl.BlockSpec((tm, d), tok),
            pl.BlockSpec((tm, d), tok),
            full(ssm_d), of_layer(w_glu), full(b_glu), of_layer(w_attn), of_layer(w_o),
        ],
        out_specs=pl.BlockSpec((tm, d), tok),
        out_shape=jax.ShapeDtypeStruct((n, d), F32),
        input_output_aliases={0: 0} if in_place else {},
        compiler_params=_cparams("parallel"),
        name="merge_out_projection",
    )(x, mod, o, y, u, ga, gs, ssm_d, w_glu, b_glu, w_attn, w_o)


def _ffn_kernel(x_ref, mod_ref, g_ref, win_ref, wd_ref, fg_ref, out_ref, h_ref, t_ref, *, final_norm):
    mod = mod_ref[0]
    d_ff = wd_ref.shape[0]
    h_ref[...] = (_rms(x_ref[...]) * g_ref[...] * (1.0 + mod[4:5]) + mod[3:4]).astype(BF16)
    for c in range(d_ff // FFN_CHUNK):
        cols = slice(c * FFN_CHUNK, (c + 1) * FFN_CHUNK)
        gate = _dot(h_ref[...], win_ref[:, cols])
        up = _dot(h_ref[...], win_ref[:, d_ff + c * FFN_CHUNK:d_ff + (c + 1) * FFN_CHUNK])
        t_ref[:, cols] = (gate * jax.nn.sigmoid(gate) * up).astype(BF16)
    xn = x_ref[...] + mod[5:6] * _dot(t_ref[...], wd_ref[...])
    if final_norm:
        xn = _rms(xn) * fg_ref[...]
    out_ref[...] = xn


def _ffn(x, mod, g, w_in, w_out, final_g, layer, seq_len, final_norm):
    n, d = x.shape
    d_ff = w_out.shape[1]
    assert d_ff % FFN_CHUNK == 0
    tm = min(TOKEN_TILE, seq_len)
    tps = seq_len // tm
    return pl.pallas_call(
        functools.partial(_ffn_kernel, final_norm=final_norm),
        grid=(n // tm,),
        in_specs=[
            pl.BlockSpec((tm, d), lambda i: (i, 0)),
            pl.BlockSpec((1, 6, d), lambda i: (i // tps, 0, 0)),
            pl.BlockSpec((1, d), lambda i: (0, 0)),
            pl.BlockSpec((None, d, 2 * d_ff), lambda i: (layer, 0, 0)),
            pl.BlockSpec((None, d_ff, d), lambda i: (layer, 0, 0)),
            pl.BlockSpec((1, d), lambda i: (0, 0)),
        ],
        out_specs=pl.BlockSpec((tm, d), lambda i: (i, 0)),
        out_shape=jax.ShapeDtypeStruct((n, d), F32),
        scratch_shapes=[pltpu.VMEM((tm, d), BF16), pltpu.VMEM((tm, d_ff), BF16)],
        input_output_aliases={0: 0},
        compiler_params=_cparams("parallel"),
        name="ffn",
    )(x, mod, g, w_in, w_out, final_g)


def _rope_tables(seq_len):
    inv = 1.0 / (ROPE_THETA ** (jnp.arange(0, HEAD_DIM, 2, dtype=F32) / HEAD_DIM))
    ang = jnp.arange(seq_len, dtype=F32)[:, None] * inv[None, :]
    cos = jnp.cos(ang)
    sin = jnp.sin(ang)
    reps = LANES // HEAD_DIM
    cos_t = jnp.tile(jnp.concatenate([cos, cos], axis=-1), (1, reps))
    sin_t = jnp.tile(jnp.concatenate([-sin, sin], axis=-1), (1, reps))
    return cos_t, sin_t


def _lambda_init(layer):
    return 0.8 - 0.6 * math.exp(-0.3 * layer)


def _trunk(x, mod_all, p, ssm_ops):
    batch, seq_len, d = x.shape
    depth = p["w_in"].shape[0]
    d_v = N_HEADS * V_DIM
    d_u = p["ssm_d"].shape[1]
    for tile in (PROJ_TILE, TOKEN_TILE, RELAYOUT_TILE, 2 * ATTN_TQ, 2 * ATTN_TK, SSM_CHUNK):
        assert seq_len % min(tile, seq_len) == 0, (seq_len, tile)
    cos, sin = _rope_tables(seq_len)
    x = x.reshape(batch * seq_len, d)
    row = lambda a: a.reshape(1, -1)
    for i in range(depth):
        mod = mod_all[i].reshape(batch, 6, d)
        qt, k, vt, u, ug, ga, gs = _in_projection(x, mod, row(p["norm1_g"][i]), cos, sin, p["w_in"], i,
                                              batch, seq_len, d_u)
        lam_params = jnp.stack([p["lam_q1"][i], p["lam_k1"][i], p["lam_q2"][i], p["lam_k2"][i]])
        o = _diff_attention(qt, k, vt, lam_params, row(p["subln_g"][i]), _lambda_init(i))
        y = _ssm_branch(ug, i, *ssm_ops, batch, seq_len, SSM_CHUNK)
        x = _merge(x, mod, o.reshape(batch * seq_len, d_v), y, u, ga, gs, row(p["ssm_d"][i]),
                   p["w_glu"], row(p["b_glu"][i]), p["w_attn_br"], p["w_o"], i, seq_len,
                   in_place=(i > 0))
        x = _ffn(x, mod, row(p["norm2_g"][i]), p["w_ffn_in"], p["w_ffn_out"],
                 row(p["final_g"]), i, seq_len, final_norm=(i == depth - 1))
    return x.reshape(batch, seq_len, d)


def kernel(x_prompt, x_sample, c_prompt, c_sample, w_mod, b_mod, norm1_g, w_in, lam_q1, lam_k1, lam_q2, lam_k2, subln_g, w_attn_br, ssm_a_re, ssm_a_im, ssm_log_dt, ssm_b_re, ssm_b_im, ssm_c_re, ssm_c_im, ssm_d, w_glu, b_glu, w_o, norm2_g, w_ffn_in, w_ffn_out, final_g):
    bp, bs = c_prompt.shape[0], c_sample.shape[0]
    pad = -(bp + bs) % 8
    c_all = jnp.concatenate([c_prompt, c_sample, jnp.zeros((pad, c_prompt.shape[1]), F32)], axis=0)
    mod_all = _modulation(c_all, w_mod, b_mod)
    ssm_ops = _ssm_prep(ssm_a_re, ssm_a_im, ssm_log_dt, ssm_b_re, ssm_b_im, ssm_c_re, ssm_c_im, SSM_CHUNK)
    p = dict(
        norm1_g=norm1_g, w_in=w_in.astype(BF16), lam_q1=lam_q1, lam_k1=lam_k1, lam_q2=lam_q2, lam_k2=lam_k2,
        subln_g=subln_g, w_attn_br=w_attn_br.astype(BF16), ssm_d=ssm_d, w_glu=w_glu.astype(BF16),
        b_glu=b_glu, w_o=w_o.astype(BF16), norm2_g=norm2_g, w_ffn_in=w_ffn_in.astype(BF16),
        w_ffn_out=w_ffn_out.astype(BF16), final_g=final_g)
    y_prompt = _trunk(x_prompt, mod_all[:, :bp], p, ssm_ops)
    y_sample = _trunk(x_sample, mod_all[:, bp:bp + bs], p, ssm_ops)
    return (y_prompt, y_sample)
```

```python
import functools
import math

import jax
import jax.numpy as jnp
from jax import lax
from jax.experimental import pallas as pl
from jax.experimental.pallas import tpu as pltpu

F32 = jnp.float32
BF16 = jnp.bfloat16

N_HEADS = 4
HEAD_DIM = 64
V_DIM = 2 * HEAD_DIM
GROUP_CH = 16
GROUP_SHIFT = GROUP_CH.bit_length() - 1
N_STATE = 64
ROPE_THETA = 10000.0
EPS = 1e-6
Q_SCALE = HEAD_DIM ** -0.5 * math.log2(math.e)
V_PAD = 16

V7X_VMEM_BYTES = 64 * 1024 * 1024
VMEM_LIMIT_BYTES = V7X_VMEM_BYTES * 7 // 8
LANES = 128

TOKEN_TILE = 512
PROJ_TILE = 1024
ATTN_TQ = 256
ATTN_TK = 256
ATTN_MAX_UNROLLED_CHUNKS = 64
SSM_CHUNK = 64
FFN_CHUNK = 256
SCAN_GROUPS = 4


def _cparams(*sem):
    return pltpu.CompilerParams(dimension_semantics=sem, vmem_limit_bytes=VMEM_LIMIT_BYTES)


def _dot(a, b):
    return jnp.dot(a, b, preferred_element_type=F32)


def _split3(x):
    hi = x.astype(BF16)
    r1 = x - hi.astype(F32)
    mid = r1.astype(BF16)
    lo = (r1 - mid.astype(F32)).astype(BF16)
    return hi, mid, lo


def _dot_f32(a, b):
    a_hi = a.astype(BF16)
    a_lo = (a - a_hi.astype(F32)).astype(BF16)
    b_hi = b.astype(BF16)
    b_lo = (b - b_hi.astype(F32)).astype(BF16)
    return _dot(a_hi, b_hi) + (_dot(a_hi, b_lo) + _dot(a_lo, b_hi))


def _select_cols(x, sel):
    hi, mid, lo = _split3(x)
    return _dot(hi, sel) + (_dot(mid, sel) + _dot(lo, sel))


def _select_rows(sel, x):
    hi, mid, lo = _split3(x)
    return _dot(sel, hi) + (_dot(sel, mid) + _dot(sel, lo))


def _onehot(cond):
    return jnp.where(cond, 1.0, 0.0).astype(BF16)


def _rms(x):
    return x * lax.rsqrt(jnp.mean(x * x, axis=-1, keepdims=True) + EPS)


def _mod_kernel(c_ref, w_ref, b_ref, o_ref):
    c = c_ref[...]
    s = c * jax.nn.sigmoid(c)
    o_ref[0] = _dot_f32(s, w_ref[0]) + b_ref[0]


def _modulation(c_all, w_mod, b_mod):
    depth, d, n6 = w_mod.shape
    bp = c_all.shape[0]
    tn = 1536
    return pl.pallas_call(
        _mod_kernel,
        grid=(depth, n6 // tn),
        in_specs=[
            pl.BlockSpec((bp, d), lambda l, j: (0, 0)),
            pl.BlockSpec((1, d, tn), lambda l, j: (l, 0, j)),
            pl.BlockSpec((1, 1, tn), lambda l, j: (l, 0, j)),
        ],
        out_specs=pl.BlockSpec((1, bp, tn), lambda l, j: (l, 0, j)),
        out_shape=jax.ShapeDtypeStruct((depth, bp, n6), F32),
        compiler_params=_cparams("parallel", "parallel"),
        name="modulation",
    )(c_all, w_mod, b_mod.reshape(depth, 1, n6))


def _inproj_kernel(x_ref, mod_ref, g_ref, cos_ref, sin_ref, w_ref,
                   qt_ref, k_ref, vt_ref, u_ref, ug_ref, ga_ref, gs_ref, u_scr_ref):
    d = x_ref.shape[1]
    n_heads = k_ref.shape[1]
    d_qk = n_heads * V_DIM
    mod = mod_ref[0]
    h = (_rms(x_ref[...]) * g_ref[...] * (1.0 + mod[1:2]) + mod[0:1]).astype(BF16)

    qk = _dot(h, w_ref[:, 0:2 * d_qk])
    reps = 2 * d_qk // LANES
    cos = jnp.tile(cos_ref[...], (1, reps))
    sin = jnp.tile(sin_ref[...], (1, reps))
    lane = lax.broadcasted_iota(jnp.int32, qk.shape, 1)
    first_half = jnp.bitwise_and(lane, HEAD_DIM // 2) == 0
    half = HEAD_DIM // 2
    rot = jnp.where(first_half,
                    pltpu.roll(qk, 2 * d_qk - half, 1),
                    pltpu.roll(qk, half, 1))
    qk = qk * cos + rot * sin
    tq = qt_ref.shape[4]
    for hd in range(n_heads):
        qh = qk[:, hd * V_DIM:(hd + 1) * V_DIM] * Q_SCALE
        for c in range(qt_ref.shape[2]):
            qt_ref[0, hd, c] = qh[c * tq:(c + 1) * tq].T.astype(BF16)
        k_ref[0, hd] = qk[:, d_qk + hd * V_DIM:d_qk + (hd + 1) * V_DIM].astype(BF16)

    off = 2 * d_qk
    d_v = n_heads * V_DIM
    d_u = u_ref.shape[1]
    vu = _dot(h, w_ref[:, off:off + d_v + d_u])
    tk = vt_ref.shape[4]
    tail_row = lax.broadcasted_iota(jnp.int32, (V_PAD, tk), 0)
    ones_row = jnp.where(tail_row == 0, 1.0, 0.0).astype(BF16)
    for hd in range(n_heads):
        for c in range(vt_ref.shape[2]):
            vt_ref[0, hd, c, :V_DIM, :] = vu[c * tk:(c + 1) * tk, hd * V_DIM:(hd + 1) * V_DIM].T.astype(BF16)
            vt_ref[0, hd, c, V_DIM:, :] = ones_row
    u_ref[...] = vu[:, d_v:].astype(BF16)
    for j in range(d_u // LANES):
        u_scr_ref[j] = vu[:, d_v + j * LANES:d_v + (j + 1) * LANES]
    _scatter_to_groups(u_scr_ref, ug_ref, SSM_CHUNK)
    off += d_v + d_u
    ga_ref[...] = jax.nn.sigmoid(_dot(h, w_ref[:, off:off + d])).astype(BF16)
    gs_ref[...] = jax.nn.sigmoid(_dot(h, w_ref[:, off + d:off + 2 * d])).astype(BF16)


def _in_projection(x, mod, g, cos, sin, w_all, layer, batch, seq_len, d_u):
    n, d = x.shape
    tm = min(PROJ_TILE, seq_len)
    tps = seq_len // tm
    tk = min(ATTN_TK, tm)
    tq = min(ATTN_TQ, tm)
    tok = lambda i: (i, 0)
    bf = lambda *shape: jax.ShapeDtypeStruct(shape, BF16)
    return pl.pallas_call(
        _inproj_kernel,
        grid=(n // tm,),
        in_specs=[
            pl.BlockSpec((tm, d), tok),
            pl.BlockSpec((1, 6, d), lambda i: (i // tps, 0, 0)),
            pl.BlockSpec((1, d), lambda i: (0, 0)),
            pl.BlockSpec((tm, LANES), lambda i: (i % tps, 0)),
            pl.BlockSpec((tm, LANES), lambda i: (i % tps, 0)),
            pl.BlockSpec((None,) + w_all.shape[1:], lambda i: (layer, 0, 0)),
        ],
        out_specs=[
            pl.BlockSpec((1, N_HEADS, tm // tq, V_DIM, tq), lambda i: (i // tps, 0, i % tps, 0, 0)),
            pl.BlockSpec((1, N_HEADS, tm, V_DIM), lambda i: (i // tps, 0, i % tps, 0)),
            pl.BlockSpec((1, N_HEADS, tm // tk, V_DIM + V_PAD, tk), lambda i: (i // tps, 0, i % tps, 0, 0)),
            pl.BlockSpec((tm, d_u), tok),
            pl.BlockSpec((d_u // GROUP_CH, tm // SSM_CHUNK, SSM_CHUNK * GROUP_CH), lambda i: (0, i, 0)),
            pl.BlockSpec((tm, d), tok),
            pl.BlockSpec((tm, d), tok),
        ],
        out_shape=[
            bf(batch, N_HEADS, seq_len // tq, V_DIM, tq),
            bf(batch, N_HEADS, seq_len, V_DIM),
            bf(batch, N_HEADS, seq_len // tk, V_DIM + V_PAD, tk),
            bf(n, d_u), bf(d_u // GROUP_CH, n // SSM_CHUNK, SSM_CHUNK * GROUP_CH), bf(n, d), bf(n, d),
        ],
        scratch_shapes=[pltpu.VMEM((d_u // LANES, tm, LANES), F32)],
        compiler_params=_cparams("parallel"),
        name="in_projection",
    )(x, mod, g, cos, sin, w_all)


def _attn_kernel(qt_ref, k_ref, vt_ref, lam_ref, g_ref, o_ref, *scratch, lam_init):
    n_chunks, _, tk = vt_ref.shape[2:]
    n_blocks, _, tq = qt_ref.shape[2:]
    per_set = len(scratch) // 2
    sets = (scratch[:per_set], scratch[per_set:])

    lp = lam_ref[...]
    lam = (jnp.exp(jnp.sum(lp[0:1] * lp[1:2], axis=-1, keepdims=True))
           - jnp.exp(jnp.sum(lp[2:3] * lp[3:4], axis=-1, keepdims=True)) + lam_init)
    gain = g_ref[...] * (1.0 - lam_init)

    def q_operand(j):
        qt = qt_ref[0, 0, j].astype(F32)
        row = lax.broadcasted_iota(jnp.int32, qt.shape, 0)
        return jnp.concatenate([jnp.where(row < HEAD_DIM, qt, 0.0), jnp.where(row >= HEAD_DIM, qt, 0.0)],
                               axis=1).astype(BF16)

    def scores(qtb, c):
        return _dot(k_ref[0, 0, c * tk:(c + 1) * tk, :], qtb)

    def softmax(st, slot, m_old):
        s = st[slot][...]
        m_new = jnp.maximum(m_old, jnp.max(s, axis=0, keepdims=True))
        st[2 + slot][...] = jnp.exp2(s - m_new).astype(BF16)
        return m_new, jnp.exp2(m_old - m_new)

    def attend(st, c, slot, alpha):
        st[4][...] = alpha * st[4][...] + _dot(vt_ref[0, 0, c], st[2 + slot][...])

    def head(st, j):
        qtb = q_operand(j)
        st[4][...] = jnp.zeros(st[4].shape, F32)
        st[0][...] = scores(qtb, 0)
        st[1][...] = scores(qtb, 1)
        return (qtb,) + softmax(st, 0, jnp.full((1, 2 * tq), -jnp.inf, F32))

    def steady(st, carry):
        qtb, m, alpha = carry
        for c in range(1, n_chunks - 1):
            slot = c % 2
            st[1 - slot][...] = scores(qtb, c + 1)
            m, alpha_new = softmax(st, slot, m)
            attend(st, c - 1, 1 - slot, alpha)
            alpha = alpha_new
        return m, alpha

    def tail(st, j, m, alpha):
        last = (n_chunks - 1) % 2
        _, alpha_last = softmax(st, last, m)
        attend(st, n_chunks - 2, 1 - last, alpha)
        attend(st, n_chunks - 1, last, alpha_last)
        acc = st[4][...]
        acc = acc[:V_DIM] / acc[V_DIM:V_DIM + 1]
        o = (acc[:, :tq] - lam * acc[:, tq:]).T
        o_ref[0, pl.ds(pl.multiple_of(j * tq, tq), tq), :] = (_rms(o) * gain).astype(o_ref.dtype)

    per_iter = 2
    while n_blocks % (2 * per_iter) == 0 and 2 * per_iter * n_chunks <= ATTN_MAX_UNROLLED_CHUNKS:
        per_iter *= 2

    def body(jj, carry):
        j0 = per_iter * jj
        for i in range(per_iter):
            cur, nxt = sets[i % 2], sets[(i + 1) % 2]
            m, alpha = steady(cur, carry)
            carry = head(nxt, jnp.minimum(j0 + i + 1, n_blocks - 1))
            tail(cur, j0 + i, m, alpha)
        return carry

    lax.fori_loop(0, n_blocks // per_iter, body, head(sets[0], 0))


def _diff_attention(qt, k, vt, lam_params, subln_g, lam_init):
    b, n_heads, l, _ = k.shape
    n_blocks, _, tq = qt.shape[2:]
    n_chunks, v_rows, tk = vt.shape[2:]
    assert n_blocks % 2 == 0 and n_chunks >= 2
    one_set = ([pltpu.VMEM((tk, 2 * tq), F32)] * 2 + [pltpu.VMEM((tk, 2 * tq), BF16)] * 2
               + [pltpu.VMEM((v_rows, 2 * tq), F32)])
    whole = lambda a: pl.BlockSpec((1, 1) + a.shape[2:], lambda bi, h: (bi, h) + (0,) * (a.ndim - 2))
    return pl.pallas_call(
        functools.partial(_attn_kernel, lam_init=lam_init),
        grid=(b, n_heads),
        in_specs=[
            whole(qt), whole(k), whole(vt),
            pl.BlockSpec(lam_params.shape, lambda bi, h: (0, 0)),
            pl.BlockSpec((1, V_DIM), lambda bi, h: (0, 0)),
        ],
        out_specs=pl.BlockSpec((1, l, V_DIM), lambda bi, h: (bi, 0, h)),
        out_shape=jax.ShapeDtypeStruct((b, l, n_heads * V_DIM), BF16),
        scratch_shapes=one_set * 2,
        compiler_params=_cparams("parallel", "parallel"),
        name="diff_attention",
    )(qt, k, vt, lam_params, subln_g)


def _ssm_prep_kernel(arow_ref, acol_ref, bt_ref, ct_ref, m_ref, win_ref, wout_ref, dec_ref, *, chunk):
    t_len = chunk
    w = t_len * GROUP_CH
    w2 = 2 * w
    arow = arow_ref[0, 0]
    acol = acol_ref[0, 0]
    fwd_lanes = lax.broadcasted_iota(jnp.int32, (1, LANES), 1) < N_STATE

    ar = arow[0:1]
    ai = arow[1:2]
    dt = jnp.exp(arow[2:3])
    zr = dt * ar
    zi = dt * ai
    mag = jnp.exp(zr)
    nr = mag * jnp.cos(zi) - 1.0
    ni = mag * jnp.sin(zi)
    den = ar * ar + ai * ai
    fr = (nr * ar + ni * ai) / den
    fi = (ni * ar - nr * ai) / den
    br = bt_ref[0, 0, 0]
    bi = bt_ref[0, 0, 1]
    bbr = fr * br - fi * bi
    bbi = fr * bi + fi * br

    s_of_row = lax.shift_right_logical(lax.broadcasted_iota(jnp.int32, (w, t_len), 0), GROUP_SHIFT)
    j_of_col = lax.broadcasted_iota(jnp.int32, (w, t_len), 1)
    rep_fwd = _onehot(j_of_col == (t_len - 1 - s_of_row))
    rep_bwd = _onehot(j_of_col == s_of_row)
    n_rows = lax.broadcasted_iota(jnp.int32, (t_len, 1), 0).astype(F32)
    pm = jnp.exp(n_rows * zr)

    def expand(p):
        return jnp.where(fwd_lanes, _select_rows(rep_fwd, p), _select_rows(rep_bwd, p))

    e_re = expand(pm * jnp.cos(n_rows * zi))
    e_im = expand(pm * jnp.sin(n_rows * zi))
    b_re = jnp.tile(bbr, (t_len, 1))
    b_im = jnp.tile(bbi, (t_len, 1))
    win_ref[0, 0, :, :LANES] = (e_re * b_re - e_im * b_im).astype(win_ref.dtype)
    win_ref[0, 0, :, LANES:] = (e_re * b_im + e_im * b_re).astype(win_ref.dtype)
    dm = jnp.exp(t_len * zr)
    dec_ref[0, 0, 0:1, :] = dm * jnp.cos(t_len * zi)
    dec_ref[0, 0, 1:2, :] = dm * jnp.sin(t_len * zi)

    def lane_maps(width):
        lane = lax.broadcasted_iota(jnp.int32, (LANES, width), 1)
        return (lax.shift_right_logical(lane, GROUP_SHIFT), jnp.bitwise_and(lane, GROUP_CH - 1),
                lax.broadcasted_iota(jnp.int32, (LANES, width), 0))

    lag_idx, ch_idx, jrow = lane_maps(w2)
    lag_idx_w, _, jrow_w = lane_maps(w)
    tile_ch = _onehot(jrow == ch_idx)
    n_lanes = jnp.minimum(lax.broadcasted_iota(jnp.int32, (1, LANES), 1), t_len).astype(F32)

    lag_tables = []
    out_tables = []
    for d in range(2):
        dtc = jnp.exp(acol[:, 4 + d:5 + d])
        zrc = dtc * acol[:, d:d + 1]
        zic = dtc * acol[:, 2 + d:3 + d]
        ptm = jnp.exp(zrc * n_lanes)
        pt_re = ptm * jnp.cos(zic * n_lanes)
        pt_im = ptm * jnp.sin(zic * n_lanes)
        c_re = _select_cols(ct_ref[0, 0, d], tile_ch)
        c_im = _select_cols(ct_ref[0, 0, 2 + d], tile_ch)

        def table(rep):
            width = rep.shape[1]
            p_re = _select_cols(pt_re, rep)
            p_im = _select_cols(pt_im, rep)
            cr = c_re[:, :width]
            ci = c_im[:, :width]
            return p_re * cr - p_im * ci, p_re * ci + p_im * cr

        if d == 0:
            power = lag_idx - (t_len - 1)
        else:
            power = (t_len - 1) - lag_idx
        power = jnp.where(power >= 0, power, -1)
        f_re, f_im = table(_onehot(jrow == power))
        lag_tables.append((f_re, f_im))
        if d == 0:
            out_tables.append((f_re[:, w:], f_im[:, w:]))
        else:
            out_tables.append(table(_onehot(jrow_w == (t_len - lag_idx_w))))

    for i, o in enumerate((out_tables[0][0], out_tables[1][0], -out_tables[0][1], -out_tables[1][1])):
        wout_ref[0, 0, i * N_STATE:(i + 1) * N_STATE, :] = o.astype(wout_ref.dtype)

    lhs = jnp.concatenate([bbr, -bbi], axis=1)
    rhs = jnp.concatenate([lag_tables[0][0], lag_tables[1][0], lag_tables[0][1], lag_tables[1][1]], axis=0)
    strip = _dot_f32(lhs, rhs)

    per_tile = LANES // GROUP_CH
    for r in range(per_tile):
        rolled = strip if r == 0 else pltpu.roll(strip, w2 - r * GROUP_CH, 1)
        for s in range(t_len):
            if (t_len - 1 - s) % per_tile == r:
                q = (t_len - 1 - s) // per_tile
                m_ref[0, 0, s * GROUP_CH:(s + 1) * GROUP_CH, :] = (
                    rolled[:, q * LANES:q * LANES + w].astype(m_ref.dtype))


def _ssm_prep(a_re, a_im, log_dt, b_re, b_im, c_re, c_im, chunk):
    depth, _, n_groups, n_state = a_re.shape
    w = chunk * GROUP_CH
    gd = lambda x: jnp.swapaxes(x, 1, 2)
    ldt = jnp.broadcast_to(gd(log_dt)[..., None], (depth, n_groups, 2, n_state))
    packed = lambda x: x.reshape(depth, n_groups, 1, 2 * n_state)
    arow = jnp.concatenate([packed(gd(a_re)), packed(gd(a_im)), packed(ldt),
                            jnp.zeros((depth, n_groups, 5, 2 * n_state), F32)], axis=2)
    acol = jnp.swapaxes(jnp.concatenate([gd(a_re), gd(a_im), ldt, jnp.zeros_like(ldt)], axis=2), 2, 3)
    bt = jnp.stack([gd(b_re), gd(b_im)], axis=2)
    bt = bt.transpose(0, 1, 2, 5, 3, 4).reshape(depth, n_groups, 2, GROUP_CH, 2 * n_state)
    ct = jnp.swapaxes(jnp.concatenate([gd(c_re), gd(c_im)], axis=2), 3, 4)
    ct = jnp.pad(ct, ((0, 0),) * 4 + ((0, LANES - GROUP_CH),))
    blk = lambda *tail: pl.BlockSpec((1, 1) + tail, lambda l, g: (l, g) + (0,) * len(tail))
    return pl.pallas_call(
        functools.partial(_ssm_prep_kernel, chunk=chunk),
        grid=(depth, n_groups),
        in_specs=[blk(8, LANES), blk(n_state, 8), blk(2, GROUP_CH, LANES), blk(4, n_state, LANES)],
        out_specs=[blk(w, w), blk(w, 4 * n_state), blk(4 * n_state, w), blk(2, LANES)],
        out_shape=[
            jax.ShapeDtypeStruct((depth, n_groups, w, w), BF16),
            jax.ShapeDtypeStruct((depth, n_groups, w, 4 * n_state), BF16),
            jax.ShapeDtypeStruct((depth, n_groups, 4 * n_state, w), BF16),
            jax.ShapeDtypeStruct((depth, n_groups, 2, LANES), F32),
        ],
        compiler_params=_cparams("parallel", "parallel"),
        name="ssm_prep",
    )(arow, acol, bt, ct)


def _ssm_state_kernel(u_ref, win_ref, sre_ref, sim_ref):
    s = _dot(u_ref[0], win_ref[0, 0])
    sre_ref[0] = s[:, :LANES]
    sim_ref[0] = s[:, LANES:]


def _ssm_out_kernel(u_ref, hfr_ref, hfi_ref, hbr_ref, hbi_ref, m_ref, wout_ref, y_ref):
    fwd = lax.broadcasted_iota(jnp.int32, hfr_ref.shape[1:], 1) < N_STATE
    h = jnp.concatenate([jnp.where(fwd, hfr_ref[0], hbr_ref[0]),
                         jnp.where(fwd, hfi_ref[0], hbi_ref[0])], axis=1).astype(BF16)
    y_ref[0] = (_dot(u_ref[0], m_ref[0, 0]) + _dot(h, wout_ref[0, 0])).astype(y_ref.dtype)


def _ssm_scan_kernel(sre_ref, sim_ref, dre_ref, dim_ref, hfr_ref, hfi_ref, hbr_ref, hbi_ref,
                     *, batch, n_chunks):
    n_blk, _, lanes = sre_ref.shape
    fwd = jnp.bitwise_and(lax.broadcasted_iota(jnp.int32, (batch, lanes), 1), N_STATE) == 0

    def step(i, carry):
        rows_f = pl.ds(i, batch, stride=n_chunks)
        rows_b = pl.ds(n_chunks - 1 - i, batch, stride=n_chunks)
        out = []
        for q in range(n_blk):
            xr, xi = carry[2 * q], carry[2 * q + 1]
            hfr_ref[q, rows_f, :] = xr
            hfi_ref[q, rows_f, :] = xi
            hbr_ref[q, rows_b, :] = xr
            hbi_ref[q, rows_b, :] = xi
            sr = jnp.where(fwd, sre_ref[q, rows_f, :], sre_ref[q, rows_b, :])
            si = jnp.where(fwd, sim_ref[q, rows_f, :], sim_ref[q, rows_b, :])
            dr = dre_ref[q]
            di = dim_ref[q]
            out += [dr * xr - di * xi + sr, dr * xi + di * xr + si]
        return tuple(out)

    zeros = jnp.zeros((batch, lanes), F32)
    lax.fori_loop(0, n_chunks, step, (zeros,) * (2 * n_blk))


def _scatter_to_groups(scr_ref, ug_ref, chunk):
    n_tiles, tm, _ = scr_ref.shape
    n_ch = tm // chunk
    per_tile = LANES // GROUP_CH
    for s in range(chunk):
        dst = (s % per_tile) * GROUP_CH
        for j in range(n_tiles):
            slab = scr_ref[j, pl.ds(s, n_ch, stride=chunk), :]
            for gg in range(per_tile):
                shift = (dst - gg * GROUP_CH) % LANES
                moved = (pltpu.roll(slab, shift, 1) if shift else slab).astype(ug_ref.dtype)
                ug_ref[j * per_tile + gg, :, s * GROUP_CH:(s + 1) * GROUP_CH] = moved[:, dst:dst + GROUP_CH]


def _gather_from_groups(yg_ref, scr_ref, chunk):
    n_tiles, tm, _ = scr_ref.shape
    n_ch = tm // chunk
    per_tile = LANES // GROUP_CH
    group_of_lane = lax.shift_right_logical(lax.broadcasted_iota(jnp.int32, (n_ch, LANES), 1), GROUP_SHIFT)
    for s in range(chunk):
        src = (s % per_tile) * GROUP_CH
        base = (s // per_tile) * LANES
        for j in range(n_tiles):
            slab = jnp.zeros((n_ch, LANES), F32)
            for gg in range(per_tile):
                v = yg_ref[j * per_tile + gg, :, base:base + LANES].astype(F32)
                shift = (gg * GROUP_CH - src) % LANES
                v = pltpu.roll(v, shift, 1) if shift else v
                slab = jnp.where(group_of_lane == gg, v, slab)
            scr_ref[j, pl.ds(s, n_ch, stride=chunk), :] = slab


def _ssm_branch(ug, layer, m_all, win_all, wout_all, dec_all, batch, seq_len, chunk):
    n_groups, rows, w = ug.shape
    n_chunks = seq_len // chunk
    st = 4 * N_STATE

    state = jax.ShapeDtypeStruct((n_groups, rows, LANES), F32)
    one = lambda g: (g, 0, 0)
    sre, sim = pl.pallas_call(
        _ssm_state_kernel,
        grid=(n_groups,),
        in_specs=[pl.BlockSpec((1, rows, w), one),
                  pl.BlockSpec((1, 1, w, st), lambda g: (layer, g, 0, 0))],
        out_specs=[pl.BlockSpec((1, rows, LANES), one)] * 2,
        out_shape=[state] * 2,
        compiler_params=_cparams("parallel"),
        name="ssm_chunk_state",
    )(ug, win_all)

    dec = dec_all[layer]
    gb = min(SCAN_GROUPS, n_groups)
    hs = pl.pallas_call(
        functools.partial(_ssm_scan_kernel, batch=batch, n_chunks=n_chunks),
        grid=(n_groups // gb,),
        in_specs=[pl.BlockSpec((gb, rows, LANES), one)] * 2 + [pl.BlockSpec((gb, 1, LANES), one)] * 2,
        out_specs=[pl.BlockSpec((gb, rows, LANES), one)] * 4,
        out_shape=[state] * 4,
        compiler_params=_cparams("parallel"),
        name="ssm_chunk_scan",
    )(sre, sim, dec[:, 0:1, :], dec[:, 1:2, :])

    return pl.pallas_call(
        _ssm_out_kernel,
        grid=(n_groups,),
        in_specs=[pl.BlockSpec((1, rows, w), one)]
                 + [pl.BlockSpec((1, rows, LANES), one)] * 4
                 + [pl.BlockSpec((1, 1, w, w), lambda g: (layer, g, 0, 0)),
                    pl.BlockSpec((1, 1, st, w), lambda g: (layer, g, 0, 0))],
        out_specs=pl.BlockSpec((1, rows, w), one),
        out_shape=jax.ShapeDtypeStruct((n_groups, rows, w), BF16),
        compiler_params=_cparams("parallel"),
        name="ssm_chunk_output",
    )(ug, *hs, m_all, wout_all)


def _merge_kernel(x_ref, mod_ref, o_ref, yg_ref, u_ref, ga_ref, gs_ref, d_ref,
                  wglu_ref, bglu_ref, wattn_ref, wo_ref, out_ref, y_scr_ref):
    d = x_ref.shape[1]
    _gather_from_groups(yg_ref, y_scr_ref, SSM_CHUNK)
    y = jnp.concatenate([y_scr_ref[j] for j in range(y_scr_ref.shape[0])], axis=1)
    y = y + d_ref[...] * u_ref[...].astype(F32)
    z = jax.nn.gelu(y, approximate=True).astype(BF16)
    glu = _dot(z, wglu_ref[...]) + bglu_ref[...]
    y_ssm = glu[:, :d] * jax.nn.sigmoid(glu[:, d:])
    y_attn = _dot(o_ref[...], wattn_ref[...])
    merged = (ga_ref[...].astype(F32) * y_attn + gs_ref[...].astype(F32) * y_ssm).astype(BF16)
    out_ref[...] = x_ref[...] + mod_ref[0][2:3] * _dot(merged, wo_ref[...])


def _merge(x, mod, o, yg, u, ga, gs, ssm_d, w_glu, b_glu, w_attn, w_o, layer, seq_len, in_place):
    n, d = x.shape
    tm = min(PROJ_TILE, seq_len)
    tps = seq_len // tm
    tok = lambda i: (i, 0)
    full = lambda a: pl.BlockSpec(a.shape, lambda i: (0, 0))
    of_layer = lambda a: pl.BlockSpec((None,) + a.shape[1:], lambda i: (layer, 0, 0))
    return pl.pallas_call(
        _merge_kernel,
        grid=(n // tm,),
        in_specs=[
            pl.BlockSpec((tm, d), tok),
            pl.BlockSpec((1, 6, d), lambda i: (i // tps, 0, 0)),
            pl.BlockSpec((tm, o.shape[1]), tok),
            pl.BlockSpec((yg.shape[0], tm // SSM_CHUNK, yg.shape[2]), lambda i: (0, i, 0)),
            pl.BlockSpec((tm, u.shape[1]), tok),
            pl.BlockSpec((tm, d), tok),
            pl.BlockSpec((tm, d), tok),
            full(ssm_d), of_layer(w_glu), full(b_glu), of_layer(w_attn), of_layer(w_o),
        ],
        out_specs=pl.BlockSpec((tm, d), tok),
        out_shape=jax.ShapeDtypeStruct((n, d), F32),
        scratch_shapes=[pltpu.VMEM((u.shape[1] // LANES, tm, LANES), F32)],
        input_output_aliases={0: 0} if in_place else {},
        compiler_params=_cparams("parallel"),
        name="merge_out_projection",
    )(x, mod, o, yg, u, ga, gs, ssm_d, w_glu, b_glu, w_attn, w_o)


def _ffn_kernel(x_ref, mod_ref, g_ref, win_ref, wd_ref, fg_ref, out_ref, h_ref, t_ref, *, final_norm):
    mod = mod_ref[0]
    d_ff = wd_ref.shape[0]
    h_ref[...] = (_rms(x_ref[...]) * g_ref[...] * (1.0 + mod[4:5]) + mod[3:4]).astype(BF16)
    for c in range(d_ff // FFN_CHUNK):
        cols = slice(c * FFN_CHUNK, (c + 1) * FFN_CHUNK)
        gate = _dot(h_ref[...], win_ref[:, cols])
        up = _dot(h_ref[...], win_ref[:, d_ff + c * FFN_CHUNK:d_ff + (c + 1) * FFN_CHUNK])
        t_ref[:, cols] = (gate * jax.nn.sigmoid(gate) * up).astype(BF16)
    xn = x_ref[...] + mod[5:6] * _dot(t_ref[...], wd_ref[...])
    if final_norm:
        xn = _rms(xn) * fg_ref[...]
    out_ref[...] = xn


def _ffn(x, mod, g, w_in, w_out, final_g, layer, seq_len, final_norm):
    n, d = x.shape
    d_ff = w_out.shape[1]
    assert d_ff % FFN_CHUNK == 0
    tm = min(TOKEN_TILE, seq_len)
    tps = seq_len // tm
    return pl.pallas_call(
        functools.partial(_ffn_kernel, final_norm=final_norm),
        grid=(n // tm,),
        in_specs=[
            pl.BlockSpec((tm, d), lambda i: (i, 0)),
            pl.BlockSpec((1, 6, d), lambda i: (i // tps, 0, 0)),
            pl.BlockSpec((1, d), lambda i: (0, 0)),
            pl.BlockSpec((None, d, 2 * d_ff), lambda i: (layer, 0, 0)),
            pl.BlockSpec((None, d_ff, d), lambda i: (layer, 0, 0)),
            pl.BlockSpec((1, d), lambda i: (0, 0)),
        ],
        out_specs=pl.BlockSpec((tm, d), lambda i: (i, 0)),
        out_shape=jax.ShapeDtypeStruct((n, d), F32),
        scratch_shapes=[pltpu.VMEM((tm, d), BF16), pltpu.VMEM((tm, d_ff), BF16)],
        input_output_aliases={0: 0},
        compiler_params=_cparams("parallel"),
        name="ffn",
    )(x, mod, g, w_in, w_out, final_g)


def _rope_tables(seq_len):
    inv = 1.0 / (ROPE_THETA ** (jnp.arange(0, HEAD_DIM, 2, dtype=F32) / HEAD_DIM))
    ang = jnp.arange(seq_len, dtype=F32)[:, None] * inv[None, :]
    cos = jnp.cos(ang)
    sin = jnp.sin(ang)
    reps = LANES // HEAD_DIM
    cos_t = jnp.tile(jnp.concatenate([cos, cos], axis=-1), (1, reps))
    sin_t = jnp.tile(jnp.concatenate([-sin, sin], axis=-1), (1, reps))
    return cos_t, sin_t


def _lambda_init(layer):
    return 0.8 - 0.6 * math.exp(-0.3 * layer)


def _trunk(x, mod_all, p, ssm_ops):
    batch, seq_len, d = x.shape
    depth = p["w_in"].shape[0]
    d_v = N_HEADS * V_DIM
    d_u = p["ssm_d"].shape[1]
    for tile in (PROJ_TILE, TOKEN_TILE, 2 * ATTN_TQ, 2 * ATTN_TK, SSM_CHUNK):
        assert seq_len % min(tile, seq_len) == 0, (seq_len, tile)
    cos, sin = _rope_tables(seq_len)
    x = x.reshape(batch * seq_len, d)
    row = lambda a: a.reshape(1, -1)
    for i in range(depth):
        mod = mod_all[i].reshape(batch, 6, d)
        qt, k, vt, u, ug, ga, gs = _in_projection(x, mod, row(p["norm1_g"][i]), cos, sin, p["w_in"], i,
                                              batch, seq_len, d_u)
        lam_params = jnp.stack([p["lam_q1"][i], p["lam_k1"][i], p["lam_q2"][i], p["lam_k2"][i]])
        o = _diff_attention(qt, k, vt, lam_params, row(p["subln_g"][i]), _lambda_init(i))
        yg = _ssm_branch(ug, i, *ssm_ops, batch, seq_len, SSM_CHUNK)
        x = _merge(x, mod, o.reshape(batch * seq_len, d_v), yg, u, ga, gs, row(p["ssm_d"][i]),
                   p["w_glu"], row(p["b_glu"][i]), p["w_attn_br"], p["w_o"], i, seq_len,
                   in_place=(i > 0))
        x = _ffn(x, mod, row(p["norm2_g"][i]), p["w_ffn_in"], p["w_ffn_out"],
                 row(p["final_g"]), i, seq_len, final_norm=(i == depth - 1))
    return x.reshape(batch, seq_len, d)


def kernel(x_prompt, x_sample, c_prompt, c_sample, w_mod, b_mod, norm1_g, w_in, lam_q1, lam_k1, lam_q2, lam_k2, subln_g, w_attn_br, ssm_a_re, ssm_a_im, ssm_log_dt, ssm_b_re, ssm_b_im, ssm_c_re, ssm_c_im, ssm_d, w_glu, b_glu, w_o, norm2_g, w_ffn_in, w_ffn_out, final_g):
    bp, bs = c_prompt.shape[0], c_sample.shape[0]
    pad = -(bp + bs) % 8
    c_all = jnp.concatenate([c_prompt, c_sample, jnp.zeros((pad, c_prompt.shape[1]), F32)], axis=0)
    mod_all = _modulation(c_all, w_mod, b_mod)
    ssm_ops = _ssm_prep(ssm_a_re, ssm_a_im, ssm_log_dt, ssm_b_re, ssm_b_im, ssm_c_re, ssm_c_im, SSM_CHUNK)
    p = dict(
        norm1_g=norm1_g, w_in=w_in.astype(BF16), lam_q1=lam_q1, lam_k1=lam_k1, lam_q2=lam_q2, lam_k2=lam_k2,
        subln_g=subln_g, w_attn_br=w_attn_br.astype(BF16), ssm_d=ssm_d, w_glu=w_glu.astype(BF16),
        b_glu=b_glu, w_o=w_o.astype(BF16), norm2_g=norm2_g, w_ffn_in=w_ffn_in.astype(BF16),
        w_ffn_out=w_ffn_out.astype(BF16), final_g=final_g)
    y_prompt = _trunk(x_prompt, mod_all[:, :bp], p, ssm_ops)
    y_sample = _trunk(x_sample, mod_all[:, bp:bp + bs], p, ssm_ops)
    return (y_prompt, y_sample)
```
